```python
import jax
import jax.numpy as jnp
from jax import lax
import numpy as np

D_MODEL = 1024
BATCH = 8
SEQ = 2048
DEPTH = 2
DEC_BATCH = 32
DEC_SEQ = 1
PAST_LEN = 16384
PAGE_SIZE = 128

MLA_HEADS = 4
QK_NOPE = 64
QK_ROPE = 32
QK_HEAD = QK_NOPE + QK_ROPE
V_HEAD = 128
Q_RANK = 384
KV_RANK = 256
ROPE_THETA = 10000.0
GLA_HEADS = 4
GLA_DK = 32
GLA_DV = 64
GLA_GATE_RANK = 16
GLA_TAU = 16.0
GLA_CHUNK = 64
POOL_WINDOWS = (2, 4, 8, 16)
POOL_GROUP = 64
POOL_WIDTH = POOL_GROUP * len(POOL_WINDOWS)
POOL_BUF = max(POOL_WINDOWS) - 1
MLA_OUT = MLA_HEADS * V_HEAD
GLA_OUT = GLA_HEADS * GLA_DV
D_MIX = MLA_OUT + GLA_OUT + POOL_WIDTH
IN_SIZES = (Q_RANK, KV_RANK, QK_ROPE, GLA_HEADS * GLA_DK, GLA_HEADS * GLA_DK, GLA_HEADS * GLA_DV, GLA_GATE_RANK, GLA_OUT, POOL_WIDTH)
D_IN = sum(IN_SIZES)
N_MEM = 256
XA_HEADS = 4
XA_HEAD = D_MODEL // XA_HEADS
D_FF = 4 * D_MODEL
Q_BLOCK = 128
EPS = 1e-6

kernel_name = 'hybrid_mla_gla_pool_decoder_step'


def _rmsnorm(x, g):
    xf = x.astype(jnp.float32)
    y = xf * lax.rsqrt(jnp.mean(xf * xf, axis=-1, keepdims=True) + EPS)
    return (y * g.astype(jnp.float32)).astype(x.dtype)


def _rope(x, pos):
    half = x.shape[-1] // 2
    inv = ROPE_THETA ** (-jnp.arange(half, dtype=jnp.float32) / half)
    ang = pos.astype(jnp.float32)[:, None] * inv[None, :]
    cos = jnp.cos(ang)[:, None, :]
    sin = jnp.sin(ang)[:, None, :]
    xf = x.astype(jnp.float32)
    x1, x2 = xf[..., :half], xf[..., half:]
    return jnp.concatenate([x1 * cos - x2 * sin, x2 * cos + x1 * sin], axis=-1).astype(x.dtype)


def _split_in(z):
    parts, off = [], 0
    for size in IN_SIZES:
        parts.append(z[..., off:off + size])
        off += size
    return parts


def _causal_attend(q, k, v, q_pos, k_pos, scale):
    def attend(qb, qpb):
        s = jnp.einsum('bqhd,bkhd->bhqk', qb, k, preferred_element_type=jnp.float32) * scale
        mask = k_pos[None, :] <= qpb[:, None]
        s = jnp.where(mask[None, None], s, -jnp.inf)
        p = jax.nn.softmax(s, axis=-1).astype(v.dtype)
        return jnp.einsum('bhqk,bkhd->bqhd', p, v)
    B, Tq, H, dq = q.shape
    dv = v.shape[-1]
    if Tq > Q_BLOCK and Tq % Q_BLOCK == 0:
        nb = Tq // Q_BLOCK
        qb = q.reshape(B, nb, Q_BLOCK, H, dq).transpose(1, 0, 2, 3, 4)
        pb = q_pos.reshape(nb, Q_BLOCK)
        out = lax.map(lambda a: attend(a[0], a[1]), (qb, pb))
        return out.transpose(1, 0, 2, 3, 4).reshape(B, Tq, H, dv)
    return attend(q, q_pos)


def _mla(c_q, c_kv, k_rope_raw, pos_new, past_latent, past_krope, pos_past,
         g_q_lat, w_uq, g_kv_lat, w_uk, w_uv, g_qk_q, g_qk_k):
    B, T, _ = c_q.shape
    latent_new = _rmsnorm(c_kv, g_kv_lat)
    q = jnp.einsum('btr,rhd->bthd', _rmsnorm(c_q, g_q_lat), w_uq)
    q = _rmsnorm(q, g_qk_q)
    q = jnp.concatenate([q[..., :QK_NOPE], _rope(q[..., QK_NOPE:], pos_new)], axis=-1)
    latent = jnp.concatenate([past_latent.astype(latent_new.dtype), latent_new], axis=1)
    krope = jnp.concatenate([past_krope.astype(k_rope_raw.dtype), k_rope_raw], axis=1)
    pos_k = jnp.concatenate([pos_past, pos_new])
    S = latent.shape[1]
    k_nope = jnp.einsum('bsc,chd->bshd', latent, w_uk)
    v = jnp.einsum('bsc,chd->bshd', latent, w_uv)
    k = jnp.concatenate([k_nope, jnp.broadcast_to(krope[:, :, None, :], (B, S, MLA_HEADS, QK_ROPE))], axis=-1)
    k = _rmsnorm(k, g_qk_k)
    k = jnp.concatenate([k[..., :QK_NOPE], _rope(k[..., QK_NOPE:], pos_k)], axis=-1)
    o = _causal_attend(q, k, v, pos_new, pos_k, QK_HEAD ** -0.5)
    return o.reshape(B, T, MLA_OUT), latent_new


def _gla_scan(q, k, v, log_a, s0):
    B, T, H, DK = q.shape
    DV = v.shape[-1]
    C = min(GLA_CHUNK, T)
    n = -(-T // C)
    pad = n * C - T

    def chunks(a):
        a = jnp.pad(a.astype(jnp.float32), ((0, 0), (0, pad), (0, 0), (0, 0)))
        return a.reshape(B, n, C, H, a.shape[-1]).transpose(1, 0, 3, 2, 4)

    causal = jnp.tril(jnp.ones((C, C), dtype=bool))

    def step(S, inp):
        qc, kc, vc, ac = inp
        b = jnp.cumsum(ac, axis=2)
        b_last = b[:, :, -1:, :]
        q_dec = qc * jnp.exp(b)
        k_dec = kc * jnp.exp(-b)
        att = jnp.where(causal, jnp.einsum('bhtk,bhsk->bhts', q_dec, k_dec), 0.0)
        o = jnp.einsum('bhts,bhsv->bhtv', att, vc) + jnp.einsum('bhtk,bhkv->bhtv', q_dec, S)
        S = jnp.exp(b_last[:, :, 0, :])[..., None] * S + jnp.einsum('bhsk,bhsv->bhkv', kc * jnp.exp(b_last - b), vc)
        return S, o

    s_fin, o = lax.scan(step, s0.astype(jnp.float32), (chunks(q), chunks(k), chunks(v), chunks(log_a)))
    o = o.transpose(1, 0, 3, 2, 4).reshape(B, n * C, H, DV)[:, :T]
    return o, s_fin


def _gla(gq, gk, gv, g_low, g_r, s0, w_gate_up, b_gate, g_gla_out):
    B, T, _ = gq.shape
    q = gq.reshape(B, T, GLA_HEADS, GLA_DK) * (GLA_DK ** -0.5)
    k = gk.reshape(B, T, GLA_HEADS, GLA_DK)
    v = gv.reshape(B, T, GLA_HEADS, GLA_DV)
    log_a = jax.nn.log_sigmoid((g_low @ w_gate_up + b_gate).astype(jnp.float32)) / GLA_TAU
    log_a = log_a.reshape(B, T, GLA_HEADS, GLA_DK)
    o, s_new = _gla_scan(q, k, v, log_a, s0)
    o = _rmsnorm(o.astype(gv.dtype), g_gla_out).reshape(B, T, GLA_OUT)
    return o * jax.nn.silu(g_r), s_new


def _pool_mix(u_ext, n_prev, start_pos, w_pool, pool_scale):
    B, L, W = u_ext.shape
    T = L - n_prev
    cs = jnp.cumsum(u_ext.astype(jnp.float32), axis=1)
    cs = jnp.concatenate([jnp.zeros((B, 1, W), jnp.float32), cs], axis=1)
    hi = n_prev + jnp.arange(T) + 1
    pos = start_pos + jnp.arange(T)
    u_new = u_ext[:, n_prev:].astype(jnp.float32)
    outs = []
    for gi, w in enumerate(POOL_WINDOWS):
        lo = jnp.maximum(hi - w, 0)
        c0, c1 = gi * POOL_GROUP, (gi + 1) * POOL_GROUP
        s = cs[:, hi, c0:c1] - cs[:, lo, c0:c1]
        cnt = jnp.minimum(w, pos + 1).astype(jnp.float32)
        outs.append(s / cnt[None, :, None] - u_new[:, :, c0:c1])
    pooled = jnp.stack(outs, axis=2)
    y = jnp.einsum('btgc,gcd->btgd', pooled, w_pool.astype(jnp.float32)).reshape(B, T, POOL_WIDTH)
    return (y * pool_scale.astype(jnp.float32)).astype(u_ext.dtype)


def _memory_kv(mem, norm_mem, w_xk, w_xv, g_xk):
    B, N, _ = mem.shape
    m = _rmsnorm(mem, norm_mem)
    k = _rmsnorm((m @ w_xk).reshape(B, N, XA_HEADS, XA_HEAD), g_xk)
    v = (m @ w_xv).reshape(B, N, XA_HEADS, XA_HEAD)
    return k, v


def _cross_attend(h, mem_k, mem_v, w_xq, g_xq, w_xo):
    B, T, _ = h.shape
    q = _rmsnorm((h @ w_xq).reshape(B, T, XA_HEADS, XA_HEAD), g_xq)
    s = jnp.einsum('bthd,bnhd->bhtn', q, mem_k, preferred_element_type=jnp.float32) * (XA_HEAD ** -0.5)
    p = jax.nn.softmax(s, axis=-1).astype(mem_v.dtype)
    o = jnp.einsum('bhtn,bnhd->bthd', p, mem_v)
    return o.reshape(B, T, XA_HEADS * XA_HEAD) @ w_xo


def _layer(x, mem_k, mem_v, pos_new, past_latent, past_krope, pos_past, gla_s0, pool_buf, start_pos, lw):
    (norm_mix, w_in, g_q_lat, w_uq, g_kv_lat, w_uk, w_uv, g_qk_q, g_qk_k,
     w_gate_up, b_gate, g_gla_out, w_pool, pool_scale, w_out,
     norm_xa, w_xq, g_xq, w_xo, norm_mlp, w_ff1, w_ff2) = lw
    h = _rmsnorm(x, norm_mix)
    c_q, c_kv, k_rope, gq, gk, gv, g_low, g_r, u = _split_in(h @ w_in)
    mla_o, latent_new = _mla(c_q, c_kv, k_rope, pos_new, past_latent, past_krope, pos_past,
                             g_q_lat, w_uq, g_kv_lat, w_uk, w_uv, g_qk_q, g_qk_k)
    gla_o, gla_s = _gla(gq, gk, gv, g_low, g_r, gla_s0, w_gate_up, b_gate, g_gla_out)
    u_ext = jnp.concatenate([pool_buf.astype(u.dtype), u], axis=1)
    pool_o = _pool_mix(u_ext, pool_buf.shape[1], start_pos, w_pool, pool_scale)
    mix = jnp.concatenate([mla_o, gla_o.astype(mla_o.dtype), pool_o.astype(mla_o.dtype)], axis=-1)
    x = x + mix @ w_out
    x = x + _cross_attend(_rmsnorm(x, norm_xa), mem_k, mem_v, w_xq, g_xq, w_xo)
    hm = _rmsnorm(x, norm_mlp)
    x = x + jnp.square(jax.nn.relu(hm @ w_ff1)) @ w_ff2
    return x, latent_new, k_rope, gla_s, u_ext[:, -POOL_BUF:]


def setup_inputs(seed: int = 0) -> dict:
    key = jax.random.key(seed)
    ks = iter(jax.random.split(key, 64))
    f32 = jnp.float32

    def nrm(shape, scale=1.0):
        return jax.random.normal(next(ks), shape, f32) * scale

    def gain(shape):
        return 1.0 + 0.02 * jax.random.normal(next(ks), shape, f32)

    n_pages = PAST_LEN // PAGE_SIZE
    n_phys = (DEC_BATCH * n_pages * 5) // 4
    page_table = jax.random.permutation(next(ks), n_phys)[: DEC_BATCH * n_pages].reshape(DEC_BATCH, n_pages).astype(jnp.int32)
    inputs = {
        'x_prompt': nrm((BATCH, SEQ, D_MODEL)),
        'x_sample': nrm((DEC_BATCH, DEC_SEQ, D_MODEL)),
        'mem_prompt': nrm((BATCH, N_MEM, D_MODEL)),
        'cache_mla_latent': nrm((n_phys, DEPTH, PAGE_SIZE, KV_RANK)),
        'cache_mla_krope': nrm((n_phys, DEPTH, PAGE_SIZE, QK_ROPE)),
        'state_gla': nrm((DEC_BATCH, DEPTH, GLA_HEADS, GLA_DK, GLA_DV)),
        'state_pool': nrm((DEC_BATCH, DEPTH, POOL_BUF, POOL_WIDTH)),
        'cache_mem_k': nrm((DEC_BATCH, DEPTH, N_MEM, XA_HEADS, XA_HEAD)),
        'cache_mem_v': nrm((DEC_BATCH, DEPTH, N_MEM, XA_HEADS, XA_HEAD)),
        'page_table': page_table,
        'norm_mix': gain((DEPTH, D_MODEL)),
        'w_in': nrm((DEPTH, D_MODEL, D_IN), D_MODEL ** -0.5),
        'g_q_lat': gain((DEPTH, Q_RANK)),
        'w_uq': nrm((DEPTH, Q_RANK, MLA_HEADS, QK_HEAD), Q_RANK ** -0.5),
        'g_kv_lat': gain((DEPTH, KV_RANK)),
        'w_uk': nrm((DEPTH, KV_RANK, MLA_HEADS, QK_NOPE), KV_RANK ** -0.5),
        'w_uv': nrm((DEPTH, KV_RANK, MLA_HEADS, V_HEAD), KV_RANK ** -0.5),
        'g_qk_q': gain((DEPTH, QK_HEAD)),
        'g_qk_k': gain((DEPTH, QK_HEAD)),
        'w_gate_up': nrm((DEPTH, GLA_GATE_RANK, GLA_HEADS * GLA_DK), GLA_GATE_RANK ** -0.5),
        'b_gate': nrm((DEPTH, GLA_HEADS * GLA_DK), 0.1),
        'g_gla_out': gain((DEPTH, GLA_DV)),
        'w_pool': nrm((DEPTH, len(POOL_WINDOWS), POOL_GROUP, POOL_GROUP), POOL_GROUP ** -0.5),
        'pool_scale': gain((DEPTH, POOL_WIDTH)),
        'w_out': nrm((DEPTH, D_MIX, D_MODEL), D_MIX ** -0.5),
        'norm_xa': gain((DEPTH, D_MODEL)),
        'norm_mem': gain((DEPTH, D_MODEL)),
        'w_xq': nrm((DEPTH, D_MODEL, D_MODEL), D_MODEL ** -0.5),
        'w_xk': nrm((DEPTH, D_MODEL, D_MODEL), D_MODEL ** -0.5),
        'w_xv': nrm((DEPTH, D_MODEL, D_MODEL), D_MODEL ** -0.5),
        'g_xq': gain((DEPTH, XA_HEAD)),
        'g_xk': gain((DEPTH, XA_HEAD)),
        'w_xo': nrm((DEPTH, D_MODEL, D_MODEL), D_MODEL ** -0.5),
        'norm_mlp': gain((DEPTH, D_MODEL)),
        'w_ff1': nrm((DEPTH, D_MODEL, D_FF), D_MODEL ** -0.5),
        'w_ff2': nrm((DEPTH, D_FF, D_MODEL), D_FF ** -0.5),
    }
    return inputs


def reference(x_prompt, x_sample, mem_prompt, cache_mla_latent, cache_mla_krope, state_gla, state_pool,
              cache_mem_k, cache_mem_v, page_table, norm_mix, w_in, g_q_lat, w_uq, g_kv_lat, w_uk, w_uv,
              g_qk_q, g_qk_k, w_gate_up, b_gate, g_gla_out, w_pool, pool_scale, w_out, norm_xa, norm_mem,
              w_xq, w_xk, w_xv, g_xq, g_xk, w_xo, norm_mlp, w_ff1, w_ff2):
    n_prompt, seq, _ = x_prompt.shape
    n_dec, dec_seq, _ = x_sample.shape
    n_pages = page_table.shape[1]
    past = n_pages * PAGE_SIZE
    dt = x_prompt.dtype
    pos_prompt = jnp.arange(seq, dtype=jnp.int32)
    pos_sample = past + jnp.arange(dec_seq, dtype=jnp.int32)
    pos_past = jnp.arange(past, dtype=jnp.int32)
    no_pos = jnp.zeros((0,), jnp.int32)
    no_latent = jnp.zeros((n_prompt, 0, KV_RANK), dt)
    no_krope = jnp.zeros((n_prompt, 0, QK_ROPE), dt)
    no_buf = jnp.zeros((n_prompt, 0, POOL_WIDTH), dt)
    gla_zero = jnp.zeros((n_prompt, GLA_HEADS, GLA_DK, GLA_DV), jnp.float32)
    y_p, y_s = x_prompt, x_sample
    p_lat, p_rope, p_gla, p_pool, p_mk, p_mv = [], [], [], [], [], []
    s_lat, s_rope, s_gla, s_pool = [], [], [], []
    for l in range(DEPTH):
        lw = (norm_mix[l], w_in[l], g_q_lat[l], w_uq[l], g_kv_lat[l], w_uk[l], w_uv[l], g_qk_q[l], g_qk_k[l],
              w_gate_up[l], b_gate[l], g_gla_out[l], w_pool[l], pool_scale[l], w_out[l],
              norm_xa[l], w_xq[l], g_xq[l], w_xo[l], norm_mlp[l], w_ff1[l], w_ff2[l])
        mk, mv = _memory_kv(mem_prompt, norm_mem[l], w_xk[l], w_xv[l], g_xk[l])
        y_p, lat, kr, gs, buf = _layer(y_p, mk, mv, pos_prompt, no_latent, no_krope, no_pos, gla_zero, no_buf, 0, lw)
        p_lat.append(lat)
        p_rope.append(kr)
        p_gla.append(gs)
        p_pool.append(buf)
        p_mk.append(mk)
        p_mv.append(mv)
        past_lat = cache_mla_latent[page_table, l].reshape(n_dec, past, KV_RANK)
        past_kr = cache_mla_krope[page_table, l].reshape(n_dec, past, QK_ROPE)
        y_s, lat, kr, gs, buf = _layer(y_s, cache_mem_k[:, l], cache_mem_v[:, l], pos_sample, past_lat, past_kr,
                                       pos_past, state_gla[:, l], state_pool[:, l], past, lw)
        s_lat.append(lat)
        s_rope.append(kr)
        s_gla.append(gs)
        s_pool.append(buf)
    return (y_p, y_s,
            jnp.stack(p_lat, axis=1), jnp.stack(p_rope, axis=1), jnp.stack(p_gla, axis=1), jnp.stack(p_pool, axis=1),
            jnp.stack(p_mk, axis=1), jnp.stack(p_mv, axis=1),
            jnp.stack(s_lat, axis=1), jnp.stack(s_rope, axis=1), jnp.stack(s_gla, axis=1), jnp.stack(s_pool, axis=1))
```

```python
import functools

import jax
import jax.numpy as jnp
from jax import lax
from jax.experimental import pallas as pl
from jax.experimental.pallas import tpu as pltpu

F32 = jnp.float32
BF16 = jnp.bfloat16

D_MODEL = 1024
MLA_HEADS = 4
QK_NOPE = 64
QK_ROPE = 32
QK_HEAD = QK_NOPE + QK_ROPE
V_HEAD = 128
Q_RANK = 384
KV_RANK = 256
ROPE_THETA = 10000.0
GLA_HEADS = 4
GLA_DK = 32
GLA_DV = 64
GLA_GATE_RANK = 16
GLA_TAU = 16.0
GLA_CHUNK = 64
POOL_WINDOWS = (2, 4, 8, 16)
POOL_GROUP = 64
POOL_WIDTH = POOL_GROUP * len(POOL_WINDOWS)
POOL_BUF = max(POOL_WINDOWS) - 1
MLA_OUT = MLA_HEADS * V_HEAD
GLA_OUT = GLA_HEADS * GLA_DV
N_MEM = 256
XA_HEADS = 4
XA_HEAD = D_MODEL // XA_HEADS
D_FF = 4 * D_MODEL
PAGE_SIZE = 128
EPS = 1e-6

HEAD_PAD = 128
QK_W = MLA_HEADS * HEAD_PAD
GLA_QK = GLA_HEADS * GLA_DK
IN_PAD = 1792
MISC0 = 1664

MIB = 1024 * 1024


def _cparams(sem, vmem_mib):
    return pltpu.CompilerParams(dimension_semantics=sem, vmem_limit_bytes=vmem_mib * MIB)


def _rms(x, g):
    y = x * lax.rsqrt(jnp.mean(x * x, axis=-1, keepdims=True) + EPS)
    return y * g


def _head_rms(x, g, nh, width, count):
    outs = []
    for h in range(nh):
        p = x[:, h * width:(h + 1) * width]
        ms = jnp.sum(p * p, axis=-1, keepdims=True) * (1.0 / count)
        outs.append(p * lax.rsqrt(ms + EPS) * g[:, h * width:(h + 1) * width])
    return jnp.concatenate(outs, axis=1)


def _dot(a, b):
    return jnp.dot(a, b, preferred_element_type=F32)


def _dot_nt(a, b):
    return lax.dot_general(a, b, (((1,), (1,)), ((), ())), preferred_element_type=F32)


def _split_bf16(x):
    hi = x.astype(BF16)
    lo = (x - hi.astype(F32)).astype(BF16)
    return hi, lo


def _dot_split(x, w):
    hi, lo = _split_bf16(x)
    return _dot(hi, w) + _dot(lo, w)


def _rope_rows(x, c, s1, s2):
    n = x.shape[1]
    c4 = jnp.concatenate([c] * MLA_HEADS, axis=1)
    s14 = jnp.concatenate([s1] * MLA_HEADS, axis=1)
    s24 = jnp.concatenate([s2] * MLA_HEADS, axis=1)
    half = QK_ROPE // 2
    return x * c4 + pltpu.roll(x, n - half, 1) * s14 + pltpu.roll(x, half, 1) * s24


def _log_sigmoid(x):
    return -(jnp.maximum(-x, 0.0) + jnp.log1p(jnp.exp(-jnp.abs(x))))


def _sigmoid(x):
    return 1.0 / (1.0 + jnp.exp(-x))


def _memkv_kernel(mem_ref, g_ref, wk_ref, wv_ref, gk_ref, k_ref, v_ref, kb_ref, vb_ref):
    m = _rms(mem_ref[...], g_ref[...]).astype(BF16)
    k = _head_rms(_dot(m, wk_ref[...]), gk_ref[...], XA_HEADS, XA_HEAD, XA_HEAD)
    v = _dot(m, wv_ref[...])
    k_ref[...] = k
    v_ref[...] = v
    kb_ref[...] = k.astype(BF16)
    vb_ref[...] = v.astype(BF16)


def _memory_kv(mem2, g, wk, wv, gk):
    m = mem2.shape[0]
    tm = 512
    row = lambda i: (i, 0)
    fix = lambda i: (0, 0)
    return pl.pallas_call(
        _memkv_kernel,
        grid=(m // tm,),
        in_specs=[pl.BlockSpec((tm, D_MODEL), row), pl.BlockSpec((1, D_MODEL), fix),
                  pl.BlockSpec((D_MODEL, D_MODEL), fix), pl.BlockSpec((D_MODEL, D_MODEL), fix),
                  pl.BlockSpec((1, D_MODEL), fix)],
        out_specs=[pl.BlockSpec((tm, D_MODEL), row)] * 4,
        out_shape=[jax.ShapeDtypeStruct((m, D_MODEL), F32)] * 2
        + [jax.ShapeDtypeStruct((m, D_MODEL), BF16)] * 2,
        compiler_params=_cparams(("parallel",), 40),
        name="memory_kv",
    )(mem2, g, wk, wv, gk)


def _inproj_kernel(x_ref, gmix_ref, win_ref, gql_ref, wuq_ref, gkv_ref, wuk_ref, wuv_ref,
                   gq_ref, gk_ref, place_ref, wg_ref, bg_ref, c_ref, s1_ref, s2_ref,
                   lat_ref, kr_ref, q_ref, k_ref, v_ref, gq_o, gk_o, gv_o, la_o, gr_o, u_o):
    h = _rms(x_ref[...], gmix_ref[...]).astype(BF16)
    z = _dot(h, win_ref[...])
    c_q = z[:, 0:384]
    c_kv = z[:, 384:640]
    misc = z[:, MISC0:MISC0 + 128]
    gq_o[...] = z[:, 640:768]
    gk_o[...] = z[:, 768:896]
    gv_o[...] = z[:, 896:1152]
    gr_o[...] = z[:, 1152:1408]
    u_o[...] = z[:, 1408:1664]
    kr_ref[...] = misc[:, 0:QK_ROPE]

    lat = _rms(c_kv, gkv_ref[...])
    lat_ref[...] = lat
    latb = lat.astype(BF16)
    c, s1, s2 = c_ref[...], s1_ref[...], s2_ref[...]

    q = _dot(_rms(c_q, gql_ref[...]).astype(BF16), wuq_ref[...])
    q = _head_rms(q, gq_ref[...], MLA_HEADS, HEAD_PAD, QK_HEAD)
    q_ref[...] = (_rope_rows(q, c, s1, s2) * (QK_HEAD ** -0.5)).astype(BF16)

    k = _dot(latb, wuk_ref[...]) + _dot_split(misc, place_ref[...])
    k = _head_rms(k, gk_ref[...], MLA_HEADS, HEAD_PAD, QK_HEAD)
    k_ref[...] = _rope_rows(k, c, s1, s2).astype(BF16)
    v_ref[...] = _dot(latb, wuv_ref[...]).astype(BF16)

    gate = _dot(misc.astype(BF16), wg_ref[...]) + bg_ref[...]
    la_o[...] = _log_sigmoid(gate) * (1.0 / GLA_TAU)


def _in_proj(x2, lw, ropes, tm):
    m = x2.shape[0]
    row = lambda i: (i, 0)
    fix = lambda i: (0, 0)
    c, s1, s2 = ropes
    if c.shape[0] == m:
        rope_map = row
    else:
        nb = c.shape[0] // tm
        rope_map = lambda i: (i % nb, 0)
    weights = [lw["norm_mix"], lw["w_in"], lw["g_q_lat"], lw["w_uq"], lw["g_kv_lat"], lw["w_uk"],
               lw["w_uv"], lw["g_qk_q"], lw["g_qk_k"], lw["place"], lw["w_gate"], lw["b_gate"]]
    in_specs = [pl.BlockSpec((tm, D_MODEL), row)]
    in_specs += [pl.BlockSpec(w.shape, fix) for w in weights]
    in_specs += [pl.BlockSpec((tm, HEAD_PAD), rope_map)] * 3
    widths = [(KV_RANK, F32), (QK_ROPE, F32), (QK_W, BF16), (QK_W, BF16), (MLA_OUT, BF16),
              (GLA_QK, F32), (GLA_QK, F32), (GLA_OUT, F32), (GLA_QK, F32), (GLA_OUT, F32),
              (POOL_WIDTH, F32)]
    return pl.pallas_call(
        _inproj_kernel,
        grid=(m // tm,),
        in_specs=in_specs,
        out_specs=[pl.BlockSpec((tm, w), row) for w, _ in widths],
        out_shape=[jax.ShapeDtypeStruct((m, w), dt) for w, dt in widths],
        compiler_params=_cparams(("parallel",), 48),
        name="in_proj",
    )(x2, *weights, c, s1, s2)


def _flash_kernel(q_ref, k_ref, v_ref, o_ref, *, tq):
    qi = pl.program_id(2)
    q = q_ref[...]

    def block(j, carry, masked):
        m, l, acc = carry
        start = pl.multiple_of(j * tq, tq)
        k = k_ref[pl.ds(start, tq), :]
        v = v_ref[pl.ds(start, tq), :]
        s = _dot_nt(q, k)
        if masked:
            rows = lax.broadcasted_iota(jnp.int32, (tq, tq), 0)
            cols = lax.broadcasted_iota(jnp.int32, (tq, tq), 1)
            s = jnp.where(cols <= rows, s, -jnp.inf)
        m_new = jnp.maximum(m, jnp.max(s, axis=-1, keepdims=True))
        alpha = jnp.exp(m - m_new)
        p = jnp.exp(s - m_new)
        l = alpha * l + jnp.sum(p, axis=-1, keepdims=True)
        acc = alpha * acc + _dot(p.astype(BF16), v)
        return m_new, l, acc

    init = (jnp.full((tq, 1), -jnp.inf, F32), jnp.zeros((tq, 1), F32), jnp.zeros((tq, V_HEAD), F32))
    carry = lax.fori_loop(0, qi, lambda j, cr: block(j, cr, False), init)
    _, l, acc = block(qi, carry, True)
    o_ref[...] = (acc / l).astype(o_ref.dtype)


def _mla_prompt(q3, k3, v3, tq):
    b, t, _ = q3.shape
    qmap = lambda bi, h, qi: (bi, qi, h)
    kmap = lambda bi, h, qi: (bi, 0, h)
    return pl.pallas_call(
        functools.partial(_flash_kernel, tq=tq),
        grid=(b, MLA_HEADS, t // tq),
        in_specs=[pl.BlockSpec((None, tq, HEAD_PAD), qmap), pl.BlockSpec((None, t, HEAD_PAD), kmap),
                  pl.BlockSpec((None, t, V_HEAD), kmap)],
        out_specs=pl.BlockSpec((None, tq, V_HEAD), qmap),
        out_shape=jax.ShapeDtypeStruct((b, t, MLA_OUT), BF16),
        compiler_params=_cparams(("parallel", "parallel", "arbitrary"), 40),
        name="mla_prompt",
    )(q3, k3, v3)


def _gla_kernel(q_ref, k_ref, v_ref, la_ref, gr_ref, tri_ref, e64_ref, gout_ref, o_ref, st_ref, *, t):
    c = GLA_CHUNK
    tri = tri_ref[...]
    e64 = e64_ref[...]
    gout = gout_ref[...]
    lane_qk = lax.broadcasted_iota(jnp.int32, (c, GLA_QK), 1) // GLA_DK
    lane_v = lax.broadcasted_iota(jnp.int32, (c, GLA_OUT), 1) // GLA_DV
    causal = lax.broadcasted_iota(jnp.int32, (c, c), 1) <= lax.broadcasted_iota(jnp.int32, (c, c), 0)
    bd = (lax.broadcasted_iota(jnp.int32, (GLA_OUT, GLA_QK), 0) // GLA_DV
          == lax.broadcasted_iota(jnp.int32, (GLA_OUT, GLA_QK), 1) // GLA_DK)

    def chunk(ci, st):
        r0 = pl.multiple_of(ci * c, c)
        q = q_ref[pl.ds(r0, c), :] * (GLA_DK ** -0.5)
        k = k_ref[pl.ds(r0, c), :]
        v = v_ref[pl.ds(r0, c), :]
        a = la_ref[pl.ds(r0, c), :]
        a_hi, a_lo = _split_bf16(a)
        b = _dot(tri, a_hi) + _dot(tri, a_lo)
        b_last = b[c - 1:c, :]
        qd = q * jnp.exp(b)
        kd = (k * jnp.exp(-b)).astype(BF16)
        k2 = (k * jnp.exp(b_last - b)).astype(BF16)
        vb = v.astype(BF16)
        o = _dot_nt(qd.astype(BF16), st.astype(BF16))
        for h in range(GLA_HEADS):
            qh = jnp.where(lane_qk == h, qd, 0.0).astype(BF16)
            att = jnp.where(causal, _dot_nt(qh, kd), 0.0)
            o = o + jnp.where(lane_v == h, _dot(att.astype(BF16), vb), 0.0)
        st = st * jnp.exp(b_last) + jnp.where(bd, _dot(v.T.astype(BF16), k2), 0.0)
        ms = _dot_split(o * o, e64) * (1.0 / GLA_DV)
        on = o * lax.rsqrt(ms + EPS) * gout
        gr = gr_ref[pl.ds(r0, c), :]
        o_ref[pl.ds(r0, c), :] = (on * (gr * _sigmoid(gr))).astype(o_ref.dtype)
        return st

    st_ref[...] = lax.fori_loop(0, t // c, chunk, jnp.zeros((GLA_OUT, GLA_QK), F32))


def _gla_prompt(gq3, gk3, gv3, la3, gr3, lw):
    b, t, _ = gq3.shape
    bmap = lambda i: (i, 0, 0)
    fix = lambda i: (0, 0)
    tri = jnp.tril(jnp.ones((GLA_CHUNK, GLA_CHUNK), F32)).astype(BF16)
    return pl.pallas_call(
        functools.partial(_gla_kernel, t=t),
        grid=(b,),
        in_specs=[pl.BlockSpec((None, t, GLA_QK), bmap), pl.BlockSpec((None, t, GLA_QK), bmap),
                  pl.BlockSpec((None, t, GLA_OUT), bmap), pl.BlockSpec((None, t, GLA_QK), bmap),
                  pl.BlockSpec((None, t, GLA_OUT), bmap), pl.BlockSpec((GLA_CHUNK, GLA_CHUNK), fix),
                  pl.BlockSpec((GLA_OUT, GLA_OUT), fix), pl.BlockSpec((1, GLA_OUT), fix)],
        out_specs=[pl.BlockSpec((None, t, GLA_OUT), bmap), pl.BlockSpec((None, GLA_OUT, GLA_QK), bmap)],
        out_shape=[jax.ShapeDtypeStruct((b, t, GLA_OUT), BF16),
                   jax.ShapeDtypeStruct((b, GLA_OUT, GLA_QK), F32)],
        compiler_params=_cparams(("parallel",), 40),
        name="gla_prompt",
    )(gq3, gk3, gv3, la3, gr3, tri, lw["e64"], lw["g_gla_out"])


def _pool_select(parts):
    lane = lax.broadcasted_iota(jnp.int32, parts[0].shape, 1) // POOL_GROUP
    out = parts[-1]
    for g in range(len(parts) - 2, -1, -1):
        out = jnp.where(lane == g, parts[g], out)
    return out


def _pool_kernel(u_ref, halo_ref, w_ref, sc_ref, o_ref, *, tp):
    i = pl.program_id(1)
    u = u_ref[...]
    halo = jnp.where(i == 0, 0.0, halo_ref[...])
    ext = jnp.concatenate([halo, u], axis=0)
    hb = halo.shape[0]
    pos = i * tp + lax.broadcasted_iota(jnp.int32, (tp, 1), 0)
    sums = ext
    parts = []
    span = 1
    for w in POOL_WINDOWS:
        while span < w:
            sums = sums + pltpu.roll(sums, span, 0)
            span *= 2
        cnt = jnp.minimum(w, pos + 1).astype(F32)
        parts.append(sums[hb:, :] / cnt - u)
    pooled = _pool_select(parts)
    o_ref[...] = (_dot(pooled.astype(BF16), w_ref[...]) * sc_ref[...]).astype(o_ref.dtype)


def _pool_prompt(u3, lw, tp):
    b, t, _ = u3.shape
    hb = 16
    return pl.pallas_call(
        functools.partial(_pool_kernel, tp=tp),
        grid=(b, t // tp),
        in_specs=[pl.BlockSpec((None, tp, POOL_WIDTH), lambda bi, i: (bi, i, 0)),
                  pl.BlockSpec((None, hb, POOL_WIDTH),
                               lambda bi, i: (bi, jnp.maximum(i * (tp // hb) - 1, 0), 0)),
                  pl.BlockSpec((POOL_WIDTH, POOL_WIDTH), lambda bi, i: (0, 0)),
                  pl.BlockSpec((1, POOL_WIDTH), lambda bi, i: (0, 0))],
        out_specs=pl.BlockSpec((None, tp, POOL_WIDTH), lambda bi, i: (bi, i, 0)),
        out_shape=jax.ShapeDtypeStruct((b, t, POOL_WIDTH), BF16),
        compiler_params=_cparams(("parallel", "parallel"), 32),
        name="pool_prompt",
    )(u3, u3, lw["w_pool"], lw["pool_scale"])


def _mix_out(x, mla, gla, pool, wo_ref):
    return (x + _dot(mla.astype(BF16), wo_ref[0:MLA_OUT, :])
            + _dot(gla.astype(BF16), wo_ref[MLA_OUT:MLA_OUT + GLA_OUT, :])
            + _dot(pool.astype(BF16), wo_ref[MLA_OUT + GLA_OUT:, :]))


def _xa_query(x1, gxa, wxq, gxq):
    q = _dot(_rms(x1, gxa).astype(BF16), wxq)
    return _head_rms(q, gxq, XA_HEADS, XA_HEAD, XA_HEAD) * (XA_HEAD ** -0.5)


def _xa_attend(qb, kb, vb):
    outs = []
    for h in range(XA_HEADS):
        sl = slice(h * XA_HEAD, (h + 1) * XA_HEAD)
        s = _dot_nt(qb[:, sl], kb[:, sl])
        e = jnp.exp(s - jnp.max(s, axis=-1, keepdims=True))
        p = e / jnp.sum(e, axis=-1, keepdims=True)
        outs.append(_dot(p.astype(BF16), vb[:, sl]))
    return jnp.concatenate(outs, axis=1).astype(BF16)


def _post_kernel(x_ref, mla_ref, gla_ref, pool_ref, wo_ref, gxa_ref, wxq_ref, gxq_ref, mk_ref, mv_ref,
                 wxo_ref, o_ref):
    x1 = _mix_out(x_ref[...], mla_ref[...], gla_ref[...], pool_ref[...], wo_ref)
    qb = _xa_query(x1, gxa_ref[...], wxq_ref[...], gxq_ref[...]).astype(BF16)
    o = _xa_attend(qb, mk_ref[...], mv_ref[...])
    o_ref[...] = x1 + _dot(o, wxo_ref[...])


def _post_prompt(x2, mla, gla, pool, mk3, mv3, lw, t, tm):
    m = x2.shape[0]
    per_seq = t // tm
    row = lambda i: (i, 0)
    fix = lambda i: (0, 0)
    mem = lambda i: (i // per_seq, 0, 0)
    sq = (D_MODEL, D_MODEL)
    return pl.pallas_call(
        _post_kernel,
        grid=(m // tm,),
        in_specs=[pl.BlockSpec((tm, D_MODEL), row), pl.BlockSpec((tm, MLA_OUT), row),
                  pl.BlockSpec((tm, GLA_OUT), row), pl.BlockSpec((tm, POOL_WIDTH), row),
                  pl.BlockSpec(sq, fix), pl.BlockSpec((1, D_MODEL), fix), pl.BlockSpec(sq, fix),
                  pl.BlockSpec((1, D_MODEL), fix), pl.BlockSpec((None, N_MEM, D_MODEL), mem),
                  pl.BlockSpec((None, N_MEM, D_MODEL), mem), pl.BlockSpec(sq, fix)],
        out_specs=pl.BlockSpec((tm, D_MODEL), row),
        out_shape=jax.ShapeDtypeStruct((m, D_MODEL), F32),
        compiler_params=_cparams(("parallel",), 48),
        name="post_prompt",
    )(x2, mla, gla, pool, lw["w_out"], lw["norm_xa"], lw["w_xq"], lw["g_xq"], mk3, mv3, lw["w_xo"])


def _mlp_kernel(x_ref, g_ref, w1_ref, w2_ref, o_ref, h_sc, acc_sc):
    j = pl.program_id(1)

    @pl.when(j == 0)
    def _():
        h_sc[...] = _rms(x_ref[...], g_ref[...]).astype(BF16)
        acc_sc[...] = x_ref[...]

    a = jnp.maximum(_dot(h_sc[...], w1_ref[...]), 0.0)
    acc_sc[...] += _dot((a * a).astype(BF16), w2_ref[...])

    @pl.when(j == pl.num_programs(1) - 1)
    def _():
        o_ref[...] = acc_sc[...]


def _mlp(x2, lw, tm, tf=1024):
    m = x2.shape[0]
    return pl.pallas_call(
        _mlp_kernel,
        grid=(m // tm, D_FF // tf),
        in_specs=[pl.BlockSpec((tm, D_MODEL), lambda i, j: (i, 0)),
                  pl.BlockSpec((1, D_MODEL), lambda i, j: (0, 0)),
                  pl.BlockSpec((D_MODEL, tf), lambda i, j: (0, j)),
                  pl.BlockSpec((tf, D_MODEL), lambda i, j: (j, 0))],
        out_specs=pl.BlockSpec((tm, D_MODEL), lambda i, j: (i, 0)),
        out_shape=jax.ShapeDtypeStruct((m, D_MODEL), F32),
        scratch_shapes=[pltpu.VMEM((tm, D_MODEL), BF16), pltpu.VMEM((tm, D_MODEL), F32)],
        compiler_params=_cparams(("parallel", "arbitrary"), 48),
        name="mlp",
    )(x2, lw["norm_mlp"], lw["w_ff1"], lw["w_ff2"])


def _head_sum(x):
    row = lax.broadcasted_iota(jnp.int32, (8, x.shape[1]), 0)
    out = jnp.zeros((8, x.shape[1]), F32)
    for h in range(MLA_HEADS):
        s = jnp.sum(x[h * QK_NOPE:(h + 1) * QK_NOPE, :], axis=0, keepdims=True)
        out = jnp.where(row == h, s, out)
    return out


def _mla_decode_kernel(pt_ref, q_ref, k_ref, latn_ref, cos_ref, sin_ref, wukt_ref, gkn_ref, gcol_ref,
                       eye_ref, wuv_ref, *rest, pp):
    lat_refs = rest[:pp]
    kr_refs = rest[pp:2 * pp]
    o_ref, m_sc, l_sc, acc_sc, qbc_sc = rest[2 * pp:]
    j = pl.program_id(1)
    q4 = q_ref[...].astype(F32)

    @pl.when(j == 0)
    def _():
        m_sc[...] = jnp.full(m_sc.shape, -jnp.inf, F32)
        l_sc[...] = jnp.zeros(l_sc.shape, F32)
        acc_sc[...] = jnp.zeros(acc_sc.shape, F32)
        qn = q4 * gkn_ref[...]
        eye = lax.broadcasted_iota(jnp.int32, (128, 128), 0) == lax.broadcasted_iota(jnp.int32, (128, 128), 1)
        ones = jnp.ones((128, 128), BF16)
        for h in range(MLA_HEADS):
            diag = jnp.where(eye, qn[h:h + 1, :], 0.0).astype(BF16)
            qbc_sc[h * QK_NOPE:(h + 1) * QK_NOPE, :] = _dot(diag, ones)[0:QK_NOPE, :]

    qr = q4[:, QK_NOPE:QK_HEAD].astype(BF16)
    qb = qbc_sc[...]
    qb2 = jnp.concatenate([qb, qb], axis=1)
    gcol = gcol_ref[...]
    eye32 = eye_ref[...]
    half = QK_ROPE // 2
    scores, lats = [], []
    for pr in range(pp // 2):
        lb = jnp.concatenate([lat_refs[2 * pr][...].astype(BF16), lat_refs[2 * pr + 1][...].astype(BF16)],
                             axis=0)
        knt = _dot_nt(wukt_ref[...], lb)
        sn = _head_sum(knt * qb2)
        ssn = _head_sum(knt * knt)
        kr = jnp.concatenate([kr_refs[2 * pr][...], kr_refs[2 * pr + 1][...]], axis=0)
        hi, lo = _split_bf16(kr)
        krt = _dot_nt(eye32, hi) + _dot_nt(eye32, lo)
        ssr = jnp.sum(krt * krt, axis=0, keepdims=True)
        krg = krt * gcol
        cs = cos_ref[:, pr * 256:(pr + 1) * 256]
        sn_ = sin_ref[:, pr * 256:(pr + 1) * 256]
        x1, x2 = krg[0:half, :], krg[half:, :]
        roped = jnp.concatenate([x1 * cs - x2 * sn_, x2 * cs + x1 * sn_], axis=0)
        sr = _dot(qr, roped.astype(BF16))
        r = lax.rsqrt((ssn + ssr) * (1.0 / QK_HEAD) + EPS)
        scores.append((sn + sr) * r)
        lats.append(lb)

    smax = scores[0].max(axis=1, keepdims=True)
    for s in scores[1:]:
        smax = jnp.maximum(smax, s.max(axis=1, keepdims=True))
    m_old = m_sc[...]
    m_new = jnp.maximum(m_old, smax)
    alpha = jnp.exp(m_old - m_new)
    psum = jnp.zeros((8, 1), F32)
    pv = jnp.zeros((8, KV_RANK), F32)
    for s, lb in zip(scores, lats):
        p = jnp.exp(s - m_new[:, 0:1])
        psum = psum + jnp.sum(p, axis=1, keepdims=True)
        pv = pv + _dot(p.astype(BF16), lb)
    l_sc[...] = alpha * l_sc[...] + psum
    acc_sc[...] = alpha[:, 0:1] * acc_sc[...] + pv
    m_sc[...] = m_new

    @pl.when(j == pl.num_programs(1) - 1)
    def _():
        s_self = jnp.sum(q4 * k_ref[...].astype(F32), axis=1, keepdims=True)
        m_o = m_sc[...]
        m_n = jnp.maximum(m_o, s_self)
        al = jnp.exp(m_o - m_n)
        ps = jnp.exp(s_self - m_n)
        l = al * l_sc[...] + ps
        acc = al[:, 0:1] * acc_sc[...] + ps[:, 0:1] * latn_ref[...]
        o_lat = acc / l[:, 0:1]
        res = _dot(o_lat.astype(BF16), wuv_ref[...])
        row = lax.broadcasted_iota(jnp.int32, res.shape, 0)
        lane = lax.broadcasted_iota(jnp.int32, res.shape, 1) // V_HEAD
        o_ref[...] = jnp.sum(jnp.where(row == lane, res, 0.0), axis=0, keepdims=True).astype(o_ref.dtype)


def _mla_decode(page_table, q8, k8, lat_new3, cache_lat, cache_kr, layer, lw, tabs, pp=16):
    n = q8.shape[0]
    n_pages = page_table.shape[1]
    nj = n_pages // pp
    cos_t, sin_t = tabs
    fix2 = lambda b, j, pt: (0, 0)
    seq3 = lambda b, j, pt: (b, 0, 0)

    def page_map(p):
        return lambda b, j, pt: (pt[b, j * pp + p], layer, 0, 0)

    weights = [lw["w_uk_t"], lw["g_k_nope"], lw["g_k_rope_col"], lw["eye32"], lw["w_uv"]]
    in_specs = [pl.BlockSpec((None, 8, HEAD_PAD), seq3), pl.BlockSpec((None, 8, HEAD_PAD), seq3),
                pl.BlockSpec((None, 1, KV_RANK), seq3),
                pl.BlockSpec((QK_ROPE // 2, pp * PAGE_SIZE), lambda b, j, pt: (0, j)),
                pl.BlockSpec((QK_ROPE // 2, pp * PAGE_SIZE), lambda b, j, pt: (0, j))]
    in_specs += [pl.BlockSpec(w.shape, fix2) for w in weights]
    in_specs += [pl.BlockSpec((None, None, PAGE_SIZE, KV_RANK), page_map(p)) for p in range(pp)]
    in_specs += [pl.BlockSpec((None, None, PAGE_SIZE, QK_ROPE), page_map(p)) for p in range(pp)]
    grid_spec = pltpu.PrefetchScalarGridSpec(
        num_scalar_prefetch=1,
        grid=(n, nj),
        in_specs=in_specs,
        out_specs=pl.BlockSpec((None, 1, MLA_OUT), seq3),
        scratch_shapes=[pltpu.VMEM((8, 128), F32), pltpu.VMEM((8, 128), F32), pltpu.VMEM((8, KV_RANK), F32),
                        pltpu.VMEM((MLA_HEADS * QK_NOPE, 128), F32)])
    return pl.pallas_call(
        functools.partial(_mla_decode_kernel, pp=pp),
        grid_spec=grid_spec,
        out_shape=jax.ShapeDtypeStruct((n, 1, MLA_OUT), F32),
        compiler_params=_cparams(("parallel", "arbitrary"), 40),
        name="mla_decode",
    )(page_table, q8, k8, lat_new3, cos_t, sin_t, *weights, *([cache_lat] * pp), *([cache_kr] * pp))


def _dec_mix_kernel(q_ref, k_ref, la_ref, v_ref, gr_ref, u_ref, s_ref, pb_ref, gout_ref, wp_ref, sc_ref,
                    s_o, gla_o, pool_o, pb_o, *, past):
    a = jnp.exp(la_ref[...])
    v4 = v_ref[...]
    vexp = jnp.concatenate([jnp.broadcast_to(v4[h:h + 1, :], (GLA_DK, GLA_DV)) for h in range(GLA_HEADS)],
                           axis=0)
    s_new = a * s_ref[...] + k_ref[...] * vexp
    s_o[...] = s_new
    prod = (q_ref[...] * (GLA_DK ** -0.5)) * s_new
    gr = gr_ref[...]
    gout = gout_ref[...]
    for h in range(GLA_HEADS):
        o = jnp.sum(prod[h * GLA_DK:(h + 1) * GLA_DK, :], axis=0, keepdims=True)
        on = _rms(o, gout)
        g = gr[h:h + 1, :]
        gla_o[h:h + 1, :] = (on * (g * _sigmoid(g))).astype(gla_o.dtype)

    st = pb_ref[...]
    u = u_ref[...]
    ridx = lax.broadcasted_iota(jnp.int32, st.shape, 0)
    parts = []
    for w in POOL_WINDOWS:
        tot = u + jnp.sum(jnp.where(ridx >= POOL_BUF + 1 - w, st, 0.0), axis=0, keepdims=True)
        parts.append(tot / float(min(w, past + 1)) - u)
    pooled = jnp.broadcast_to(_pool_select(parts), (8, POOL_WIDTH)).astype(BF16)
    y = _dot(pooled, wp_ref[...])[0:1, :] * sc_ref[...]
    pool_o[...] = y.astype(pool_o.dtype)
    pb_o[0:POOL_BUF - 1, :] = st[1:POOL_BUF, :]
    pb_o[POOL_BUF - 1:POOL_BUF, :] = u


def _dec_mix(gq, gk, la, gv, gr, u, state_gla4, state_pool, layer, lw, past):
    n = gq.shape[0]
    col = lambda x: x.reshape(n, GLA_QK, 1)
    b3 = lambda b: (b, 0, 0)
    fix = lambda b: (0, 0)
    return pl.pallas_call(
        functools.partial(_dec_mix_kernel, past=past),
        grid=(n,),
        in_specs=[pl.BlockSpec((None, GLA_QK, 1), b3)] * 3
        + [pl.BlockSpec((None, GLA_HEADS, GLA_DV), b3)] * 2
        + [pl.BlockSpec((None, 1, POOL_WIDTH), b3),
           pl.BlockSpec((None, None, GLA_QK, GLA_DV), lambda b: (b, layer, 0, 0)),
           pl.BlockSpec((None, None, POOL_BUF, POOL_WIDTH), lambda b: (b, layer, 0, 0)),
           pl.BlockSpec((1, GLA_DV), fix), pl.BlockSpec((POOL_WIDTH, POOL_WIDTH), fix),
           pl.BlockSpec((1, POOL_WIDTH), fix)],
        out_specs=[pl.BlockSpec((None, GLA_QK, GLA_DV), b3), pl.BlockSpec((None, GLA_HEADS, GLA_DV), b3),
                   pl.BlockSpec((None, 1, POOL_WIDTH), b3), pl.BlockSpec((None, POOL_BUF, POOL_WIDTH), b3)],
        out_shape=[jax.ShapeDtypeStruct((n, GLA_QK, GLA_DV), F32),
                   jax.ShapeDtypeStruct((n, GLA_HEADS, GLA_DV), F32),
                   jax.ShapeDtypeStruct((n, 1, POOL_WIDTH), F32),
                   jax.ShapeDtypeStruct((n, POOL_BUF, POOL_WIDTH), F32)],
        compiler_params=_cparams(("parallel",), 32),
        name="dec_mix",
    )(col(gq), col(gk), col(la), gv.reshape(n, GLA_HEADS, GLA_DV), gr.reshape(n, GLA_HEADS, GLA_DV),
      u.reshape(n, 1, POOL_WIDTH), state_gla4, state_pool, lw["g_gla_out64"], lw["w_pool"], lw["pool_scale"])


def _post_a_kernel(x_ref, mla_ref, gla_ref, pool_ref, wo_ref, gxa_ref, wxq_ref, gxq_ref, x1_ref, q_ref):
    x1 = _mix_out(x_ref[...], mla_ref[...], gla_ref[...], pool_ref[...], wo_ref)
    x1_ref[...] = x1
    q_ref[...] = _xa_query(x1, gxa_ref[...], wxq_ref[...], gxq_ref[...])


def _post_a(x2, mla, gla, pool, lw):
    m = x2.shape[0]
    return pl.pallas_call(
        _post_a_kernel,
        out_shape=[jax.ShapeDtypeStruct((m, D_MODEL), F32), jax.ShapeDtypeStruct((m, D_MODEL), F32)],
        compiler_params=pltpu.CompilerParams(vmem_limit_bytes=32 * MIB),
        name="post_a",
    )(x2, mla, gla, pool, lw["w_out"], lw["norm_xa"], lw["w_xq"], lw["g_xq"])


def _xa_decode_kernel(q_ref, mk_ref, mv_ref, o_ref):
    qb = jnp.broadcast_to(q_ref[...], (16, D_MODEL)).astype(BF16)
    o = _xa_attend(qb, mk_ref[...].astype(BF16), mv_ref[...].astype(BF16))
    o_ref[...] = o[0:1, :].astype(o_ref.dtype)


def _xa_decode(q3, cache_k, cache_v, layer):
    n = q3.shape[0]
    b3 = lambda b: (b, 0, 0)
    mem = lambda b: (b, layer, 0, 0)
    return pl.pallas_call(
        _xa_decode_kernel,
        grid=(n,),
        in_specs=[pl.BlockSpec((None, 1, D_MODEL), b3), pl.BlockSpec((None, None, N_MEM, D_MODEL), mem),
                  pl.BlockSpec((None, None, N_MEM, D_MODEL), mem)],
        out_specs=pl.BlockSpec((None, 1, D_MODEL), b3),
        out_shape=jax.ShapeDtypeStruct((n, 1, D_MODEL), F32),
        compiler_params=_cparams(("parallel",), 32),
        name="xa_decode",
    )(q3, cache_k, cache_v)


def _post_b_kernel(x1_ref, o_ref, wxo_ref, x2_ref):
    x2_ref[...] = x1_ref[...] + _dot(o_ref[...].astype(BF16), wxo_ref[...])


def _post_b(x1, o, lw):
    return pl.pallas_call(
        _post_b_kernel,
        out_shape=jax.ShapeDtypeStruct(x1.shape, F32),
        compiler_params=pltpu.CompilerParams(vmem_limit_bytes=32 * MIB),
        name="post_b",
    )(x1, o, lw["w_xo"])


def _rope_tables_rows(pos):
    half = QK_ROPE // 2
    inv = ROPE_THETA ** (-jnp.arange(half, dtype=F32) / half)
    ang = pos.astype(F32)[:, None] * inv[None, :]
    cos, sin = jnp.cos(ang), jnp.sin(ang)
    n = pos.shape[0]
    one = jnp.ones((n, QK_NOPE), F32)
    zero = jnp.zeros((n, QK_NOPE), F32)
    pad1 = jnp.ones((n, HEAD_PAD - QK_HEAD), F32)
    pad0 = jnp.zeros((n, HEAD_PAD - QK_HEAD), F32)
    zh = jnp.zeros((n, half), F32)
    c = jnp.concatenate([one, cos, cos, pad1], axis=1)
    s1 = jnp.concatenate([zero, -sin, zh, pad0], axis=1)
    s2 = jnp.concatenate([zero, zh, sin, pad0], axis=1)
    return c, s1, s2


def _rope_tables_cols(pos):
    half = QK_ROPE // 2
    inv = ROPE_THETA ** (-jnp.arange(half, dtype=F32) / half)
    ang = pos.astype(F32)[:, None] * inv[None, :]
    return jnp.cos(ang).T, jnp.sin(ang).T


def _pad_heads(w, width):
    pad = [(0, 0)] * (w.ndim - 1) + [(0, HEAD_PAD - width)]
    w = jnp.pad(w, pad)
    return w.reshape(w.shape[:-2] + (w.shape[-2] * HEAD_PAD,))


def _layer_weights(l, p):
    w_in = p["w_in"][l]
    sizes = (Q_RANK, KV_RANK, QK_ROPE, GLA_QK, GLA_QK, GLA_OUT, GLA_GATE_RANK, GLA_OUT, POOL_WIDTH)
    offs = [0]
    for s in sizes:
        offs.append(offs[-1] + s)
    piece = lambda i: w_in[:, offs[i]:offs[i + 1]]
    w_in_p = jnp.concatenate([piece(0), piece(1), piece(3), piece(4), piece(5), piece(7), piece(8), piece(2),
                              piece(6), jnp.zeros((D_MODEL, IN_PAD - sum(sizes)), F32)], axis=1)
    g_q = _pad_heads(jnp.broadcast_to(p["g_qk_q"][l], (MLA_HEADS, QK_HEAD)), QK_HEAD)
    g_k = _pad_heads(jnp.broadcast_to(p["g_qk_k"][l], (MLA_HEADS, QK_HEAD)), QK_HEAD)
    place = jnp.zeros((128, QK_W), F32)
    j = jnp.arange(QK_ROPE)
    for h in range(MLA_HEADS):
        place = place.at[j, h * HEAD_PAD + QK_NOPE + j].set(1.0)
    w_gate = jnp.zeros((128, GLA_QK), F32).at[QK_ROPE:QK_ROPE + GLA_GATE_RANK].set(p["w_gate_up"][l])
    wp = p["w_pool"][l]
    w_pool = jnp.zeros((POOL_WIDTH, POOL_WIDTH), F32)
    for g in range(len(POOL_WINDOWS)):
        w_pool = w_pool.at[g * POOL_GROUP:(g + 1) * POOL_GROUP, g * POOL_GROUP:(g + 1) * POOL_GROUP].set(wp[g])
    blk = jnp.arange(GLA_OUT) // GLA_DV
    row = lambda v: v.reshape(1, -1).astype(F32)
    bf = lambda v: v.astype(BF16)
    w_uk2 = p["w_uk"][l].reshape(KV_RANK, MLA_HEADS * QK_NOPE)
    g_k_nope = jnp.zeros((1, HEAD_PAD), F32).at[0, :QK_NOPE].set(p["g_qk_k"][l][:QK_NOPE])
    return {
        "norm_mix": row(p["norm_mix"][l]), "w_in": bf(w_in_p), "g_q_lat": row(p["g_q_lat"][l]),
        "w_uq": bf(_pad_heads(p["w_uq"][l], QK_HEAD)), "g_kv_lat": row(p["g_kv_lat"][l]),
        "w_uk": bf(_pad_heads(p["w_uk"][l], QK_NOPE)), "w_uv": bf(p["w_uv"][l].reshape(KV_RANK, MLA_OUT)),
        "g_qk_q": row(g_q), "g_qk_k": row(g_k), "place": bf(place), "w_gate": bf(w_gate),
        "b_gate": row(p["b_gate"][l]),
        "e64": bf((blk[:, None] == blk[None, :]).astype(F32)),
        "g_gla_out": row(jnp.tile(p["g_gla_out"][l], GLA_HEADS)), "g_gla_out64": row(p["g_gla_out"][l]),
        "w_pool": bf(w_pool), "pool_scale": row(p["pool_scale"][l]),
        "w_out": bf(p["w_out"][l]), "norm_xa": row(p["norm_xa"][l]), "w_xq": bf(p["w_xq"][l]),
        "g_xq": row(jnp.tile(p["g_xq"][l], XA_HEADS)), "w_xo": bf(p["w_xo"][l]),
        "norm_mlp": row(p["norm_mlp"][l]), "w_ff1": bf(p["w_ff1"][l]), "w_ff2": bf(p["w_ff2"][l]),
        "norm_mem": row(p["norm_mem"][l]), "w_xk": bf(p["w_xk"][l]), "w_xv": bf(p["w_xv"][l]),
        "g_xk": row(jnp.tile(p["g_xk"][l], XA_HEADS)),
        "w_uk_t": bf(w_uk2.T), "g_k_nope": g_k_nope,
        "g_k_rope_col": jnp.broadcast_to(p["g_qk_k"][l][QK_NOPE:, None], (QK_ROPE, 256)).astype(F32),
        "eye32": bf(jnp.eye(QK_ROPE, dtype=F32)),
    }


def _prompt_layer(x2, mem2, lw, ropes, b, t):
    mk, mv, mkb, mvb = _memory_kv(mem2, lw["norm_mem"], lw["w_xk"], lw["w_xv"], lw["g_xk"])
    lat, kr, q, k, v, gq, gk, gv, la, gr, u = _in_proj(x2, lw, ropes, 512)
    r3 = lambda a: a.reshape(b, t, a.shape[-1])
    mla = _mla_prompt(r3(q), r3(k), r3(v), 512).reshape(b * t, MLA_OUT)
    gla, st = _gla_prompt(r3(gq), r3(gk), r3(gv), r3(la), r3(gr), lw)
    pool = _pool_prompt(r3(u), lw, 512).reshape(b * t, POOL_WIDTH)
    x2 = _post_prompt(x2, mla, gla.reshape(b * t, GLA_OUT), pool, mkb.reshape(b, N_MEM, D_MODEL),
                      mvb.reshape(b, N_MEM, D_MODEL), lw, t, 512)
    x2 = _mlp(x2, lw, 1024)
    gla_state = jnp.stack([st[:, h * GLA_DV:(h + 1) * GLA_DV, h * GLA_DK:(h + 1) * GLA_DK]
                           for h in range(GLA_HEADS)], axis=1).transpose(0, 1, 3, 2)
    outs = (lat.reshape(b, t, KV_RANK), kr.reshape(b, t, QK_ROPE), gla_state,
            r3(u)[:, t - POOL_BUF:], mk.reshape(b, N_MEM, XA_HEADS, XA_HEAD),
            mv.reshape(b, N_MEM, XA_HEADS, XA_HEAD))
    return x2, outs


def _sample_layer(x2, layer, lw, ropes, tabs, page_table, cache_lat, cache_kr, state_gla4, state_pool,
                  cache_mk, cache_mv, past):
    n = x2.shape[0]
    lat, kr, q, k, v, gq, gk, gv, la, gr, u = _in_proj(x2, lw, ropes, n)
    pad8 = lambda a: jnp.pad(a.reshape(n, MLA_HEADS, HEAD_PAD), ((0, 0), (0, 8 - MLA_HEADS), (0, 0)))
    mla = _mla_decode(page_table, pad8(q), pad8(k), lat.reshape(n, 1, KV_RANK), cache_lat, cache_kr, layer,
                      lw, tabs).reshape(n, MLA_OUT)
    s_new, gla, pool, pb = _dec_mix(gq, gk, la, gv, gr, u, state_gla4, state_pool, layer, lw, past)
    x1, qx = _post_a(x2, mla, gla.reshape(n, GLA_OUT), pool.reshape(n, POOL_WIDTH), lw)
    ox = _xa_decode(qx.reshape(n, 1, D_MODEL), cache_mk, cache_mv, layer).reshape(n, D_MODEL)
    x2 = _mlp(_post_b(x1, ox, lw), lw, n)
    outs = (lat.reshape(n, 1, KV_RANK), kr.reshape(n, 1, QK_ROPE),
            s_new.reshape(n, GLA_HEADS, GLA_DK, GLA_DV), pb)
    return x2, outs


def kernel(x_prompt, x_sample, mem_prompt, cache_mla_latent, cache_mla_krope, state_gla, state_pool,
           cache_mem_k, cache_mem_v, page_table, norm_mix, w_in, g_q_lat, w_uq, g_kv_lat, w_uk, w_uv,
           g_qk_q, g_qk_k, w_gate_up, b_gate, g_gla_out, w_pool, pool_scale, w_out, norm_xa, norm_mem,
           w_xq, w_xk, w_xv, g_xq, g_xk, w_xo, norm_mlp, w_ff1, w_ff2):
    params = dict(norm_mix=norm_mix, w_in=w_in, g_q_lat=g_q_lat, w_uq=w_uq, g_kv_lat=g_kv_lat, w_uk=w_uk,
                  w_uv=w_uv, g_qk_q=g_qk_q, g_qk_k=g_qk_k, w_gate_up=w_gate_up, b_gate=b_gate,
                  g_gla_out=g_gla_out, w_pool=w_pool, pool_scale=pool_scale, w_out=w_out, norm_xa=norm_xa,
                  norm_mem=norm_mem, w_xq=w_xq, w_xk=w_xk, w_xv=w_xv, g_xq=g_xq, g_xk=g_xk, w_xo=w_xo,
                  norm_mlp=norm_mlp, w_ff1=w_ff1, w_ff2=w_ff2)
    b, t, _ = x_prompt.shape
    n, dec_seq, _ = x_sample.shape
    depth = norm_mix.shape[0]
    n_pages = page_table.shape[1]
    past = n_pages * PAGE_SIZE

    ropes_p = _rope_tables_rows(jnp.arange(t, dtype=jnp.int32))
    ropes_s = _rope_tables_rows(jnp.full((n,), past, jnp.int32))
    tabs = _rope_tables_cols(jnp.arange(past, dtype=jnp.int32))

    yp = x_prompt.reshape(b * t, D_MODEL)
    ys = x_sample.reshape(n * dec_seq, D_MODEL)
    mem2 = mem_prompt.reshape(b * N_MEM, D_MODEL)
    state_gla4 = state_gla.reshape(n, depth, GLA_QK, GLA_DV)
    cache_mk = cache_mem_k.reshape(n, depth, N_MEM, D_MODEL)
    cache_mv = cache_mem_v.reshape(n, depth, N_MEM, D_MODEL)

    p_outs, s_outs = [], []
    for l in range(depth):
        lw = _layer_weights(l, params)
        yp, po = _prompt_layer(yp, mem2, lw, ropes_p, b, t)
        ys, so = _sample_layer(ys, l, lw, ropes_s, tabs, page_table, cache_mla_latent, cache_mla_krope,
                               state_gla4, state_pool, cache_mk, cache_mv, past)
        p_outs.append(po)
        s_outs.append(so)

    stack = lambda outs, i: jnp.stack([o[i] for o in outs], axis=1)
    return (yp.reshape(b, t, D_MODEL), ys.reshape(n, dec_seq, D_MODEL),
            stack(p_outs, 0), stack(p_outs, 1), stack(p_outs, 2), stack(p_outs, 3), stack(p_outs, 4),
            stack(p_outs, 5), stack(s_outs, 0), stack(s_outs, 1), stack(s_outs, 2), stack(s_outs, 3))
```

```python
import functools

import jax
import jax.numpy as jnp
from jax import lax
from jax.experimental import pallas as pl
from jax.experimental.pallas import tpu as pltpu

F32 = jnp.float32
BF16 = jnp.bfloat16

D_MODEL = 1024
MLA_HEADS = 4
QK_NOPE = 64
QK_ROPE = 32
QK_HEAD = QK_NOPE + QK_ROPE
V_HEAD = 128
Q_RANK = 384
KV_RANK = 256
ROPE_THETA = 10000.0
GLA_HEADS = 4
GLA_DK = 32
GLA_DV = 64
GLA_GATE_RANK = 16
GLA_TAU = 16.0
GLA_CHUNK = 64
POOL_WINDOWS = (2, 4, 8, 16)
POOL_GROUP = 64
POOL_WIDTH = POOL_GROUP * len(POOL_WINDOWS)
POOL_BUF = max(POOL_WINDOWS) - 1
MLA_OUT = MLA_HEADS * V_HEAD
GLA_OUT = GLA_HEADS * GLA_DV
N_MEM = 256
XA_HEADS = 4
XA_HEAD = D_MODEL // XA_HEADS
D_FF = 4 * D_MODEL
PAGE_SIZE = 128
EPS = 1e-6

HEAD_PAD = 128
QK_W = MLA_HEADS * HEAD_PAD
GLA_QK = GLA_HEADS * GLA_DK
IN_PAD = 1792
MISC0 = 1664

MIB = 1024 * 1024


def _cparams(sem, vmem_mib):
    return pltpu.CompilerParams(dimension_semantics=sem, vmem_limit_bytes=vmem_mib * MIB)


def _rms(x, g):
    y = x * lax.rsqrt(jnp.mean(x * x, axis=-1, keepdims=True) + EPS)
    return y * g


def _head_rms(x, g, nh, width, count):
    outs = []
    for h in range(nh):
        p = x[:, h * width:(h + 1) * width]
        ms = jnp.sum(p * p, axis=-1, keepdims=True) * (1.0 / count)
        outs.append(p * lax.rsqrt(ms + EPS) * g[:, h * width:(h + 1) * width])
    return jnp.concatenate(outs, axis=1)


def _dot(a, b):
    return jnp.dot(a, b, preferred_element_type=F32)


def _dot_nt(a, b):
    return lax.dot_general(a, b, (((1,), (1,)), ((), ())), preferred_element_type=F32)


def _split_bf16(x):
    hi = x.astype(BF16)
    lo = (x - hi.astype(F32)).astype(BF16)
    return hi, lo


def _dot_split(x, w):
    hi, lo = _split_bf16(x)
    return _dot(hi, w) + _dot(lo, w)


def _rope_rows(x, c, s1, s2):
    n = x.shape[1]
    c4 = jnp.concatenate([c] * MLA_HEADS, axis=1)
    s14 = jnp.concatenate([s1] * MLA_HEADS, axis=1)
    s24 = jnp.concatenate([s2] * MLA_HEADS, axis=1)
    half = QK_ROPE // 2
    return x * c4 + pltpu.roll(x, n - half, 1) * s14 + pltpu.roll(x, half, 1) * s24


def _log_sigmoid(x):
    return -(jnp.maximum(-x, 0.0) + jnp.log1p(jnp.exp(-jnp.abs(x))))


def _sigmoid(x):
    return 1.0 / (1.0 + jnp.exp(-x))


def _memkv_kernel(mem_ref, g_ref, wk_ref, wv_ref, gk_ref, k_ref, v_ref, kb_ref, vb_ref):
    m = _rms(mem_ref[...], g_ref[...]).astype(BF16)
    k = _head_rms(_dot(m, wk_ref[...]), gk_ref[...], XA_HEADS, XA_HEAD, XA_HEAD)
    v = _dot(m, wv_ref[...])
    k_ref[...] = k
    v_ref[...] = v
    kb_ref[...] = k.astype(BF16)
    vb_ref[...] = v.astype(BF16)


def _memory_kv(mem2, g, wk, wv, gk):
    m = mem2.shape[0]
    tm = 512
    row = lambda i: (i, 0)
    fix = lambda i: (0, 0)
    return pl.pallas_call(
        _memkv_kernel,
        grid=(m // tm,),
        in_specs=[pl.BlockSpec((tm, D_MODEL), row), pl.BlockSpec((1, D_MODEL), fix),
                  pl.BlockSpec((D_MODEL, D_MODEL), fix), pl.BlockSpec((D_MODEL, D_MODEL), fix),
                  pl.BlockSpec((1, D_MODEL), fix)],
        out_specs=[pl.BlockSpec((tm, D_MODEL), row)] * 4,
        out_shape=[jax.ShapeDtypeStruct((m, D_MODEL), F32)] * 2
        + [jax.ShapeDtypeStruct((m, D_MODEL), BF16)] * 2,
        compiler_params=_cparams(("parallel",), 40),
        name="memory_kv",
    )(mem2, g, wk, wv, gk)


def _inproj_kernel(x_ref, gmix_ref, win_ref, gql_ref, wuq_ref, gkv_ref, wuk_ref, wuv_ref,
                   gq_ref, gk_ref, place_ref, wg_ref, bg_ref, c_ref, s1_ref, s2_ref,
                   lat_ref, kr_ref, q_ref, k_ref, v_ref, gq_o, gk_o, gv_o, la_o, gr_o, u_o):
    h = _rms(x_ref[...], gmix_ref[...]).astype(BF16)
    z = _dot(h, win_ref[...])
    c_q = z[:, 0:384]
    c_kv = z[:, 384:640]
    misc = z[:, MISC0:MISC0 + 128]
    gq_o[...] = z[:, 640:768]
    gk_o[...] = z[:, 768:896]
    gv_o[...] = z[:, 896:1152]
    gr_o[...] = z[:, 1152:1408]
    u_o[...] = z[:, 1408:1664]
    kr_ref[...] = misc[:, 0:QK_ROPE]

    lat = _rms(c_kv, gkv_ref[...])
    lat_ref[...] = lat
    latb = lat.astype(BF16)
    c, s1, s2 = c_ref[...], s1_ref[...], s2_ref[...]

    q = _dot(_rms(c_q, gql_ref[...]).astype(BF16), wuq_ref[...])
    q = _head_rms(q, gq_ref[...], MLA_HEADS, HEAD_PAD, QK_HEAD)
    q_ref[...] = (_rope_rows(q, c, s1, s2) * (QK_HEAD ** -0.5)).astype(BF16)

    k = _dot(latb, wuk_ref[...]) + _dot_split(misc, place_ref[...])
    k = _head_rms(k, gk_ref[...], MLA_HEADS, HEAD_PAD, QK_HEAD)
    k_ref[...] = _rope_rows(k, c, s1, s2).astype(BF16)
    v_ref[...] = _dot(latb, wuv_ref[...]).astype(BF16)

    gate = _dot(misc.astype(BF16), wg_ref[...]) + bg_ref[...]
    la_o[...] = _log_sigmoid(gate) * (1.0 / GLA_TAU)


def _in_proj(x2, lw, ropes, tm):
    m = x2.shape[0]
    row = lambda i: (i, 0)
    fix = lambda i: (0, 0)
    c, s1, s2 = ropes
    if c.shape[0] == m:
        rope_map = row
    else:
        nb = c.shape[0] // tm
        rope_map = lambda i: (i % nb, 0)
    weights = [lw["norm_mix"], lw["w_in"], lw["g_q_lat"], lw["w_uq"], lw["g_kv_lat"], lw["w_uk"],
               lw["w_uv"], lw["g_qk_q"], lw["g_qk_k"], lw["place"], lw["w_gate"], lw["b_gate"]]
    in_specs = [pl.BlockSpec((tm, D_MODEL), row)]
    in_specs += [pl.BlockSpec(w.shape, fix) for w in weights]
    in_specs += [pl.BlockSpec((tm, HEAD_PAD), rope_map)] * 3
    widths = [(KV_RANK, F32), (QK_ROPE, F32), (QK_W, BF16), (QK_W, BF16), (MLA_OUT, BF16),
              (GLA_QK, F32), (GLA_QK, F32), (GLA_OUT, F32), (GLA_QK, F32), (GLA_OUT, F32),
              (POOL_WIDTH, F32)]
    return pl.pallas_call(
        _inproj_kernel,
        grid=(m // tm,),
        in_specs=in_specs,
        out_specs=[pl.BlockSpec((tm, w), row) for w, _ in widths],
        out_shape=[jax.ShapeDtypeStruct((m, w), dt) for w, dt in widths],
        compiler_params=_cparams(("parallel",), 48),
        name="in_proj",
    )(x2, *weights, c, s1, s2)


def _flash_kernel(q_ref, k_ref, v_ref, o_ref, *, tq):
    qi = pl.program_id(2)
    q = q_ref[...]

    def block(j, carry, masked):
        m, l, acc = carry
        start = pl.multiple_of(j * tq, tq)
        k = k_ref[pl.ds(start, tq), :]
        v = v_ref[pl.ds(start, tq), :]
        s = _dot_nt(q, k)
        if masked:
            rows = lax.broadcasted_iota(jnp.int32, (tq, tq), 0)
            cols = lax.broadcasted_iota(jnp.int32, (tq, tq), 1)
            s = jnp.where(cols <= rows, s, -jnp.inf)
        m_new = jnp.maximum(m, jnp.max(s, axis=-1, keepdims=True))
        alpha = jnp.exp(m - m_new)
        p = jnp.exp(s - m_new)
        l = alpha * l + jnp.sum(p, axis=-1, keepdims=True)
        acc = alpha * acc + _dot(p.astype(BF16), v)
        return m_new, l, acc

    init = (jnp.full((tq, 1), -jnp.inf, F32), jnp.zeros((tq, 1), F32), jnp.zeros((tq, V_HEAD), F32))
    carry = lax.fori_loop(0, qi, lambda j, cr: block(j, cr, False), init)
    _, l, acc = block(qi, carry, True)
    o_ref[...] = (acc / l).astype(o_ref.dtype)


def _mla_prompt(q3, k3, v3, tq):
    b, t, _ = q3.shape
    qmap = lambda bi, h, qi: (bi, qi, h)
    kmap = lambda bi, h, qi: (bi, 0, h)
    return pl.pallas_call(
        functools.partial(_flash_kernel, tq=tq),
        grid=(b, MLA_HEADS, t // tq),
        in_specs=[pl.BlockSpec((None, tq, HEAD_PAD), qmap), pl.BlockSpec((None, t, HEAD_PAD), kmap),
                  pl.BlockSpec((None, t, V_HEAD), kmap)],
        out_specs=pl.BlockSpec((None, tq, V_HEAD), qmap),
        out_shape=jax.ShapeDtypeStruct((b, t, MLA_OUT), BF16),
        compiler_params=_cparams(("parallel", "parallel", "arbitrary"), 40),
        name="mla_prompt",
    )(q3, k3, v3)


def _gla_kernel(q_ref, k_ref, v_ref, la_ref, gr_ref, tri_ref, e64_ref, gout_ref, o_ref, st_ref, *, t):
    c = GLA_CHUNK
    tri = tri_ref[...]
    e64 = e64_ref[...]
    gout = gout_ref[...]
    lane_qk = lax.broadcasted_iota(jnp.int32, (c, GLA_QK), 1) // GLA_DK
    lane_v = lax.broadcasted_iota(jnp.int32, (c, GLA_OUT), 1) // GLA_DV
    causal = lax.broadcasted_iota(jnp.int32, (c, c), 1) <= lax.broadcasted_iota(jnp.int32, (c, c), 0)
    bd = (lax.broadcasted_iota(jnp.int32, (GLA_OUT, GLA_QK), 0) // GLA_DV
          == lax.broadcasted_iota(jnp.int32, (GLA_OUT, GLA_QK), 1) // GLA_DK)

    def chunk(ci, st):
        r0 = pl.multiple_of(ci * c, c)
        q = q_ref[pl.ds(r0, c), :] * (GLA_DK ** -0.5)
        k = k_ref[pl.ds(r0, c), :]
        v = v_ref[pl.ds(r0, c), :]
        a = la_ref[pl.ds(r0, c), :]
        a_hi, a_lo = _split_bf16(a)
        b = _dot(tri, a_hi) + _dot(tri, a_lo)
        b_last = b[c - 1:c, :]
        qd = q * jnp.exp(b)
        kd = (k * jnp.exp(-b)).astype(BF16)
        k2 = (k * jnp.exp(b_last - b)).astype(BF16)
        vb = v.astype(BF16)
        o = _dot_nt(qd.astype(BF16), st.astype(BF16))
        for h in range(GLA_HEADS):
            qh = jnp.where(lane_qk == h, qd, 0.0).astype(BF16)
            att = jnp.where(causal, _dot_nt(qh, kd), 0.0)
            o = o + jnp.where(lane_v == h, _dot(att.astype(BF16), vb), 0.0)
        st = st * jnp.exp(b_last) + jnp.where(bd, _dot(v.T.astype(BF16), k2), 0.0)
        ms = _dot_split(o * o, e64) * (1.0 / GLA_DV)
        on = o * lax.rsqrt(ms + EPS) * gout
        gr = gr_ref[pl.ds(r0, c), :]
        o_ref[pl.ds(r0, c), :] = (on * (gr * _sigmoid(gr))).astype(o_ref.dtype)
        return st

    st_ref[...] = lax.fori_loop(0, t // c, chunk, jnp.zeros((GLA_OUT, GLA_QK), F32), unroll=GLA_UNROLL)


def _gla_prompt(gq3, gk3, gv3, la3, gr3, lw):
    b, t, _ = gq3.shape
    bmap = lambda i: (i, 0, 0)
    fix = lambda i: (0, 0)
    tri = jnp.tril(jnp.ones((GLA_CHUNK, GLA_CHUNK), F32)).astype(BF16)
    return pl.pallas_call(
        functools.partial(_gla_kernel, t=t),
        grid=(b,),
        in_specs=[pl.BlockSpec((None, t, GLA_QK), bmap), pl.BlockSpec((None, t, GLA_QK), bmap),
                  pl.BlockSpec((None, t, GLA_OUT), bmap), pl.BlockSpec((None, t, GLA_QK), bmap),
                  pl.BlockSpec((None, t, GLA_OUT), bmap), pl.BlockSpec((GLA_CHUNK, GLA_CHUNK), fix),
                  pl.BlockSpec((GLA_OUT, GLA_OUT), fix), pl.BlockSpec((1, GLA_OUT), fix)],
        out_specs=[pl.BlockSpec((None, t, GLA_OUT), bmap), pl.BlockSpec((None, GLA_OUT, GLA_QK), bmap)],
        out_shape=[jax.ShapeDtypeStruct((b, t, GLA_OUT), BF16),
                   jax.ShapeDtypeStruct((b, GLA_OUT, GLA_QK), F32)],
        compiler_params=_cparams(("parallel",), 40),
        name="gla_prompt",
    )(gq3, gk3, gv3, la3, gr3, tri, lw["e64"], lw["g_gla_out"])


def _pool_select(parts):
    lane = lax.broadcasted_iota(jnp.int32, parts[0].shape, 1) // POOL_GROUP
    out = parts[-1]
    for g in range(len(parts) - 2, -1, -1):
        out = jnp.where(lane == g, parts[g], out)
    return out


def _pool_kernel(u_ref, halo_ref, w_ref, sc_ref, o_ref, *, tp):
    i = pl.program_id(1)
    u = u_ref[...]
    halo = jnp.where(i == 0, 0.0, halo_ref[...])
    ext = jnp.concatenate([halo, u], axis=0)
    hb = halo.shape[0]
    pos = i * tp + lax.broadcasted_iota(jnp.int32, (tp, 1), 0)
    sums = ext
    parts = []
    span = 1
    for w in POOL_WINDOWS:
        while span < w:
            sums = sums + pltpu.roll(sums, span, 0)
            span *= 2
        cnt = jnp.minimum(w, pos + 1).astype(F32)
        parts.append(sums[hb:, :] / cnt - u)
    pooled = _pool_select(parts)
    o_ref[...] = (_dot(pooled.astype(BF16), w_ref[...]) * sc_ref[...]).astype(o_ref.dtype)


def _pool_prompt(u3, lw, tp):
    b, t, _ = u3.shape
    hb = 16
    return pl.pallas_call(
        functools.partial(_pool_kernel, tp=tp),
        grid=(b, t // tp),
        in_specs=[pl.BlockSpec((None, tp, POOL_WIDTH), lambda bi, i: (bi, i, 0)),
                  pl.BlockSpec((None, hb, POOL_WIDTH),
                               lambda bi, i: (bi, jnp.maximum(i * (tp // hb) - 1, 0), 0)),
                  pl.BlockSpec((POOL_WIDTH, POOL_WIDTH), lambda bi, i: (0, 0)),
                  pl.BlockSpec((1, POOL_WIDTH), lambda bi, i: (0, 0))],
        out_specs=pl.BlockSpec((None, tp, POOL_WIDTH), lambda bi, i: (bi, i, 0)),
        out_shape=jax.ShapeDtypeStruct((b, t, POOL_WIDTH), BF16),
        compiler_params=_cparams(("parallel", "parallel"), 32),
        name="pool_prompt",
    )(u3, u3, lw["w_pool"], lw["pool_scale"])


def _mix_out(x, mla, gla, pool, wo_ref):
    return (x + _dot(mla.astype(BF16), wo_ref[0:MLA_OUT, :])
            + _dot(gla.astype(BF16), wo_ref[MLA_OUT:MLA_OUT + GLA_OUT, :])
            + _dot(pool.astype(BF16), wo_ref[MLA_OUT + GLA_OUT:, :]))


def _xa_query(x1, gxa, wxq, gxq):
    q = _dot(_rms(x1, gxa).astype(BF16), wxq)
    return _head_rms(q, gxq, XA_HEADS, XA_HEAD, XA_HEAD) * (XA_HEAD ** -0.5)


def _xa_attend(qb, kb, vb):
    outs = []
    for h in range(XA_HEADS):
        sl = slice(h * XA_HEAD, (h + 1) * XA_HEAD)
        s = _dot_nt(qb[:, sl], kb[:, sl])
        e = jnp.exp(s - jnp.max(s, axis=-1, keepdims=True))
        p = e / jnp.sum(e, axis=-1, keepdims=True)
        outs.append(_dot(p.astype(BF16), vb[:, sl]))
    return jnp.concatenate(outs, axis=1).astype(BF16)


def _post_kernel(x_ref, mla_ref, gla_ref, pool_ref, wo_ref, gxa_ref, wxq_ref, gxq_ref, mk_ref, mv_ref,
                 wxo_ref, o_ref):
    x1 = _mix_out(x_ref[...], mla_ref[...], gla_ref[...], pool_ref[...], wo_ref)
    qb = _xa_query(x1, gxa_ref[...], wxq_ref[...], gxq_ref[...]).astype(BF16)
    o = _xa_attend(qb, mk_ref[...], mv_ref[...])
    o_ref[...] = x1 + _dot(o, wxo_ref[...])


def _post_prompt(x2, mla, gla, pool, mk3, mv3, lw, t, tm):
    m = x2.shape[0]
    per_seq = t // tm
    row = lambda i: (i, 0)
    fix = lambda i: (0, 0)
    mem = lambda i: (i // per_seq, 0, 0)
    sq = (D_MODEL, D_MODEL)
    return pl.pallas_call(
        _post_kernel,
        grid=(m // tm,),
        in_specs=[pl.BlockSpec((tm, D_MODEL), row), pl.BlockSpec((tm, MLA_OUT), row),
                  pl.BlockSpec((tm, GLA_OUT), row), pl.BlockSpec((tm, POOL_WIDTH), row),
                  pl.BlockSpec(sq, fix), pl.BlockSpec((1, D_MODEL), fix), pl.BlockSpec(sq, fix),
                  pl.BlockSpec((1, D_MODEL), fix), pl.BlockSpec((None, N_MEM, D_MODEL), mem),
                  pl.BlockSpec((None, N_MEM, D_MODEL), mem), pl.BlockSpec(sq, fix)],
        out_specs=pl.BlockSpec((tm, D_MODEL), row),
        out_shape=jax.ShapeDtypeStruct((m, D_MODEL), F32),
        compiler_params=_cparams(("parallel",), 48),
        name="post_prompt",
    )(x2, mla, gla, pool, lw["w_out"], lw["norm_xa"], lw["w_xq"], lw["g_xq"], mk3, mv3, lw["w_xo"])


def _mlp_kernel(x_ref, g_ref, w1_ref, w2_ref, o_ref, h_sc, acc_sc):
    j = pl.program_id(1)

    @pl.when(j == 0)
    def _():
        h_sc[...] = _rms(x_ref[...], g_ref[...]).astype(BF16)
        acc_sc[...] = x_ref[...]

    a = jnp.maximum(_dot(h_sc[...], w1_ref[...]), 0.0)
    acc_sc[...] += _dot((a * a).astype(BF16), w2_ref[...])

    @pl.when(j == pl.num_programs(1) - 1)
    def _():
        o_ref[...] = acc_sc[...]


def _mlp(x2, lw, tm, tf=1024):
    m = x2.shape[0]
    return pl.pallas_call(
        _mlp_kernel,
        grid=(m // tm, D_FF // tf),
        in_specs=[pl.BlockSpec((tm, D_MODEL), lambda i, j: (i, 0)),
                  pl.BlockSpec((1, D_MODEL), lambda i, j: (0, 0)),
                  pl.BlockSpec((D_MODEL, tf), lambda i, j: (0, j)),
                  pl.BlockSpec((tf, D_MODEL), lambda i, j: (j, 0))],
        out_specs=pl.BlockSpec((tm, D_MODEL), lambda i, j: (i, 0)),
        out_shape=jax.ShapeDtypeStruct((m, D_MODEL), F32),
        scratch_shapes=[pltpu.VMEM((tm, D_MODEL), BF16), pltpu.VMEM((tm, D_MODEL), F32)],
        compiler_params=_cparams(("parallel", "arbitrary"), 48),
        name="mlp",
    )(x2, lw["norm_mlp"], lw["w_ff1"], lw["w_ff2"])


GLA_UNROLL = 4
DEC_PAGES = 32
DEC_CHUNK = 4096
DEC_LHS_ROWS = MLA_HEADS * QK_NOPE + 16
DEC_SS_ROWS = 48


def _mla_decode_kernel(pt_ref, q_ref, k_ref, latn_ref, cos_ref, sin_ref, wukt_ref, gkn_ref, gcol_ref,
                       ssel_ref, wuv_ref, lat_hbm, krt_hbm, o_ref,
                       lat_buf, kr_buf, sem, m_sc, l_sc, acc_sc, lhs_sc, lb_sc, *, pp, nj, layer):
    t = pl.program_id(0)
    n_steps = pl.num_programs(0)
    j = t % nj
    slot = t % 2

    def page_copies(step, buf_slot):
        b = step // nj
        base = (step % nj) * pp
        copies = []
        for p in range(pp):
            page = pt_ref[b, base + p]
            copies.append(pltpu.make_async_copy(lat_hbm.at[page, layer], lat_buf.at[buf_slot, p],
                                                sem.at[buf_slot, 0]))
            copies.append(pltpu.make_async_copy(krt_hbm.at[page, layer],
                                                kr_buf.at[buf_slot, :, pl.ds(p * PAGE_SIZE, PAGE_SIZE)],
                                                sem.at[buf_slot, 1]))
        return copies

    for cp in page_copies(jnp.minimum(t + 1, n_steps - 1), 1 - slot):
        cp.start()

    @pl.when(t == 0)
    def _():
        for cp in page_copies(t, slot):
            cp.start()
        lhs_sc[0:MLA_HEADS * QK_NOPE, :] = wukt_ref[...]

    q16 = q_ref[...].astype(F32)

    @pl.when(j == 0)
    def _():
        m_sc[...] = jnp.full(m_sc.shape, -jnp.inf, F32)
        l_sc[...] = jnp.zeros(l_sc.shape, F32)
        acc_sc[...] = jnp.zeros(acc_sc.shape, F32)
        qn = (q16 * gkn_ref[...])[:, 0:QK_NOPE]
        tiled = jnp.concatenate([qn] * MLA_HEADS, axis=1)
        row = lax.broadcasted_iota(jnp.int32, tiled.shape, 0)
        lane = lax.broadcasted_iota(jnp.int32, tiled.shape, 1) // QK_NOPE
        qbd = jnp.where(row == lane, tiled, 0.0).astype(BF16)
        lhs_sc[MLA_HEADS * QK_NOPE:, :] = _dot(qbd, wukt_ref[...]).astype(BF16)

    for cp in page_copies(t, slot):
        cp.wait()

    qr = q16[:, QK_NOPE:QK_HEAD].astype(BF16)
    lhs = lhs_sc[...]
    gcol = gcol_ref[...]
    ssel = ssel_ref[...]
    half = QK_ROPE // 2
    nf = MLA_HEADS * QK_NOPE
    for p in range(pp):
        lb_sc[p * PAGE_SIZE:(p + 1) * PAGE_SIZE, :] = lat_buf[slot, p].astype(BF16)

    ck = DEC_CHUNK
    scores = []
    for c0 in range(0, pp * PAGE_SIZE, ck):
        res = _dot_nt(lhs, lb_sc[c0:c0 + ck, :])
        sq = res[0:nf, :] * res[0:nf, :]
        sn = res[nf:nf + 8, :]
        krt = kr_buf[slot, :, c0:c0 + ck]
        parts = [sq[h * QK_NOPE:(h + 1) * QK_NOPE, :].reshape(8, 8, ck).sum(axis=0) for h in range(MLA_HEADS)]
        parts.append((krt * krt).reshape(4, 8, ck).sum(axis=0))
        parts.append(jnp.zeros((8, ck), F32))
        p_hi, p_lo = _split_bf16(jnp.concatenate(parts, axis=0))
        ss = _dot(ssel, p_hi) + _dot(ssel, p_lo)
        krg = krt * gcol
        cs = cos_ref[:, c0:c0 + ck]
        sn_ = sin_ref[:, c0:c0 + ck]
        x1, x2 = krg[0:half, :], krg[half:, :]
        roped = jnp.concatenate([x1 * cs - x2 * sn_, x2 * cs + x1 * sn_], axis=0)
        sr = _dot(qr, roped.astype(BF16))[0:8, :]
        scores.append((sn + sr) * lax.rsqrt(ss * (1.0 / QK_HEAD) + EPS))
    s_all = jnp.concatenate(scores, axis=1)

    m_old = m_sc[...]
    m_new = jnp.maximum(m_old, jnp.max(s_all, axis=1, keepdims=True))
    alpha = jnp.exp(m_old - m_new)
    p = jnp.exp(s_all - m_new[:, 0:1])
    l_sc[...] = alpha * l_sc[...] + jnp.sum(p, axis=1, keepdims=True)
    acc_sc[...] = alpha[:, 0:1] * acc_sc[...] + _dot(p.astype(BF16), lb_sc[...])
    m_sc[...] = m_new

    @pl.when(t == n_steps - 1)
    def _():
        for cp in page_copies(t, 1 - slot):
            cp.wait()

    @pl.when(j == nj - 1)
    def _():
        s_self = jnp.sum((q16 * k_ref[...].astype(F32))[0:8, :], axis=1, keepdims=True)
        m_o = m_sc[...]
        m_n = jnp.maximum(m_o, s_self)
        al = jnp.exp(m_o - m_n)
        ps = jnp.exp(s_self - m_n)
        l = al * l_sc[...] + ps
        acc = al[:, 0:1] * acc_sc[...] + ps[:, 0:1] * latn_ref[...]
        o_lat = acc / l[:, 0:1]
        res = _dot(o_lat.astype(BF16), wuv_ref[...])
        row = lax.broadcasted_iota(jnp.int32, res.shape, 0)
        lane = lax.broadcasted_iota(jnp.int32, res.shape, 1) // V_HEAD
        o_ref[...] = jnp.sum(jnp.where(row == lane, res, 0.0), axis=0, keepdims=True).astype(o_ref.dtype)


def _mla_decode(page_table, q16, k16, lat_new3, cache_lat, cache_krt, layer, lw, tabs):
    pp = DEC_PAGES
    n = q16.shape[0]
    n_pages = page_table.shape[1]
    nj = n_pages // pp
    cos_t, sin_t = tabs
    fix2 = lambda t, pt: (0, 0)
    seq3 = lambda t, pt: (t // nj, 0, 0)
    tab = lambda t, pt: (0, t % nj)
    weights = [lw["w_uk_t"], lw["g_k_nope"], lw["g_k_rope_col"], lw["ss_select"], lw["w_uv"]]
    in_specs = [pl.BlockSpec((None, 16, HEAD_PAD), seq3), pl.BlockSpec((None, 16, HEAD_PAD), seq3),
                pl.BlockSpec((None, 1, KV_RANK), seq3),
                pl.BlockSpec((QK_ROPE // 2, pp * PAGE_SIZE), tab),
                pl.BlockSpec((QK_ROPE // 2, pp * PAGE_SIZE), tab)]
    in_specs += [pl.BlockSpec(w.shape, fix2) for w in weights]
    in_specs += [pl.BlockSpec(memory_space=pl.ANY), pl.BlockSpec(memory_space=pl.ANY)]
    grid_spec = pltpu.PrefetchScalarGridSpec(
        num_scalar_prefetch=1,
        grid=(n * nj,),
        in_specs=in_specs,
        out_specs=pl.BlockSpec((None, 1, MLA_OUT), seq3),
        scratch_shapes=[pltpu.VMEM((2, pp, PAGE_SIZE, KV_RANK), F32),
                        pltpu.VMEM((2, QK_ROPE, pp * PAGE_SIZE), F32),
                        pltpu.SemaphoreType.DMA((2, 2)),
                        pltpu.VMEM((8, 128), F32), pltpu.VMEM((8, 128), F32), pltpu.VMEM((8, KV_RANK), F32),
                        pltpu.VMEM((DEC_LHS_ROWS, KV_RANK), BF16),
                        pltpu.VMEM((pp * PAGE_SIZE, KV_RANK), BF16)])
    return pl.pallas_call(
        functools.partial(_mla_decode_kernel, pp=pp, nj=nj, layer=layer),
        grid_spec=grid_spec,
        out_shape=jax.ShapeDtypeStruct((n, 1, MLA_OUT), F32),
        compiler_params=_cparams(("arbitrary",), 40),
        name="mla_decode",
    )(page_table, q16, k16, lat_new3, cos_t, sin_t, *weights, cache_lat, cache_krt)


def _dec_mix_kernel(q_ref, k_ref, la_ref, v_ref, gr_ref, u_ref, s_ref, pb_ref, gout_ref, wp_ref, sc_ref,
                    s_o, gla_o, pool_o, pb_o, *, past):
    a = jnp.exp(la_ref[...])
    v4 = v_ref[...]
    vexp = jnp.concatenate([jnp.broadcast_to(v4[h:h + 1, :], (GLA_DK, GLA_DV)) for h in range(GLA_HEADS)],
                           axis=0)
    s_new = a * s_ref[...] + k_ref[...] * vexp
    s_o[...] = s_new
    prod = (q_ref[...] * (GLA_DK ** -0.5)) * s_new
    gr = gr_ref[...]
    gout = gout_ref[...]
    for h in range(GLA_HEADS):
        o = jnp.sum(prod[h * GLA_DK:(h + 1) * GLA_DK, :], axis=0, keepdims=True)
        on = _rms(o, gout)
        g = gr[h:h + 1, :]
        gla_o[h:h + 1, :] = (on * (g * _sigmoid(g))).astype(gla_o.dtype)

    st = pb_ref[...]
    u = u_ref[...]
    ridx = lax.broadcasted_iota(jnp.int32, st.shape, 0)
    parts = []
    for w in POOL_WINDOWS:
        tot = u + jnp.sum(jnp.where(ridx >= POOL_BUF + 1 - w, st, 0.0), axis=0, keepdims=True)
        parts.append(tot / float(min(w, past + 1)) - u)
    pooled = jnp.broadcast_to(_pool_select(parts), (8, POOL_WIDTH)).astype(BF16)
    y = _dot(pooled, wp_ref[...])[0:1, :] * sc_ref[...]
    pool_o[...] = y.astype(pool_o.dtype)
    pb_o[0:POOL_BUF - 1, :] = st[1:POOL_BUF, :]
    pb_o[POOL_BUF - 1:POOL_BUF, :] = u


def _dec_mix(gq, gk, la, gv, gr, u, state_gla4, state_pool, layer, lw, past):
    n = gq.shape[0]
    col = lambda x: x.reshape(n, GLA_QK, 1)
    b3 = lambda b: (b, 0, 0)
    fix = lambda b: (0, 0)
    return pl.pallas_call(
        functools.partial(_dec_mix_kernel, past=past),
        grid=(n,),
        in_specs=[pl.BlockSpec((None, GLA_QK, 1), b3)] * 3
        + [pl.BlockSpec((None, GLA_HEADS, GLA_DV), b3)] * 2
        + [pl.BlockSpec((None, 1, POOL_WIDTH), b3),
           pl.BlockSpec((None, None, GLA_QK, GLA_DV), lambda b: (b, layer, 0, 0)),
           pl.BlockSpec((None, None, POOL_BUF, POOL_WIDTH), lambda b: (b, layer, 0, 0)),
           pl.BlockSpec((1, GLA_DV), fix), pl.BlockSpec((POOL_WIDTH, POOL_WIDTH), fix),
           pl.BlockSpec((1, POOL_WIDTH), fix)],
        out_specs=[pl.BlockSpec((None, GLA_QK, GLA_DV), b3), pl.BlockSpec((None, GLA_HEADS, GLA_DV), b3),
                   pl.BlockSpec((None, 1, POOL_WIDTH), b3), pl.BlockSpec((None, POOL_BUF, POOL_WIDTH), b3)],
        out_shape=[jax.ShapeDtypeStruct((n, GLA_QK, GLA_DV), F32),
                   jax.ShapeDtypeStruct((n, GLA_HEADS, GLA_DV), F32),
                   jax.ShapeDtypeStruct((n, 1, POOL_WIDTH), F32),
                   jax.ShapeDtypeStruct((n, POOL_BUF, POOL_WIDTH), F32)],
        compiler_params=_cparams(("parallel",), 32),
        name="dec_mix",
    )(col(gq), col(gk), col(la), gv.reshape(n, GLA_HEADS, GLA_DV), gr.reshape(n, GLA_HEADS, GLA_DV),
      u.reshape(n, 1, POOL_WIDTH), state_gla4, state_pool, lw["g_gla_out64"], lw["w_pool"], lw["pool_scale"])


def _post_a_kernel(x_ref, mla_ref, gla_ref, pool_ref, wo_ref, gxa_ref, wxq_ref, gxq_ref, x1_ref, q_ref):
    x1 = _mix_out(x_ref[...], mla_ref[...], gla_ref[...], pool_ref[...], wo_ref)
    x1_ref[...] = x1
    q_ref[...] = _xa_query(x1, gxa_ref[...], wxq_ref[...], gxq_ref[...])


def _post_a(x2, mla, gla, pool, lw):
    m = x2.shape[0]
    return pl.pallas_call(
        _post_a_kernel,
        out_shape=[jax.ShapeDtypeStruct((m, D_MODEL), F32), jax.ShapeDtypeStruct((m, D_MODEL), F32)],
        compiler_params=pltpu.CompilerParams(vmem_limit_bytes=32 * MIB),
        name="post_a",
    )(x2, mla, gla, pool, lw["w_out"], lw["norm_xa"], lw["w_xq"], lw["g_xq"])


XA_HALVES = XA_HEAD // 128
XA_ROWS = XA_HALVES * XA_HEADS


def _xa_decode_kernel(q_ref, mk_ref, mv_ref, o_ref):
    k3 = mk_ref[...].reshape(N_MEM, XA_ROWS, 128)
    v3 = mv_ref[...].reshape(N_MEM, XA_ROWS, 128)
    s = jnp.sum(k3 * q_ref[...][None], axis=-1, keepdims=True)
    s = s[:, 0:XA_HEADS] + s[:, XA_HEADS:XA_ROWS]
    s = jnp.concatenate([s, s], axis=1)
    e = jnp.exp(s - jnp.max(s, axis=0, keepdims=True))
    p = e / jnp.sum(e, axis=0, keepdims=True)
    o_ref[...] = jnp.sum(p * v3, axis=0)


def _xa_decode(q3, cache_k, cache_v, layer):
    n = q3.shape[0]
    depth = cache_k.shape[1]
    view = lambda c: c.reshape(n, depth, N_MEM, XA_HEADS, XA_HALVES, 128).transpose(0, 1, 2, 4, 3, 5).reshape(
        n, depth, N_MEM * XA_ROWS, 128)
    q8 = q3.reshape(n, XA_HEADS, XA_HALVES, 128).transpose(0, 2, 1, 3).reshape(n, XA_ROWS, 128)
    b3 = lambda b: (b, 0, 0)
    mem = lambda b: (b, layer, 0, 0)
    o = pl.pallas_call(
        _xa_decode_kernel,
        grid=(n,),
        in_specs=[pl.BlockSpec((None, XA_ROWS, 128), b3),
                  pl.BlockSpec((None, None, N_MEM * XA_ROWS, 128), mem),
                  pl.BlockSpec((None, None, N_MEM * XA_ROWS, 128), mem)],
        out_specs=pl.BlockSpec((None, XA_ROWS, 128), b3),
        out_shape=jax.ShapeDtypeStruct((n, XA_ROWS, 128), F32),
        compiler_params=_cparams(("parallel",), 32),
        name="xa_decode",
    )(q8, view(cache_k), view(cache_v))
    return o.reshape(n, XA_HALVES, XA_HEADS, 128).transpose(0, 2, 1, 3).reshape(n, 1, D_MODEL)


def _post_b_kernel(x1_ref, o_ref, wxo_ref, x2_ref):
    x2_ref[...] = x1_ref[...] + _dot(o_ref[...].astype(BF16), wxo_ref[...])


def _post_b(x1, o, lw):
    return pl.pallas_call(
        _post_b_kernel,
        out_shape=jax.ShapeDtypeStruct(x1.shape, F32),
        compiler_params=pltpu.CompilerParams(vmem_limit_bytes=32 * MIB),
        name="post_b",
    )(x1, o, lw["w_xo"])


def _rope_tables_rows(pos):
    half = QK_ROPE // 2
    inv = ROPE_THETA ** (-jnp.arange(half, dtype=F32) / half)
    ang = pos.astype(F32)[:, None] * inv[None, :]
    cos, sin = jnp.cos(ang), jnp.sin(ang)
    n = pos.shape[0]
    one = jnp.ones((n, QK_NOPE), F32)
    zero = jnp.zeros((n, QK_NOPE), F32)
    pad1 = jnp.ones((n, HEAD_PAD - QK_HEAD), F32)
    pad0 = jnp.zeros((n, HEAD_PAD - QK_HEAD), F32)
    zh = jnp.zeros((n, half), F32)
    c = jnp.concatenate([one, cos, cos, pad1], axis=1)
    s1 = jnp.concatenate([zero, -sin, zh, pad0], axis=1)
    s2 = jnp.concatenate([zero, zh, sin, pad0], axis=1)
    return c, s1, s2


def _rope_tables_cols(pos):
    half = QK_ROPE // 2
    inv = ROPE_THETA ** (-jnp.arange(half, dtype=F32) / half)
    ang = pos.astype(F32)[:, None] * inv[None, :]
    return jnp.cos(ang).T, jnp.sin(ang).T


def _pad_heads(w, width):
    pad = [(0, 0)] * (w.ndim - 1) + [(0, HEAD_PAD - width)]
    w = jnp.pad(w, pad)
    return w.reshape(w.shape[:-2] + (w.shape[-2] * HEAD_PAD,))


def _layer_weights(l, p):
    w_in = p["w_in"][l]
    sizes = (Q_RANK, KV_RANK, QK_ROPE, GLA_QK, GLA_QK, GLA_OUT, GLA_GATE_RANK, GLA_OUT, POOL_WIDTH)
    offs = [0]
    for s in sizes:
        offs.append(offs[-1] + s)
    piece = lambda i: w_in[:, offs[i]:offs[i + 1]]
    w_in_p = jnp.concatenate([piece(0), piece(1), piece(3), piece(4), piece(5), piece(7), piece(8), piece(2),
                              piece(6), jnp.zeros((D_MODEL, IN_PAD - sum(sizes)), F32)], axis=1)
    g_q = _pad_heads(jnp.broadcast_to(p["g_qk_q"][l], (MLA_HEADS, QK_HEAD)), QK_HEAD)
    g_k = _pad_heads(jnp.broadcast_to(p["g_qk_k"][l], (MLA_HEADS, QK_HEAD)), QK_HEAD)
    place = jnp.zeros((128, QK_W), F32)
    j = jnp.arange(QK_ROPE)
    for h in range(MLA_HEADS):
        place = place.at[j, h * HEAD_PAD + QK_NOPE + j].set(1.0)
    w_gate = jnp.zeros((128, GLA_QK), F32).at[QK_ROPE:QK_ROPE + GLA_GATE_RANK].set(p["w_gate_up"][l])
    wp = p["w_pool"][l]
    w_pool = jnp.zeros((POOL_WIDTH, POOL_WIDTH), F32)
    for g in range(len(POOL_WINDOWS)):
        w_pool = w_pool.at[g * POOL_GROUP:(g + 1) * POOL_GROUP, g * POOL_GROUP:(g + 1) * POOL_GROUP].set(wp[g])
    blk = jnp.arange(GLA_OUT) // GLA_DV
    row = lambda v: v.reshape(1, -1).astype(F32)
    bf = lambda v: v.astype(BF16)
    w_uk2 = p["w_uk"][l].reshape(KV_RANK, MLA_HEADS * QK_NOPE)
    g_k_nope = jnp.zeros((1, HEAD_PAD), F32).at[0, :QK_NOPE].set(p["g_qk_k"][l][:QK_NOPE])
    col = jnp.arange(DEC_SS_ROWS)[None, :] // 8
    hrow = jnp.arange(8)[:, None]
    ss_select = ((hrow < MLA_HEADS) & ((col == hrow) | (col == MLA_HEADS))).astype(F32)
    return {
        "norm_mix": row(p["norm_mix"][l]), "w_in": bf(w_in_p), "g_q_lat": row(p["g_q_lat"][l]),
        "w_uq": bf(_pad_heads(p["w_uq"][l], QK_HEAD)), "g_kv_lat": row(p["g_kv_lat"][l]),
        "w_uk": bf(_pad_heads(p["w_uk"][l], QK_NOPE)), "w_uv": bf(p["w_uv"][l].reshape(KV_RANK, MLA_OUT)),
        "g_qk_q": row(g_q), "g_qk_k": row(g_k), "place": bf(place), "w_gate": bf(w_gate),
        "b_gate": row(p["b_gate"][l]),
        "e64": bf((blk[:, None] == blk[None, :]).astype(F32)),
        "g_gla_out": row(jnp.tile(p["g_gla_out"][l], GLA_HEADS)), "g_gla_out64": row(p["g_gla_out"][l]),
        "w_pool": bf(w_pool), "pool_scale": row(p["pool_scale"][l]),
        "w_out": bf(p["w_out"][l]), "norm_xa": row(p["norm_xa"][l]), "w_xq": bf(p["w_xq"][l]),
        "g_xq": row(jnp.tile(p["g_xq"][l], XA_HEADS)), "w_xo": bf(p["w_xo"][l]),
        "norm_mlp": row(p["norm_mlp"][l]), "w_ff1": bf(p["w_ff1"][l]), "w_ff2": bf(p["w_ff2"][l]),
        "norm_mem": row(p["norm_mem"][l]), "w_xk": bf(p["w_xk"][l]), "w_xv": bf(p["w_xv"][l]),
        "g_xk": row(jnp.tile(p["g_xk"][l], XA_HEADS)),
        "w_uk_t": bf(w_uk2.T), "g_k_nope": g_k_nope,
        "g_k_rope_col": jnp.broadcast_to(p["g_qk_k"][l][QK_NOPE:, None], (QK_ROPE, DEC_CHUNK)).astype(F32),
        "ss_select": bf(ss_select),
    }


def _prompt_layer(x2, mem2, lw, ropes, b, t):
    mk, mv, mkb, mvb = _memory_kv(mem2, lw["norm_mem"], lw["w_xk"], lw["w_xv"], lw["g_xk"])
    lat, kr, q, k, v, gq, gk, gv, la, gr, u = _in_proj(x2, lw, ropes, 512)
    r3 = lambda a: a.reshape(b, t, a.shape[-1])
    mla = _mla_prompt(r3(q), r3(k), r3(v), 512).reshape(b * t, MLA_OUT)
    gla, st = _gla_prompt(r3(gq), r3(gk), r3(gv), r3(la), r3(gr), lw)
    pool = _pool_prompt(r3(u), lw, 512).reshape(b * t, POOL_WIDTH)
    x2 = _post_prompt(x2, mla, gla.reshape(b * t, GLA_OUT), pool, mkb.reshape(b, N_MEM, D_MODEL),
                      mvb.reshape(b, N_MEM, D_MODEL), lw, t, 512)
    x2 = _mlp(x2, lw, 1024)
    gla_state = jnp.stack([st[:, h * GLA_DV:(h + 1) * GLA_DV, h * GLA_DK:(h + 1) * GLA_DK]
                           for h in range(GLA_HEADS)], axis=1).transpose(0, 1, 3, 2)
    outs = (lat.reshape(b, t, KV_RANK), kr.reshape(b, t, QK_ROPE), gla_state,
            r3(u)[:, t - POOL_BUF:], mk.reshape(b, N_MEM, XA_HEADS, XA_HEAD),
            mv.reshape(b, N_MEM, XA_HEADS, XA_HEAD))
    return x2, outs


def _sample_layer(x2, layer, lw, ropes, tabs, page_table, cache_lat, cache_krt, state_gla4, state_pool,
                  cache_mk, cache_mv, past):
    n = x2.shape[0]
    lat, kr, q, k, v, gq, gk, gv, la, gr, u = _in_proj(x2, lw, ropes, n)
    pad16 = lambda a: jnp.pad(a.reshape(n, MLA_HEADS, HEAD_PAD), ((0, 0), (0, 16 - MLA_HEADS), (0, 0)))
    mla = _mla_decode(page_table, pad16(q), pad16(k), lat.reshape(n, 1, KV_RANK), cache_lat, cache_krt, layer,
                      lw, tabs).reshape(n, MLA_OUT)
    s_new, gla, pool, pb = _dec_mix(gq, gk, la, gv, gr, u, state_gla4, state_pool, layer, lw, past)
    x1, qx = _post_a(x2, mla, gla.reshape(n, GLA_OUT), pool.reshape(n, POOL_WIDTH), lw)
    ox = _xa_decode(qx.reshape(n, 1, D_MODEL), cache_mk, cache_mv, layer).reshape(n, D_MODEL)
    x2 = _mlp(_post_b(x1, ox, lw), lw, n)
    outs = (lat.reshape(n, 1, KV_RANK), kr.reshape(n, 1, QK_ROPE),
            s_new.reshape(n, GLA_HEADS, GLA_DK, GLA_DV), pb)
    return x2, outs


def kernel(x_prompt, x_sample, mem_prompt, cache_mla_latent, cache_mla_krope, state_gla, state_pool,
           cache_mem_k, cache_mem_v, page_table, norm_mix, w_in, g_q_lat, w_uq, g_kv_lat, w_uk, w_uv,
           g_qk_q, g_qk_k, w_gate_up, b_gate, g_gla_out, w_pool, pool_scale, w_out, norm_xa, norm_mem,
           w_xq, w_xk, w_xv, g_xq, g_xk, w_xo, norm_mlp, w_ff1, w_ff2):
    params = dict(norm_mix=norm_mix, w_in=w_in, g_q_lat=g_q_lat, w_uq=w_uq, g_kv_lat=g_kv_lat, w_uk=w_uk,
                  w_uv=w_uv, g_qk_q=g_qk_q, g_qk_k=g_qk_k, w_gate_up=w_gate_up, b_gate=b_gate,
                  g_gla_out=g_gla_out, w_pool=w_pool, pool_scale=pool_scale, w_out=w_out, norm_xa=norm_xa,
                  norm_mem=norm_mem, w_xq=w_xq, w_xk=w_xk, w_xv=w_xv, g_xq=g_xq, g_xk=g_xk, w_xo=w_xo,
                  norm_mlp=norm_mlp, w_ff1=w_ff1, w_ff2=w_ff2)
    b, t, _ = x_prompt.shape
    n, dec_seq, _ = x_sample.shape
    depth = norm_mix.shape[0]
    n_pages = page_table.shape[1]
    past = n_pages * PAGE_SIZE

    ropes_p = _rope_tables_rows(jnp.arange(t, dtype=jnp.int32))
    ropes_s = _rope_tables_rows(jnp.full((n,), past, jnp.int32))
    tabs = _rope_tables_cols(jnp.arange(past, dtype=jnp.int32))

    yp = x_prompt.reshape(b * t, D_MODEL)
    ys = x_sample.reshape(n * dec_seq, D_MODEL)
    mem2 = mem_prompt.reshape(b * N_MEM, D_MODEL)
    state_gla4 = state_gla.reshape(n, depth, GLA_QK, GLA_DV)
    cache_krt = cache_mla_krope.transpose(0, 1, 3, 2)

    p_outs, s_outs = [], []
    for l in range(depth):
        lw = _layer_weights(l, params)
        yp, po = _prompt_layer(yp, mem2, lw, ropes_p, b, t)
        ys, so = _sample_layer(ys, l, lw, ropes_s, tabs, page_table, cache_mla_latent, cache_krt,
                               state_gla4, state_pool, cache_mem_k, cache_mem_v, past)
        p_outs.append(po)
        s_outs.append(so)

    stack = lambda outs, i: jnp.stack([o[i] for o in outs], axis=1)
    return (yp.reshape(b, t, D_MODEL), ys.reshape(n, dec_seq, D_MODEL),
            stack(p_outs, 0), stack(p_outs, 1), stack(p_outs, 2), stack(p_outs, 3), stack(p_outs, 4),
            stack(p_outs, 5), stack(s_outs, 0), stack(s_outs, 1), stack(s_outs, 2), stack(s_outs, 3))
```

```python
import functools

import jax
import jax.numpy as jnp
from jax import lax
from jax.experimental import pallas as pl
from jax.experimental.pallas import tpu as pltpu

F32 = jnp.float32
BF16 = jnp.bfloat16

D_MODEL = 1024
MLA_HEADS = 4
QK_NOPE = 64
QK_ROPE = 32
QK_HEAD = QK_NOPE + QK_ROPE
V_HEAD = 128
Q_RANK = 384
KV_RANK = 256
ROPE_THETA = 10000.0
GLA_HEADS = 4
GLA_DK = 32
GLA_DV = 64
GLA_GATE_RANK = 16
GLA_TAU = 16.0
GLA_CHUNK = 64
GLA_GROUP = 256
POOL_WINDOWS = (2, 4, 8, 16)
POOL_GROUP = 64
POOL_WIDTH = POOL_GROUP * len(POOL_WINDOWS)
POOL_BUF = max(POOL_WINDOWS) - 1
MLA_OUT = MLA_HEADS * V_HEAD
GLA_OUT = GLA_HEADS * GLA_DV
N_MEM = 256
XA_HEADS = 4
XA_HEAD = D_MODEL // XA_HEADS
D_FF = 4 * D_MODEL
PAGE_SIZE = 128
EPS = 1e-6

HEAD_PAD = 128
QK_W = MLA_HEADS * HEAD_PAD
GLA_QK = GLA_HEADS * GLA_DK
IN_PAD = 1792
MISC0 = 1664

MIB = 1024 * 1024


def _cparams(sem, vmem_mib):
    return pltpu.CompilerParams(dimension_semantics=sem, vmem_limit_bytes=vmem_mib * MIB)


def _rms(x, g):
    y = x * lax.rsqrt(jnp.mean(x * x, axis=-1, keepdims=True) + EPS)
    return y * g


def _head_rms(x, g, nh, width, count):
    outs = []
    for h in range(nh):
        p = x[:, h * width:(h + 1) * width]
        ms = jnp.sum(p * p, axis=-1, keepdims=True) * (1.0 / count)
        outs.append(p * lax.rsqrt(ms + EPS) * g[:, h * width:(h + 1) * width])
    return jnp.concatenate(outs, axis=1)


def _dot(a, b):
    return jnp.dot(a, b, preferred_element_type=F32)


def _dot_nt(a, b):
    return lax.dot_general(a, b, (((1,), (1,)), ((), ())), preferred_element_type=F32)


def _split_bf16(x):
    hi = x.astype(BF16)
    lo = (x - hi.astype(F32)).astype(BF16)
    return hi, lo


def _dot_split(x, w):
    hi, lo = _split_bf16(x)
    return _dot(hi, w) + _dot(lo, w)


def _rope_rows(x, c, s1, s2):
    n = x.shape[1]
    c4 = jnp.concatenate([c] * MLA_HEADS, axis=1)
    s14 = jnp.concatenate([s1] * MLA_HEADS, axis=1)
    s24 = jnp.concatenate([s2] * MLA_HEADS, axis=1)
    half = QK_ROPE // 2
    return x * c4 + pltpu.roll(x, n - half, 1) * s14 + pltpu.roll(x, half, 1) * s24


def _log_sigmoid(x):
    return -(jnp.maximum(-x, 0.0) + jnp.log1p(jnp.exp(-jnp.abs(x))))


def _sigmoid(x):
    return 1.0 / (1.0 + jnp.exp(-x))


def _memkv_kernel(mem_ref, g_ref, wk_ref, wv_ref, gk_ref, k_ref, v_ref, kb_ref, vb_ref):
    m = _rms(mem_ref[...], g_ref[...]).astype(BF16)
    k = _head_rms(_dot(m, wk_ref[...]), gk_ref[...], XA_HEADS, XA_HEAD, XA_HEAD)
    v = _dot(m, wv_ref[...])
    k_ref[...] = k
    v_ref[...] = v
    kb_ref[...] = k.astype(BF16)
    vb_ref[...] = v.astype(BF16)


def _memory_kv(mem2, g, wk, wv, gk):
    m = mem2.shape[0]
    tm = 512
    row = lambda i: (i, 0)
    fix = lambda i: (0, 0)
    return pl.pallas_call(
        _memkv_kernel,
        grid=(m // tm,),
        in_specs=[pl.BlockSpec((tm, D_MODEL), row), pl.BlockSpec((1, D_MODEL), fix),
                  pl.BlockSpec((D_MODEL, D_MODEL), fix), pl.BlockSpec((D_MODEL, D_MODEL), fix),
                  pl.BlockSpec((1, D_MODEL), fix)],
        out_specs=[pl.BlockSpec((tm, D_MODEL), row)] * 4,
        out_shape=[jax.ShapeDtypeStruct((m, D_MODEL), F32)] * 2
        + [jax.ShapeDtypeStruct((m, D_MODEL), BF16)] * 2,
        compiler_params=_cparams(("parallel",), 40),
        name="memory_kv",
    )(mem2, g, wk, wv, gk)


def _inproj_kernel(x_ref, gmix_ref, win_ref, gql_ref, wuq_ref, gkv_ref, wuk_ref, wuv_ref,
                   gq_ref, gk_ref, place_ref, wg_ref, bg_ref, c_ref, s1_ref, s2_ref,
                   lat_ref, kr_ref, q_ref, k_ref, v_ref, gq_o, gk_o, gv_o, la_o, gr_o, u_o):
    h = _rms(x_ref[...], gmix_ref[...]).astype(BF16)
    z = _dot(h, win_ref[...])
    c_q = z[:, 0:384]
    c_kv = z[:, 384:640]
    misc = z[:, MISC0:MISC0 + 128]
    gq_o[...] = z[:, 640:768]
    gk_o[...] = z[:, 768:896]
    gv_o[...] = z[:, 896:1152]
    gr_o[...] = z[:, 1152:1408]
    u_o[...] = z[:, 1408:1664]
    kr_ref[...] = misc[:, 0:QK_ROPE]

    lat = _rms(c_kv, gkv_ref[...])
    lat_ref[...] = lat
    latb = lat.astype(BF16)
    c, s1, s2 = c_ref[...], s1_ref[...], s2_ref[...]

    q = _dot(_rms(c_q, gql_ref[...]).astype(BF16), wuq_ref[...])
    q = _head_rms(q, gq_ref[...], MLA_HEADS, HEAD_PAD, QK_HEAD)
    q_ref[...] = (_rope_rows(q, c, s1, s2) * (QK_HEAD ** -0.5)).astype(BF16)

    k = _dot(latb, wuk_ref[...]) + _dot_split(misc, place_ref[...])
    k = _head_rms(k, gk_ref[...], MLA_HEADS, HEAD_PAD, QK_HEAD)
    k_ref[...] = _rope_rows(k, c, s1, s2).astype(BF16)
    v_ref[...] = _dot(latb, wuv_ref[...]).astype(BF16)

    gate = _dot(misc.astype(BF16), wg_ref[...]) + bg_ref[...]
    la_o[...] = _log_sigmoid(gate) * (1.0 / GLA_TAU)


def _in_proj(x2, lw, ropes, tm):
    m = x2.shape[0]
    row = lambda i: (i, 0)
    fix = lambda i: (0, 0)
    c, s1, s2 = ropes
    if c.shape[0] == m:
        rope_map = row
    else:
        nb = c.shape[0] // tm
        rope_map = lambda i: (i % nb, 0)
    weights = [lw["norm_mix"], lw["w_in"], lw["g_q_lat"], lw["w_uq"], lw["g_kv_lat"], lw["w_uk"],
               lw["w_uv"], lw["g_qk_q"], lw["g_qk_k"], lw["place"], lw["w_gate"], lw["b_gate"]]
    in_specs = [pl.BlockSpec((tm, D_MODEL), row)]
    in_specs += [pl.BlockSpec(w.shape, fix) for w in weights]
    in_specs += [pl.BlockSpec((tm, HEAD_PAD), rope_map)] * 3
    widths = [(KV_RANK, F32), (QK_ROPE, F32), (QK_W, BF16), (QK_W, BF16), (MLA_OUT, BF16),
              (GLA_QK, F32), (GLA_QK, F32), (GLA_OUT, F32), (GLA_QK, F32), (GLA_OUT, F32),
              (POOL_WIDTH, F32)]
    return pl.pallas_call(
        _inproj_kernel,
        grid=(m // tm,),
        in_specs=in_specs,
        out_specs=[pl.BlockSpec((tm, w), row) for w, _ in widths],
        out_shape=[jax.ShapeDtypeStruct((m, w), dt) for w, dt in widths],
        compiler_params=_cparams(("parallel",), 48),
        name="in_proj",
    )(x2, *weights, c, s1, s2)


def _flash_kernel(q_ref, k_ref, v_ref, o_ref, *, tq):
    qi = pl.program_id(2)
    q = q_ref[...]

    def block(j, carry, masked):
        m, l, acc = carry
        start = pl.multiple_of(j * tq, tq)
        k = k_ref[pl.ds(start, tq), :]
        v = v_ref[pl.ds(start, tq), :]
        s = _dot_nt(q, k)
        if masked:
            rows = lax.broadcasted_iota(jnp.int32, (tq, tq), 0)
            cols = lax.broadcasted_iota(jnp.int32, (tq, tq), 1)
            s = jnp.where(cols <= rows, s, -jnp.inf)
        m_new = jnp.maximum(m, jnp.max(s, axis=-1, keepdims=True))
        alpha = jnp.exp(m - m_new)
        p = jnp.exp(s - m_new)
        l = alpha * l + jnp.sum(p, axis=-1, keepdims=True)
        acc = alpha * acc + _dot(p.astype(BF16), v)
        return m_new, l, acc

    init = (jnp.full((tq, 1), -jnp.inf, F32), jnp.zeros((tq, 1), F32), jnp.zeros((tq, V_HEAD), F32))
    carry = lax.fori_loop(0, qi, lambda j, cr: block(j, cr, False), init)
    _, l, acc = block(qi, carry, True)
    o_ref[...] = (acc / l).astype(o_ref.dtype)


def _mla_prompt(q3, k3, v3, tq):
    b, t, _ = q3.shape
    qmap = lambda bi, h, qi: (bi, qi, h)
    kmap = lambda bi, h, qi: (bi, 0, h)
    return pl.pallas_call(
        functools.partial(_flash_kernel, tq=tq),
        grid=(b, MLA_HEADS, t // tq),
        in_specs=[pl.BlockSpec((None, tq, HEAD_PAD), qmap), pl.BlockSpec((None, t, HEAD_PAD), kmap),
                  pl.BlockSpec((None, t, V_HEAD), kmap)],
        out_specs=pl.BlockSpec((None, tq, V_HEAD), qmap),
        out_shape=jax.ShapeDtypeStruct((b, t, MLA_OUT), BF16),
        compiler_params=_cparams(("parallel", "parallel", "arbitrary"), 40),
        name="mla_prompt",
    )(q3, k3, v3)


def _gla_kernel(q_ref, k_ref, v_ref, la_ref, gr_ref, tri_ref, ones_ref, e64_ref, gout_ref, o_ref, st_ref, *, t):
    c = GLA_CHUNK
    gt = GLA_GROUP
    tri = tri_ref[...]
    ones_bd = ones_ref[...]
    e64 = e64_ref[...]
    gout = gout_ref[...]
    lane_qk = lax.broadcasted_iota(jnp.int32, (gt, GLA_QK), 1) // GLA_DK
    lane_v = lax.broadcasted_iota(jnp.int32, (gt, GLA_OUT), 1) // GLA_DV
    row_chunk = lax.broadcasted_iota(jnp.int32, (gt, GLA_QK), 0) // c
    rr = lax.broadcasted_iota(jnp.int32, (gt, gt), 0)
    cc = lax.broadcasted_iota(jnp.int32, (gt, gt), 1)
    causal = (cc <= rr) & (rr // c == cc // c)
    bd = (lax.broadcasted_iota(jnp.int32, (GLA_OUT, GLA_QK), 0) // GLA_DV
          == lax.broadcasted_iota(jnp.int32, (GLA_OUT, GLA_QK), 1) // GLA_DK)

    def group(gi, st):
        r0 = pl.multiple_of(gi * gt, gt)
        q = q_ref[pl.ds(r0, gt), :] * (GLA_DK ** -0.5)
        k = k_ref[pl.ds(r0, gt), :]
        v = v_ref[pl.ds(r0, gt), :]
        a_hi, a_lo = _split_bf16(la_ref[pl.ds(r0, gt), :])
        b = _dot(tri, a_hi) + _dot(tri, a_lo)
        tot = _dot(ones_bd, a_hi) + _dot(ones_bd, a_lo)
        qd = q * jnp.exp(b)
        qdb = qd.astype(BF16)
        kd = (k * jnp.exp(-b)).astype(BF16)
        k2 = k * jnp.exp(tot - b)
        decay = jnp.exp(tot)
        vb = v.astype(BF16)
        vt = v.T.astype(BF16)
        o = jnp.zeros((gt, GLA_OUT), F32)
        for h in range(GLA_HEADS):
            qh = jnp.where(lane_qk == h, qd, 0.0).astype(BF16)
            att = jnp.where(causal, _dot_nt(qh, kd), 0.0)
            o = o + jnp.where(lane_v == h, _dot(att.astype(BF16), vb), 0.0)
        inter = []
        for ci in range(gt // c):
            inter.append(_dot_nt(qdb[ci * c:(ci + 1) * c, :], st.astype(BF16)))
            k2c = jnp.where(row_chunk == ci, k2, 0.0).astype(BF16)
            st = st * decay[ci * c:ci * c + 1, :] + jnp.where(bd, _dot(vt, k2c), 0.0)
        o = o + jnp.concatenate(inter, axis=0)
        ms = _dot_split(o * o, e64) * (1.0 / GLA_DV)
        on = o * lax.rsqrt(ms + EPS) * gout
        gr = gr_ref[pl.ds(r0, gt), :]
        o_ref[pl.ds(r0, gt), :] = (on * (gr * _sigmoid(gr))).astype(o_ref.dtype)
        return st

    st_ref[...] = lax.fori_loop(0, t // gt, group, jnp.zeros((GLA_OUT, GLA_QK), F32))


def _gla_prompt(gq3, gk3, gv3, la3, gr3, lw):
    b, t, _ = gq3.shape
    bmap = lambda i: (i, 0, 0)
    fix = lambda i: (0, 0)
    chunk_id = jnp.arange(GLA_GROUP) // GLA_CHUNK
    same = chunk_id[:, None] == chunk_id[None, :]
    tri = (same & (jnp.arange(GLA_GROUP)[None, :] <= jnp.arange(GLA_GROUP)[:, None])).astype(BF16)
    return pl.pallas_call(
        functools.partial(_gla_kernel, t=t),
        grid=(b,),
        in_specs=[pl.BlockSpec((None, t, GLA_QK), bmap), pl.BlockSpec((None, t, GLA_QK), bmap),
                  pl.BlockSpec((None, t, GLA_OUT), bmap), pl.BlockSpec((None, t, GLA_QK), bmap),
                  pl.BlockSpec((None, t, GLA_OUT), bmap), pl.BlockSpec((GLA_GROUP, GLA_GROUP), fix),
                  pl.BlockSpec((GLA_GROUP, GLA_GROUP), fix),
                  pl.BlockSpec((GLA_OUT, GLA_OUT), fix), pl.BlockSpec((1, GLA_OUT), fix)],
        out_specs=[pl.BlockSpec((None, t, GLA_OUT), bmap), pl.BlockSpec((None, GLA_OUT, GLA_QK), bmap)],
        out_shape=[jax.ShapeDtypeStruct((b, t, GLA_OUT), BF16),
                   jax.ShapeDtypeStruct((b, GLA_OUT, GLA_QK), F32)],
        compiler_params=_cparams(("parallel",), 40),
        name="gla_prompt",
    )(gq3, gk3, gv3, la3, gr3, tri, same.astype(BF16), lw["e64"], lw["g_gla_out"])


def _pool_select(parts):
    lane = lax.broadcasted_iota(jnp.int32, parts[0].shape, 1) // POOL_GROUP
    out = parts[-1]
    for g in range(len(parts) - 2, -1, -1):
        out = jnp.where(lane == g, parts[g], out)
    return out


def _pool_kernel(u_ref, halo_ref, w_ref, sc_ref, o_ref, *, tp):
    i = pl.program_id(1)
    u = u_ref[...]
    halo = jnp.where(i == 0, 0.0, halo_ref[...])
    ext = jnp.concatenate([halo, u], axis=0)
    hb = halo.shape[0]
    pos = i * tp + lax.broadcasted_iota(jnp.int32, (tp, 1), 0)
    sums = ext
    parts = []
    span = 1
    for w in POOL_WINDOWS:
        while span < w:
            sums = sums + pltpu.roll(sums, span, 0)
            span *= 2
        cnt = jnp.minimum(w, pos + 1).astype(F32)
        parts.append(sums[hb:, :] / cnt - u)
    pooled = _pool_select(parts)
    o_ref[...] = (_dot(pooled.astype(BF16), w_ref[...]) * sc_ref[...]).astype(o_ref.dtype)


def _pool_prompt(u3, lw, tp):
    b, t, _ = u3.shape
    hb = 16
    return pl.pallas_call(
        functools.partial(_pool_kernel, tp=tp),
        grid=(b, t // tp),
        in_specs=[pl.BlockSpec((None, tp, POOL_WIDTH), lambda bi, i: (bi, i, 0)),
                  pl.BlockSpec((None, hb, POOL_WIDTH),
                               lambda bi, i: (bi, jnp.maximum(i * (tp // hb) - 1, 0), 0)),
                  pl.BlockSpec((POOL_WIDTH, POOL_WIDTH), lambda bi, i: (0, 0)),
                  pl.BlockSpec((1, POOL_WIDTH), lambda bi, i: (0, 0))],
        out_specs=pl.BlockSpec((None, tp, POOL_WIDTH), lambda bi, i: (bi, i, 0)),
        out_shape=jax.ShapeDtypeStruct((b, t, POOL_WIDTH), BF16),
        compiler_params=_cparams(("parallel", "parallel"), 32),
        name="pool_prompt",
    )(u3, u3, lw["w_pool"], lw["pool_scale"])


def _mix_out(x, mla, gla, pool, wo_ref):
    return (x + _dot(mla.astype(BF16), wo_ref[0:MLA_OUT, :])
            + _dot(gla.astype(BF16), wo_ref[MLA_OUT:MLA_OUT + GLA_OUT, :])
            + _dot(pool.astype(BF16), wo_ref[MLA_OUT + GLA_OUT:, :]))


def _xa_query(x1, gxa, wxq, gxq):
    q = _dot(_rms(x1, gxa).astype(BF16), wxq)
    return _head_rms(q, gxq, XA_HEADS, XA_HEAD, XA_HEAD) * (XA_HEAD ** -0.5)


def _xa_attend(qb, kb, vb):
    outs = []
    for h in range(XA_HEADS):
        sl = slice(h * XA_HEAD, (h + 1) * XA_HEAD)
        s = _dot_nt(qb[:, sl], kb[:, sl])
        e = jnp.exp(s - jnp.max(s, axis=-1, keepdims=True))
        p = e / jnp.sum(e, axis=-1, keepdims=True)
        outs.append(_dot(p.astype(BF16), vb[:, sl]))
    return jnp.concatenate(outs, axis=1).astype(BF16)


def _post_kernel(x_ref, mla_ref, gla_ref, pool_ref, wo_ref, gxa_ref, wxq_ref, gxq_ref, mk_ref, mv_ref,
                 wxo_ref, o_ref):
    x1 = _mix_out(x_ref[...], mla_ref[...], gla_ref[...], pool_ref[...], wo_ref)
    qb = _xa_query(x1, gxa_ref[...], wxq_ref[...], gxq_ref[...]).astype(BF16)
    o = _xa_attend(qb, mk_ref[...], mv_ref[...])
    o_ref[...] = x1 + _dot(o, wxo_ref[...])


def _post_prompt(x2, mla, gla, pool, mk3, mv3, lw, t, tm):
    m = x2.shape[0]
    per_seq = t // tm
    row = lambda i: (i, 0)
    fix = lambda i: (0, 0)
    mem = lambda i: (i // per_seq, 0, 0)
    sq = (D_MODEL, D_MODEL)
    return pl.pallas_call(
        _post_kernel,
        grid=(m // tm,),
        in_specs=[pl.BlockSpec((tm, D_MODEL), row), pl.BlockSpec((tm, MLA_OUT), row),
                  pl.BlockSpec((tm, GLA_OUT), row), pl.BlockSpec((tm, POOL_WIDTH), row),
                  pl.BlockSpec(sq, fix), pl.BlockSpec((1, D_MODEL), fix), pl.BlockSpec(sq, fix),
                  pl.BlockSpec((1, D_MODEL), fix), pl.BlockSpec((None, N_MEM, D_MODEL), mem),
                  pl.BlockSpec((None, N_MEM, D_MODEL), mem), pl.BlockSpec(sq, fix)],
        out_specs=pl.BlockSpec((tm, D_MODEL), row),
        out_shape=jax.ShapeDtypeStruct((m, D_MODEL), F32),
        compiler_params=_cparams(("parallel",), 48),
        name="post_prompt",
    )(x2, mla, gla, pool, lw["w_out"], lw["norm_xa"], lw["w_xq"], lw["g_xq"], mk3, mv3, lw["w_xo"])


def _mlp_kernel(x_ref, g_ref, w1_ref, w2_ref, o_ref):
    x = x_ref[...]
    h = _rms(x, g_ref[...]).astype(BF16)
    a = jnp.maximum(_dot(h, w1_ref[...]), 0.0)
    o_ref[...] = x + _dot((a * a).astype(BF16), w2_ref[...])


def _mlp(x2, lw, tm):
    m = x2.shape[0]
    once = pl.Buffered(1)
    return pl.pallas_call(
        _mlp_kernel,
        grid=(m // tm,),
        in_specs=[pl.BlockSpec((tm, D_MODEL), lambda i: (i, 0)),
                  pl.BlockSpec((1, D_MODEL), lambda i: (0, 0)),
                  pl.BlockSpec((D_MODEL, D_FF), lambda i: (0, 0), pipeline_mode=once),
                  pl.BlockSpec((D_FF, D_MODEL), lambda i: (0, 0), pipeline_mode=once)],
        out_specs=pl.BlockSpec((tm, D_MODEL), lambda i: (i, 0)),
        out_shape=jax.ShapeDtypeStruct((m, D_MODEL), F32),
        compiler_params=_cparams(("parallel",), 56),
        name="mlp",
    )(x2, lw["norm_mlp"], lw["w_ff1"], lw["w_ff2"])


DEC_PAGES = 32
DEC_CHUNK = 4096
DEC_LHS_ROWS = MLA_HEADS * QK_NOPE + 16
DEC_SS_ROWS = 48


def _mla_decode_kernel(pt_ref, q_ref, k_ref, latn_ref, cos_ref, sin_ref, wukt_ref, gkn_ref, gcol_ref,
                       ssel_ref, wuv_ref, lat_hbm, krt_hbm, o_ref,
                       lat_buf, kr_buf, sem, m_sc, l_sc, acc_sc, lhs_sc, lb_sc, *, pp, nj, layer):
    t = pl.program_id(0)
    n_steps = pl.num_programs(0)
    j = t % nj
    slot = t % 2

    def page_copies(step, buf_slot):
        b = step // nj
        base = (step % nj) * pp
        copies = []
        for p in range(pp):
            page = pt_ref[b, base + p]
            copies.append(pltpu.make_async_copy(lat_hbm.at[page, layer], lat_buf.at[buf_slot, p],
                                                sem.at[buf_slot, 0]))
            copies.append(pltpu.make_async_copy(krt_hbm.at[page, layer],
                                                kr_buf.at[buf_slot, :, pl.ds(p * PAGE_SIZE, PAGE_SIZE)],
                                                sem.at[buf_slot, 1]))
        return copies

    @pl.when(t == 0)
    def _():
        for cp in page_copies(t, slot):
            cp.start()
        lhs_sc[0:MLA_HEADS * QK_NOPE, :] = wukt_ref[...]

    q16 = q_ref[...].astype(F32)

    @pl.when(j == 0)
    def _():
        m_sc[...] = jnp.full(m_sc.shape, -jnp.inf, F32)
        l_sc[...] = jnp.zeros(l_sc.shape, F32)
        acc_sc[...] = jnp.zeros(acc_sc.shape, F32)
        qn = (q16 * gkn_ref[...])[:, 0:QK_NOPE]
        tiled = jnp.concatenate([qn] * MLA_HEADS, axis=1)
        row = lax.broadcasted_iota(jnp.int32, tiled.shape, 0)
        lane = lax.broadcasted_iota(jnp.int32, tiled.shape, 1) // QK_NOPE
        qbd = jnp.where(row == lane, tiled, 0.0).astype(BF16)
        lhs_sc[MLA_HEADS * QK_NOPE:, :] = _dot(qbd, wukt_ref[...]).astype(BF16)

    for cp in page_copies(t, slot):
        cp.wait()

    qr = q16[:, QK_NOPE:QK_HEAD].astype(BF16)
    lhs = lhs_sc[...]
    gcol = gcol_ref[...]
    ssel = ssel_ref[...]
    half = QK_ROPE // 2
    nf = MLA_HEADS * QK_NOPE
    next_copies = page_copies(jnp.minimum(t + 1, n_steps - 1), 1 - slot)
    for p in range(pp):
        lb_sc[p * PAGE_SIZE:(p + 1) * PAGE_SIZE, :] = lat_buf[slot, p].astype(BF16)
        next_copies[2 * p].start()
        next_copies[2 * p + 1].start()

    ck = DEC_CHUNK
    scores = []
    for c0 in range(0, pp * PAGE_SIZE, ck):
        res = _dot_nt(lhs, lb_sc[c0:c0 + ck, :])
        sq = res[0:nf, :] * res[0:nf, :]
        sn = res[nf:nf + 8, :]
        krt = kr_buf[slot, :, c0:c0 + ck]
        parts = [sq[h * QK_NOPE:(h + 1) * QK_NOPE, :].reshape(8, 8, ck).sum(axis=0) for h in range(MLA_HEADS)]
        parts.append((krt * krt).reshape(4, 8, ck).sum(axis=0))
        parts.append(jnp.zeros((8, ck), F32))
        p_hi, p_lo = _split_bf16(jnp.concatenate(parts, axis=0))
        ss = _dot(ssel, p_hi) + _dot(ssel, p_lo)
        krg = krt * gcol
        cs = cos_ref[:, c0:c0 + ck]
        sn_ = sin_ref[:, c0:c0 + ck]
        x1, x2 = krg[0:half, :], krg[half:, :]
        roped = jnp.concatenate([x1 * cs - x2 * sn_, x2 * cs + x1 * sn_], axis=0)
        sr = _dot(qr, roped.astype(BF16))[0:8, :]
        scores.append((sn + sr) * lax.rsqrt(ss * (1.0 / QK_HEAD) + EPS))
    s_all = jnp.concatenate(scores, axis=1)

    m_old = m_sc[...]
    m_new = jnp.maximum(m_old, jnp.max(s_all, axis=1, keepdims=True))
    alpha = jnp.exp(m_old - m_new)
    p = jnp.exp(s_all - m_new[:, 0:1])
    l_sc[...] = alpha * l_sc[...] + jnp.sum(p, axis=1, keepdims=True)
    acc_sc[...] = alpha[:, 0:1] * acc_sc[...] + _dot(p.astype(BF16), lb_sc[...])
    m_sc[...] = m_new

    @pl.when(t == n_steps - 1)
    def _():
        for cp in page_copies(t, 1 - slot):
            cp.wait()

    @pl.when(j == nj - 1)
    def _():
        s_self = jnp.sum((q16 * k_ref[...].astype(F32))[0:8, :], axis=1, keepdims=True)
        m_o = m_sc[...]
        m_n = jnp.maximum(m_o, s_self)
        al = jnp.exp(m_o - m_n)
        ps = jnp.exp(s_self - m_n)
        l = al * l_sc[...] + ps
        acc = al[:, 0:1] * acc_sc[...] + ps[:, 0:1] * latn_ref[...]
        o_lat = acc / l[:, 0:1]
        res = _dot(o_lat.astype(BF16), wuv_ref[...])
        row = lax.broadcasted_iota(jnp.int32, res.shape, 0)
        lane = lax.broadcasted_iota(jnp.int32, res.shape, 1) // V_HEAD
        o_ref[...] = jnp.sum(jnp.where(row == lane, res, 0.0), axis=0, keepdims=True).astype(o_ref.dtype)


def _mla_decode(page_table, q16, k16, lat_new3, cache_lat, cache_krt, layer, lw, tabs):
    pp = DEC_PAGES
    n = q16.shape[0]
    n_pages = page_table.shape[1]
    nj = n_pages // pp
    cos_t, sin_t = tabs
    fix2 = lambda t, pt: (0, 0)
    seq3 = lambda t, pt: (t // nj, 0, 0)
    tab = lambda t, pt: (0, t % nj)
    weights = [lw["w_uk_t"], lw["g_k_nope"], lw["g_k_rope_col"], lw["ss_select"], lw["w_uv"]]
    in_specs = [pl.BlockSpec((None, 16, HEAD_PAD), seq3), pl.BlockSpec((None, 16, HEAD_PAD), seq3),
                pl.BlockSpec((None, 1, KV_RANK), seq3),
                pl.BlockSpec((QK_ROPE // 2, pp * PAGE_SIZE), tab),
                pl.BlockSpec((QK_ROPE // 2, pp * PAGE_SIZE), tab)]
    in_specs += [pl.BlockSpec(w.shape, fix2) for w in weights]
    in_specs += [pl.BlockSpec(memory_space=pl.ANY), pl.BlockSpec(memory_space=pl.ANY)]
    grid_spec = pltpu.PrefetchScalarGridSpec(
        num_scalar_prefetch=1,
        grid=(n * nj,),
        in_specs=in_specs,
        out_specs=pl.BlockSpec((None, 1, MLA_OUT), seq3),
        scratch_shapes=[pltpu.VMEM((2, pp, PAGE_SIZE, KV_RANK), F32),
                        pltpu.VMEM((2, QK_ROPE, pp * PAGE_SIZE), F32),
                        pltpu.SemaphoreType.DMA((2, 2)),
                        pltpu.VMEM((8, 128), F32), pltpu.VMEM((8, 128), F32), pltpu.VMEM((8, KV_RANK), F32),
                        pltpu.VMEM((DEC_LHS_ROWS, KV_RANK), BF16),
                        pltpu.VMEM((pp * PAGE_SIZE, KV_RANK), BF16)])
    return pl.pallas_call(
        functools.partial(_mla_decode_kernel, pp=pp, nj=nj, layer=layer),
        grid_spec=grid_spec,
        out_shape=jax.ShapeDtypeStruct((n, 1, MLA_OUT), F32),
        compiler_params=_cparams(("arbitrary",), 40),
        name="mla_decode",
    )(page_table, q16, k16, lat_new3, cos_t, sin_t, *weights, cache_lat, cache_krt)


def _dec_mix_kernel(q_ref, k_ref, la_ref, v_ref, gr_ref, u_ref, s_ref, pb_ref, gout_ref, wp_ref, sc_ref,
                    s_o, gla_o, pool_o, pb_o, *, past):
    a = jnp.exp(la_ref[...])
    v4 = v_ref[...]
    vexp = jnp.concatenate([jnp.broadcast_to(v4[h:h + 1, :], (GLA_DK, GLA_DV)) for h in range(GLA_HEADS)],
                           axis=0)
    s_new = a * s_ref[...] + k_ref[...] * vexp
    s_o[...] = s_new
    prod = (q_ref[...] * (GLA_DK ** -0.5)) * s_new
    gr = gr_ref[...]
    gout = gout_ref[...]
    for h in range(GLA_HEADS):
        o = jnp.sum(prod[h * GLA_DK:(h + 1) * GLA_DK, :], axis=0, keepdims=True)
        on = _rms(o, gout)
        g = gr[h:h + 1, :]
        gla_o[h:h + 1, :] = (on * (g * _sigmoid(g))).astype(gla_o.dtype)

    st = pb_ref[...]
    u = u_ref[...]
    ridx = lax.broadcasted_iota(jnp.int32, st.shape, 0)
    parts = []
    for w in POOL_WINDOWS:
        tot = u + jnp.sum(jnp.where(ridx >= POOL_BUF + 1 - w, st, 0.0), axis=0, keepdims=True)
        parts.append(tot / float(min(w, past + 1)) - u)
    pooled = jnp.broadcast_to(_pool_select(parts), (8, POOL_WIDTH)).astype(BF16)
    y = _dot(pooled, wp_ref[...])[0:1, :] * sc_ref[...]
    pool_o[...] = y.astype(pool_o.dtype)
    pb_o[0:POOL_BUF - 1, :] = st[1:POOL_BUF, :]
    pb_o[POOL_BUF - 1:POOL_BUF, :] = u


def _dec_mix(gq, gk, la, gv, gr, u, state_gla4, state_pool, layer, lw, past):
    n = gq.shape[0]
    col = lambda x: x.reshape(n, GLA_QK, 1)
    b3 = lambda b: (b, 0, 0)
    fix = lambda b: (0, 0)
    return pl.pallas_call(
        functools.partial(_dec_mix_kernel, past=past),
        grid=(n,),
        in_specs=[pl.BlockSpec((None, GLA_QK, 1), b3)] * 3
        + [pl.BlockSpec((None, GLA_HEADS, GLA_DV), b3)] * 2
        + [pl.BlockSpec((None, 1, POOL_WIDTH), b3),
           pl.BlockSpec((None, None, GLA_QK, GLA_DV), lambda b: (b, layer, 0, 0)),
           pl.BlockSpec((None, None, POOL_BUF, POOL_WIDTH), lambda b: (b, layer, 0, 0)),
           pl.BlockSpec((1, GLA_DV), fix), pl.BlockSpec((POOL_WIDTH, POOL_WIDTH), fix),
           pl.BlockSpec((1, POOL_WIDTH), fix)],
        out_specs=[pl.BlockSpec((None, GLA_QK, GLA_DV), b3), pl.BlockSpec((None, GLA_HEADS, GLA_DV), b3),
                   pl.BlockSpec((None, 1, POOL_WIDTH), b3), pl.BlockSpec((None, POOL_BUF, POOL_WIDTH), b3)],
        out_shape=[jax.ShapeDtypeStruct((n, GLA_QK, GLA_DV), F32),
                   jax.ShapeDtypeStruct((n, GLA_HEADS, GLA_DV), F32),
                   jax.ShapeDtypeStruct((n, 1, POOL_WIDTH), F32),
                   jax.ShapeDtypeStruct((n, POOL_BUF, POOL_WIDTH), F32)],
        compiler_params=_cparams(("parallel",), 32),
        name="dec_mix",
    )(col(gq), col(gk), col(la), gv.reshape(n, GLA_HEADS, GLA_DV), gr.reshape(n, GLA_HEADS, GLA_DV),
      u.reshape(n, 1, POOL_WIDTH), state_gla4, state_pool, lw["g_gla_out64"], lw["w_pool"], lw["pool_scale"])


def _post_a_kernel(x_ref, mla_ref, gla_ref, pool_ref, wo_ref, gxa_ref, wxq_ref, gxq_ref, x1_ref, q_ref):
    x1 = _mix_out(x_ref[...], mla_ref[...], gla_ref[...], pool_ref[...], wo_ref)
    x1_ref[...] = x1
    q_ref[...] = _xa_query(x1, gxa_ref[...], wxq_ref[...], gxq_ref[...])


def _post_a(x2, mla, gla, pool, lw):
    m = x2.shape[0]
    return pl.pallas_call(
        _post_a_kernel,
        out_shape=[jax.ShapeDtypeStruct((m, D_MODEL), F32), jax.ShapeDtypeStruct((m, D_MODEL), F32)],
        compiler_params=pltpu.CompilerParams(vmem_limit_bytes=32 * MIB),
        name="post_a",
    )(x2, mla, gla, pool, lw["w_out"], lw["norm_xa"], lw["w_xq"], lw["g_xq"])


XA_HALVES = XA_HEAD // 128
XA_ROWS = XA_HALVES * XA_HEADS


def _xa_decode_kernel(q_ref, mk_ref, mv_ref, o_ref):
    k3 = mk_ref[...].reshape(N_MEM, XA_ROWS, 128)
    v3 = mv_ref[...].reshape(N_MEM, XA_ROWS, 128)
    s = jnp.sum(k3 * q_ref[...][None], axis=-1, keepdims=True)
    s = s[:, 0:XA_HEADS] + s[:, XA_HEADS:XA_ROWS]
    s = jnp.concatenate([s, s], axis=1)
    e = jnp.exp(s - jnp.max(s, axis=0, keepdims=True))
    p = e / jnp.sum(e, axis=0, keepdims=True)
    o_ref[...] = jnp.sum(p * v3, axis=0)


def _xa_decode(q3, cache_k, cache_v, layer):
    n = q3.shape[0]
    depth = cache_k.shape[1]
    view = lambda c: c.reshape(n, depth, N_MEM, XA_HEADS, XA_HALVES, 128).transpose(0, 1, 2, 4, 3, 5).reshape(
        n, depth, N_MEM * XA_ROWS, 128)
    q8 = q3.reshape(n, XA_HEADS, XA_HALVES, 128).transpose(0, 2, 1, 3).reshape(n, XA_ROWS, 128)
    b3 = lambda b: (b, 0, 0)
    mem = lambda b: (b, layer, 0, 0)
    o = pl.pallas_call(
        _xa_decode_kernel,
        grid=(n,),
        in_specs=[pl.BlockSpec((None, XA_ROWS, 128), b3),
                  pl.BlockSpec((None, None, N_MEM * XA_ROWS, 128), mem),
                  pl.BlockSpec((None, None, N_MEM * XA_ROWS, 128), mem)],
        out_specs=pl.BlockSpec((None, XA_ROWS, 128), b3),
        out_shape=jax.ShapeDtypeStruct((n, XA_ROWS, 128), F32),
        compiler_params=_cparams(("parallel",), 32),
        name="xa_decode",
    )(q8, view(cache_k), view(cache_v))
    return o.reshape(n, XA_HALVES, XA_HEADS, 128).transpose(0, 2, 1, 3).reshape(n, 1, D_MODEL)


def _post_b_kernel(x1_ref, o_ref, wxo_ref, x2_ref):
    x2_ref[...] = x1_ref[...] + _dot(o_ref[...].astype(BF16), wxo_ref[...])


def _post_b(x1, o, lw):
    return pl.pallas_call(
        _post_b_kernel,
        out_shape=jax.ShapeDtypeStruct(x1.shape, F32),
        compiler_params=pltpu.CompilerParams(vmem_limit_bytes=32 * MIB),
        name="post_b",
    )(x1, o, lw["w_xo"])


def _rope_tables_rows(pos):
    half = QK_ROPE // 2
    inv = ROPE_THETA ** (-jnp.arange(half, dtype=F32) / half)
    ang = pos.astype(F32)[:, None] * inv[None, :]
    cos, sin = jnp.cos(ang), jnp.sin(ang)
    n = pos.shape[0]
    one = jnp.ones((n, QK_NOPE), F32)
    zero = jnp.zeros((n, QK_NOPE), F32)
    pad1 = jnp.ones((n, HEAD_PAD - QK_HEAD), F32)
    pad0 = jnp.zeros((n, HEAD_PAD - QK_HEAD), F32)
    zh = jnp.zeros((n, half), F32)
    c = jnp.concatenate([one, cos, cos, pad1], axis=1)
    s1 = jnp.concatenate([zero, -sin, zh, pad0], axis=1)
    s2 = jnp.concatenate([zero, zh, sin, pad0], axis=1)
    return c, s1, s2


def _rope_tables_cols(pos):
    half = QK_ROPE // 2
    inv = ROPE_THETA ** (-jnp.arange(half, dtype=F32) / half)
    ang = pos.astype(F32)[:, None] * inv[None, :]
    return jnp.cos(ang).T, jnp.sin(ang).T


def _pad_heads(w, width):
    pad = [(0, 0)] * (w.ndim - 1) + [(0, HEAD_PAD - width)]
    w = jnp.pad(w, pad)
    return w.reshape(w.shape[:-2] + (w.shape[-2] * HEAD_PAD,))


def _layer_weights(l, p):
    w_in = p["w_in"][l]
    sizes = (Q_RANK, KV_RANK, QK_ROPE, GLA_QK, GLA_QK, GLA_OUT, GLA_GATE_RANK, GLA_OUT, POOL_WIDTH)
    offs = [0]
    for s in sizes:
        offs.append(offs[-1] + s)
    piece = lambda i: w_in[:, offs[i]:offs[i + 1]]
    w_in_p = jnp.concatenate([piece(0), piece(1), piece(3), piece(4), piece(5), piece(7), piece(8), piece(2),
                              piece(6), jnp.zeros((D_MODEL, IN_PAD - sum(sizes)), F32)], axis=1)
    g_q = _pad_heads(jnp.broadcast_to(p["g_qk_q"][l], (MLA_HEADS, QK_HEAD)), QK_HEAD)
    g_k = _pad_heads(jnp.broadcast_to(p["g_qk_k"][l], (MLA_HEADS, QK_HEAD)), QK_HEAD)
    place = jnp.zeros((128, QK_W), F32)
    j = jnp.arange(QK_ROPE)
    for h in range(MLA_HEADS):
        place = place.at[j, h * HEAD_PAD + QK_NOPE + j].set(1.0)
    w_gate = jnp.zeros((128, GLA_QK), F32).at[QK_ROPE:QK_ROPE + GLA_GATE_RANK].set(p["w_gate_up"][l])
    wp = p["w_pool"][l]
    w_pool = jnp.zeros((POOL_WIDTH, POOL_WIDTH), F32)
    for g in range(len(POOL_WINDOWS)):
        w_pool = w_pool.at[g * POOL_GROUP:(g + 1) * POOL_GROUP, g * POOL_GROUP:(g + 1) * POOL_GROUP].set(wp[g])
    blk = jnp.arange(GLA_OUT) // GLA_DV
    row = lambda v: v.reshape(1, -1).astype(F32)
    bf = lambda v: v.astype(BF16)
    w_uk2 = p["w_uk"][l].reshape(KV_RANK, MLA_HEADS * QK_NOPE)
    g_k_nope = jnp.zeros((1, HEAD_PAD), F32).at[0, :QK_NOPE].set(p["g_qk_k"][l][:QK_NOPE])
    col = jnp.arange(DEC_SS_ROWS)[None, :] // 8
    hrow = jnp.arange(8)[:, None]
    ss_select = ((hrow < MLA_HEADS) & ((col == hrow) | (col == MLA_HEADS))).astype(F32)
    return {
        "norm_mix": row(p["norm_mix"][l]), "w_in": bf(w_in_p), "g_q_lat": row(p["g_q_lat"][l]),
        "w_uq": bf(_pad_heads(p["w_uq"][l], QK_HEAD)), "g_kv_lat": row(p["g_kv_lat"][l]),
        "w_uk": bf(_pad_heads(p["w_uk"][l], QK_NOPE)), "w_uv": bf(p["w_uv"][l].reshape(KV_RANK, MLA_OUT)),        "g_qk_q": row(g_q), "g_qk_k": row(g_k), "place": bf(place), "w_gate": bf(w_gate),
        "b_gate": row(p["b_gate"][l]),
        "e64": bf((blk[:, None] == blk[None, :]).astype(F32)),
        "g_gla_out": row(jnp.tile(p["g_gla_out"][l], GLA_HEADS)), "g_gla_out64": row(p["g_gla_out"][l]),
        "w_pool": bf(w_pool), "pool_scale": row(p["pool_scale"][l]),
        "w_out": bf(p["w_out"][l]), "norm_xa": row(p["norm_xa"][l]), "w_xq": bf(p["w_xq"][l]),
        "g_xq": row(jnp.tile(p["g_xq"][l], XA_HEADS)), "w_xo": bf(p["w_xo"][l]),
        "norm_mlp": row(p["norm_mlp"][l]), "w_ff1": bf(p["w_ff1"][l]), "w_ff2": bf(p["w_ff2"][l]),
        "norm_mem": row(p["norm_mem"][l]), "w_xk": bf(p["w_xk"][l]), "w_xv": bf(p["w_xv"][l]),
        "g_xk": row(jnp.tile(p["g_xk"][l], XA_HEADS)),
        "w_uk_t": bf(w_uk2.T), "g_k_nope": g_k_nope,
        "g_k_rope_col": jnp.broadcast_to(p["g_qk_k"][l][QK_NOPE:, None], (QK_ROPE, DEC_CHUNK)).astype(F32),
        "ss_select": bf(ss_select),
    }


def _prompt_layer(x2, mem2, lw, ropes, b, t):
    mk, mv, mkb, mvb = _memory_kv(mem2, lw["norm_mem"], lw["w_xk"], lw["w_xv"], lw["g_xk"])
    lat, kr, q, k, v, gq, gk, gv, la, gr, u = _in_proj(x2, lw, ropes, 512)
    r3 = lambda a: a.reshape(b, t, a.shape[-1])
    mla = _mla_prompt(r3(q), r3(k), r3(v), 512).reshape(b * t, MLA_OUT)
    gla, st = _gla_prompt(r3(gq), r3(gk), r3(gv), r3(la), r3(gr), lw)
    pool = _pool_prompt(r3(u), lw, 512).reshape(b * t, POOL_WIDTH)
    x2 = _post_prompt(x2, mla, gla.reshape(b * t, GLA_OUT), pool, mkb.reshape(b, N_MEM, D_MODEL),
                      mvb.reshape(b, N_MEM, D_MODEL), lw, t, 512)
    x2 = _mlp(x2, lw, 512)
    gla_state = jnp.stack([st[:, h * GLA_DV:(h + 1) * GLA_DV, h * GLA_DK:(h + 1) * GLA_DK]
                           for h in range(GLA_HEADS)], axis=1).transpose(0, 1, 3, 2)
    outs = (lat.reshape(b, t, KV_RANK), kr.reshape(b, t, QK_ROPE), gla_state,
            r3(u)[:, t - POOL_BUF:], mk.reshape(b, N_MEM, XA_HEADS, XA_HEAD),
            mv.reshape(b, N_MEM, XA_HEADS, XA_HEAD))
    return x2, outs


def _sample_layer(x2, layer, lw, ropes, tabs, page_table, cache_lat, cache_krt, state_gla4, state_pool,
                  cache_mk, cache_mv, past):
    n = x2.shape[0]
    lat, kr, q, k, v, gq, gk, gv, la, gr, u = _in_proj(x2, lw, ropes, n)
    pad16 = lambda a: jnp.pad(a.reshape(n, MLA_HEADS, HEAD_PAD), ((0, 0), (0, 16 - MLA_HEADS), (0, 0)))
    mla = _mla_decode(page_table, pad16(q), pad16(k), lat.reshape(n, 1, KV_RANK), cache_lat, cache_krt, layer,
                      lw, tabs).reshape(n, MLA_OUT)
    s_new, gla, pool, pb = _dec_mix(gq, gk, la, gv, gr, u, state_gla4, state_pool, layer, lw, past)
    x1, qx = _post_a(x2, mla, gla.reshape(n, GLA_OUT), pool.reshape(n, POOL_WIDTH), lw)
    ox = _xa_decode(qx.reshape(n, 1, D_MODEL), cache_mk, cache_mv, layer).reshape(n, D_MODEL)
    x2 = _mlp(_post_b(x1, ox, lw), lw, n)
    outs = (lat.reshape(n, 1, KV_RANK), kr.reshape(n, 1, QK_ROPE),
            s_new.reshape(n, GLA_HEADS, GLA_DK, GLA_DV), pb)
    return x2, outs


def kernel(x_prompt, x_sample, mem_prompt, cache_mla_latent, cache_mla_krope, state_gla, state_pool,
           cache_mem_k, cache_mem_v, page_table, norm_mix, w_in, g_q_lat, w_uq, g_kv_lat, w_uk, w_uv,
           g_qk_q, g_qk_k, w_gate_up, b_gate, g_gla_out, w_pool, pool_scale, w_out, norm_xa, norm_mem,
           w_xq, w_xk, w_xv, g_xq, g_xk, w_xo, norm_mlp, w_ff1, w_ff2):
    params = dict(norm_mix=norm_mix, w_in=w_in, g_q_lat=g_q_lat, w_uq=w_uq, g_kv_lat=g_kv_lat, w_uk=w_uk,
                  w_uv=w_uv, g_qk_q=g_qk_q, g_qk_k=g_qk_k, w_gate_up=w_gate_up, b_gate=b_gate,
                  g_gla_out=g_gla_out, w_pool=w_pool, pool_scale=pool_scale, w_out=w_out, norm_xa=norm_xa,
                  norm_mem=norm_mem, w_xq=w_xq, w_xk=w_xk, w_xv=w_xv, g_xq=g_xq, g_xk=g_xk, w_xo=w_xo,
                  norm_mlp=norm_mlp, w_ff1=w_ff1, w_ff2=w_ff2)
    b, t, _ = x_prompt.shape
    n, dec_seq, _ = x_sample.shape
    depth = norm_mix.shape[0]
    n_pages = page_table.shape[1]
    past = n_pages * PAGE_SIZE

    ropes_p = _rope_tables_rows(jnp.arange(t, dtype=jnp.int32))
    ropes_s = _rope_tables_rows(jnp.full((n,), past, jnp.int32))
    tabs = _rope_tables_cols(jnp.arange(past, dtype=jnp.int32))

    yp = x_prompt.reshape(b * t, D_MODEL)
    ys = x_sample.reshape(n * dec_seq, D_MODEL)
    mem2 = mem_prompt.reshape(b * N_MEM, D_MODEL)
    state_gla4 = state_gla.reshape(n, depth, GLA_QK, GLA_DV)
    cache_krt = cache_mla_krope.transpose(0, 1, 3, 2)

    p_outs, s_outs = [], []
    for l in range(depth):
        lw = _layer_weights(l, params)
        yp, po = _prompt_layer(yp, mem2, lw, ropes_p, b, t)
        ys, so = _sample_layer(ys, l, lw, ropes_s, tabs, page_table, cache_mla_latent, cache_krt,
                               state_gla4, state_pool, cache_mem_k, cache_mem_v, past)
        p_outs.append(po)
        s_outs.append(so)

    stack = lambda outs, i: jnp.stack([o[i] for o in outs], axis=1)
    return (yp.reshape(b, t, D_MODEL), ys.reshape(n, dec_seq, D_MODEL),
            stack(p_outs, 0), stack(p_outs, 1), stack(p_outs, 2), stack(p_outs, 3), stack(p_outs, 4),
            stack(p_outs, 5), stack(s_outs, 0), stack(s_outs, 1), stack(s_outs, 2), stack(s_outs, 3))
```

```python
import functools

import jax
import jax.numpy as jnp
from jax import lax
from jax.experimental import pallas as pl
from jax.experimental.pallas import tpu as pltpu

F32 = jnp.float32
BF16 = jnp.bfloat16

D_MODEL = 1024
MLA_HEADS = 4
QK_NOPE = 64
QK_ROPE = 32
QK_HEAD = QK_NOPE + QK_ROPE
V_HEAD = 128
Q_RANK = 384
KV_RANK = 256
ROPE_THETA = 10000.0
GLA_HEADS = 4
GLA_DK = 32
GLA_DV = 64
GLA_GATE_RANK = 16
GLA_TAU = 16.0
GLA_CHUNK = 64
GLA_GROUP = 256
POOL_WINDOWS = (2, 4, 8, 16)
POOL_GROUP = 64
POOL_WIDTH = POOL_GROUP * len(POOL_WINDOWS)
POOL_BUF = max(POOL_WINDOWS) - 1
MLA_OUT = MLA_HEADS * V_HEAD
GLA_OUT = GLA_HEADS * GLA_DV
N_MEM = 256
XA_HEADS = 4
XA_HEAD = D_MODEL // XA_HEADS
D_FF = 4 * D_MODEL
PAGE_SIZE = 128
EPS = 1e-6

HEAD_PAD = 128
QK_W = MLA_HEADS * HEAD_PAD
GLA_QK = GLA_HEADS * GLA_DK
IN_PAD = 1792
MISC0 = 1664

MIB = 1024 * 1024


def _cparams(sem, vmem_mib):
    return pltpu.CompilerParams(dimension_semantics=sem, vmem_limit_bytes=vmem_mib * MIB)


def _rms(x, g):
    y = x * lax.rsqrt(jnp.mean(x * x, axis=-1, keepdims=True) + EPS)
    return y * g


def _head_rms(x, g, nh, width, count):
    outs = []
    for h in range(nh):
        p = x[:, h * width:(h + 1) * width]
        ms = jnp.sum(p * p, axis=-1, keepdims=True) * (1.0 / count)
        outs.append(p * lax.rsqrt(ms + EPS) * g[:, h * width:(h + 1) * width])
    return jnp.concatenate(outs, axis=1)


def _dot(a, b):
    return jnp.dot(a, b, preferred_element_type=F32)


def _dot_nt(a, b):
    return lax.dot_general(a, b, (((1,), (1,)), ((), ())), preferred_element_type=F32)


def _split_bf16(x):
    hi = x.astype(BF16)
    lo = (x - hi.astype(F32)).astype(BF16)
    return hi, lo


def _dot_split(x, w):
    hi, lo = _split_bf16(x)
    return _dot(hi, w) + _dot(lo, w)


def _rope_rows(x, c, s1, s2):
    n = x.shape[1]
    c4 = jnp.concatenate([c] * MLA_HEADS, axis=1)
    s14 = jnp.concatenate([s1] * MLA_HEADS, axis=1)
    s24 = jnp.concatenate([s2] * MLA_HEADS, axis=1)
    half = QK_ROPE // 2
    return x * c4 + pltpu.roll(x, n - half, 1) * s14 + pltpu.roll(x, half, 1) * s24


def _log_sigmoid(x):
    return -(jnp.maximum(-x, 0.0) + jnp.log1p(jnp.exp(-jnp.abs(x))))


def _sigmoid(x):
    return 1.0 / (1.0 + jnp.exp(-x))


def _memkv_kernel(mem_ref, g_ref, wk_ref, wv_ref, gk_ref, k_ref, v_ref, kb_ref, vb_ref):
    m = _rms(mem_ref[...], g_ref[...]).astype(BF16)
    k = _head_rms(_dot(m, wk_ref[...]), gk_ref[...], XA_HEADS, XA_HEAD, XA_HEAD)
    v = _dot(m, wv_ref[...])
    k_ref[...] = k.reshape(k_ref.shape)
    v_ref[...] = v.reshape(v_ref.shape)
    kb_ref[...] = k.astype(BF16)
    vb_ref[...] = v.astype(BF16)


def _memory_kv(mem2, g, wk, wv, gk):
    m = mem2.shape[0]
    depth = wk.shape[0]
    tm = 512
    nb = tm // N_MEM
    vec = lambda l, i: (l, 0, 0)
    return pl.pallas_call(
        _memkv_kernel,
        grid=(depth, m // tm),
        in_specs=[pl.BlockSpec((tm, D_MODEL), lambda l, i: (i, 0)), pl.BlockSpec((None, 1, D_MODEL), vec),
                  pl.BlockSpec((None, D_MODEL, D_MODEL), vec), pl.BlockSpec((None, D_MODEL, D_MODEL), vec),
                  pl.BlockSpec((None, 1, D_MODEL), vec)],
        out_specs=[pl.BlockSpec((nb, None, N_MEM, D_MODEL), lambda l, i: (i, l, 0, 0))] * 2
        + [pl.BlockSpec((None, tm, D_MODEL), lambda l, i: (l, i, 0))] * 2,
        out_shape=[jax.ShapeDtypeStruct((m // N_MEM, depth, N_MEM, D_MODEL), F32)] * 2
        + [jax.ShapeDtypeStruct((depth, m, D_MODEL), BF16)] * 2,
        compiler_params=_cparams(("parallel", "parallel"), 40),
        name="memory_kv",
    )(mem2, g, wk, wv, gk)


def _inproj_kernel(x_ref, gmix_ref, win_ref, gql_ref, wuq_ref, gkv_ref, wuk_ref, wuv_ref,
                   gq_ref, gk_ref, place_ref, wg_ref, bg_ref, c_ref, s1_ref, s2_ref,
                   lat_ref, kr_ref, q_ref, k_ref, v_ref, gq_o, gk_o, gv_o, la_o, gr_o, u_o):
    h = _rms(x_ref[...], gmix_ref[...]).astype(BF16)
    z = _dot(h, win_ref[...])
    c_q = z[:, 0:384]
    c_kv = z[:, 384:640]
    misc = z[:, MISC0:MISC0 + 128]
    gq_o[...] = z[:, 640:768]
    gk_o[...] = z[:, 768:896]
    gv_o[...] = z[:, 896:1152]
    gr_o[...] = z[:, 1152:1408]
    u_o[...] = z[:, 1408:1664]
    kr_ref[...] = misc[:, 0:QK_ROPE]

    lat = _rms(c_kv, gkv_ref[...])
    lat_ref[...] = lat
    latb = lat.astype(BF16)
    c, s1, s2 = c_ref[...], s1_ref[...], s2_ref[...]

    q = _dot(_rms(c_q, gql_ref[...]).astype(BF16), wuq_ref[...])
    q = _head_rms(q, gq_ref[...], MLA_HEADS, HEAD_PAD, QK_HEAD)
    q_ref[...] = (_rope_rows(q, c, s1, s2) * (QK_HEAD ** -0.5)).astype(BF16)

    k = _dot(latb, wuk_ref[...]) + _dot_split(misc, place_ref[...])
    k = _head_rms(k, gk_ref[...], MLA_HEADS, HEAD_PAD, QK_HEAD)
    k_ref[...] = _rope_rows(k, c, s1, s2).astype(BF16)
    v_ref[...] = _dot(latb, wuv_ref[...]).astype(BF16)

    gate = _dot(misc.astype(BF16), wg_ref[...]) + bg_ref[...]
    la_o[...] = _log_sigmoid(gate) * (1.0 / GLA_TAU)


def _in_proj(x2, lw, ropes, tm):
    m = x2.shape[0]
    row = lambda i: (i, 0)
    fix = lambda i: (0, 0)
    c, s1, s2 = ropes
    if c.shape[0] == m:
        rope_map = row
    else:
        nb = c.shape[0] // tm
        rope_map = lambda i: (i % nb, 0)
    weights = [lw["norm_mix"], lw["w_in"], lw["g_q_lat"], lw["w_uq"], lw["g_kv_lat"], lw["w_uk"],
               lw["w_uv"], lw["g_qk_q"], lw["g_qk_k"], lw["place"], lw["w_gate"], lw["b_gate"]]
    in_specs = [pl.BlockSpec((tm, D_MODEL), row)]
    in_specs += [pl.BlockSpec(w.shape, fix) for w in weights]
    in_specs += [pl.BlockSpec((tm, HEAD_PAD), rope_map)] * 3
    widths = [(KV_RANK, F32), (QK_ROPE, F32), (QK_W, BF16), (QK_W, BF16), (MLA_OUT, BF16),
              (GLA_QK, F32), (GLA_QK, F32), (GLA_OUT, F32), (GLA_QK, F32), (GLA_OUT, F32),
              (POOL_WIDTH, F32)]
    return pl.pallas_call(
        _inproj_kernel,
        grid=(m // tm,),
        in_specs=in_specs,
        out_specs=[pl.BlockSpec((tm, w), row) for w, _ in widths],
        out_shape=[jax.ShapeDtypeStruct((m, w), dt) for w, dt in widths],
        compiler_params=_cparams(("parallel",), 48),
        name="in_proj",
    )(x2, *weights, c, s1, s2)


def _flash_kernel(q_ref, k_ref, v_ref, o_ref, *, tq):
    qi = pl.program_id(2)
    q = q_ref[...]

    def block(j, carry, masked):
        m, l, acc = carry
        start = pl.multiple_of(j * tq, tq)
        k = k_ref[pl.ds(start, tq), :]
        v = v_ref[pl.ds(start, tq), :]
        s = _dot_nt(q, k)
        if masked:
            rows = lax.broadcasted_iota(jnp.int32, (tq, tq), 0)
            cols = lax.broadcasted_iota(jnp.int32, (tq, tq), 1)
            s = jnp.where(cols <= rows, s, -jnp.inf)
        m_new = jnp.maximum(m, jnp.max(s, axis=-1, keepdims=True))
        alpha = jnp.exp(m - m_new)
        p = jnp.exp(s - m_new)
        l = alpha * l + jnp.sum(p, axis=-1, keepdims=True)
        acc = alpha * acc + _dot(p.astype(BF16), v)
        return m_new, l, acc

    init = (jnp.full((tq, 1), -jnp.inf, F32), jnp.zeros((tq, 1), F32), jnp.zeros((tq, V_HEAD), F32))
    carry = lax.fori_loop(0, qi, lambda j, cr: block(j, cr, False), init)
    _, l, acc = block(qi, carry, True)
    o_ref[...] = (acc / l).astype(o_ref.dtype)


def _mla_prompt(q3, k3, v3, tq):
    b, t, _ = q3.shape
    qmap = lambda bi, h, qi: (bi, qi, h)
    kmap = lambda bi, h, qi: (bi, 0, h)
    return pl.pallas_call(
        functools.partial(_flash_kernel, tq=tq),
        grid=(b, MLA_HEADS, t // tq),
        in_specs=[pl.BlockSpec((None, tq, HEAD_PAD), qmap), pl.BlockSpec((None, t, HEAD_PAD), kmap),
                  pl.BlockSpec((None, t, V_HEAD), kmap)],
        out_specs=pl.BlockSpec((None, tq, V_HEAD), qmap),
        out_shape=jax.ShapeDtypeStruct((b, t, MLA_OUT), BF16),
        compiler_params=_cparams(("parallel", "parallel", "arbitrary"), 40),
        name="mla_prompt",
    )(q3, k3, v3)


def _gla_kernel(q_ref, k_ref, v_ref, la_ref, gr_ref, tri_ref, ones_ref, e64_ref, gout_ref, o_ref, st_ref, *, t):
    c = GLA_CHUNK
    gt = GLA_GROUP
    tri = tri_ref[...]
    ones_bd = ones_ref[...]
    e64 = e64_ref[...]
    gout = gout_ref[...]
    lane_qk = lax.broadcasted_iota(jnp.int32, (gt, GLA_QK), 1) // GLA_DK
    lane_v = lax.broadcasted_iota(jnp.int32, (gt, GLA_OUT), 1) // GLA_DV
    row_chunk = lax.broadcasted_iota(jnp.int32, (gt, GLA_QK), 0) // c
    rr = lax.broadcasted_iota(jnp.int32, (gt, gt), 0)
    cc = lax.broadcasted_iota(jnp.int32, (gt, gt), 1)
    causal = (cc <= rr) & (rr // c == cc // c)
    bd = (lax.broadcasted_iota(jnp.int32, (GLA_OUT, GLA_QK), 0) // GLA_DV
          == lax.broadcasted_iota(jnp.int32, (GLA_OUT, GLA_QK), 1) // GLA_DK)

    def group(gi, st):
        r0 = pl.multiple_of(gi * gt, gt)
        q = q_ref[pl.ds(r0, gt), :] * (GLA_DK ** -0.5)
        k = k_ref[pl.ds(r0, gt), :]
        v = v_ref[pl.ds(r0, gt), :]
        a_hi, a_lo = _split_bf16(la_ref[pl.ds(r0, gt), :])
        b = _dot(tri, a_hi) + _dot(tri, a_lo)
        tot = _dot(ones_bd, a_hi) + _dot(ones_bd, a_lo)
        qd = q * jnp.exp(b)
        qdb = qd.astype(BF16)
        kd = (k * jnp.exp(-b)).astype(BF16)
        k2 = k * jnp.exp(tot - b)
        decay = jnp.exp(tot)
        vb = v.astype(BF16)
        vt = v.T.astype(BF16)
        o = jnp.zeros((gt, GLA_OUT), F32)
        for h in range(GLA_HEADS):
            qh = jnp.where(lane_qk == h, qd, 0.0).astype(BF16)
            att = jnp.where(causal, _dot_nt(qh, kd), 0.0)
            o = o + jnp.where(lane_v == h, _dot(att.astype(BF16), vb), 0.0)
        inter = []
        for ci in range(gt // c):
            inter.append(_dot_nt(qdb[ci * c:(ci + 1) * c, :], st.astype(BF16)))
            k2c = jnp.where(row_chunk == ci, k2, 0.0).astype(BF16)
            st = st * decay[ci * c:ci * c + 1, :] + jnp.where(bd, _dot(vt, k2c), 0.0)
        o = o + jnp.concatenate(inter, axis=0)
        ms = _dot_split(o * o, e64) * (1.0 / GLA_DV)
        on = o * lax.rsqrt(ms + EPS) * gout
        gr = gr_ref[pl.ds(r0, gt), :]
        o_ref[pl.ds(r0, gt), :] = (on * (gr * _sigmoid(gr))).astype(o_ref.dtype)
        return st

    st_ref[...] = lax.fori_loop(0, t // gt, group, jnp.zeros((GLA_OUT, GLA_QK), F32), unroll=2)


def _gla_prompt(gq3, gk3, gv3, la3, gr3, lw):
    b, t, _ = gq3.shape
    bmap = lambda i: (i, 0, 0)
    fix = lambda i: (0, 0)
    chunk_id = jnp.arange(GLA_GROUP) // GLA_CHUNK
    same = chunk_id[:, None] == chunk_id[None, :]
    tri = (same & (jnp.arange(GLA_GROUP)[None, :] <= jnp.arange(GLA_GROUP)[:, None])).astype(BF16)
    return pl.pallas_call(
        functools.partial(_gla_kernel, t=t),
        grid=(b,),
        in_specs=[pl.BlockSpec((None, t, GLA_QK), bmap), pl.BlockSpec((None, t, GLA_QK), bmap),
                  pl.BlockSpec((None, t, GLA_OUT), bmap), pl.BlockSpec((None, t, GLA_QK), bmap),
                  pl.BlockSpec((None, t, GLA_OUT), bmap), pl.BlockSpec((GLA_GROUP, GLA_GROUP), fix),
                  pl.BlockSpec((GLA_GROUP, GLA_GROUP), fix),
                  pl.BlockSpec((GLA_OUT, GLA_OUT), fix), pl.BlockSpec((1, GLA_OUT), fix)],
        out_specs=[pl.BlockSpec((None, t, GLA_OUT), bmap), pl.BlockSpec((None, GLA_OUT, GLA_QK), bmap)],
        out_shape=[jax.ShapeDtypeStruct((b, t, GLA_OUT), BF16),
                   jax.ShapeDtypeStruct((b, GLA_OUT, GLA_QK), F32)],
        compiler_params=_cparams(("parallel",), 40),
        name="gla_prompt",
    )(gq3, gk3, gv3, la3, gr3, tri, same.astype(BF16), lw["e64"], lw["g_gla_out"])


def _pool_select(parts):
    lane = lax.broadcasted_iota(jnp.int32, parts[0].shape, 1) // POOL_GROUP
    out = parts[-1]
    for g in range(len(parts) - 2, -1, -1):
        out = jnp.where(lane == g, parts[g], out)
    return out


def _pool_kernel(u_ref, halo_ref, w_ref, sc_ref, o_ref, *, tp):
    i = pl.program_id(1)
    u = u_ref[...]
    halo = jnp.where(i == 0, 0.0, halo_ref[...])
    ext = jnp.concatenate([halo, u], axis=0)
    hb = halo.shape[0]
    pos = i * tp + lax.broadcasted_iota(jnp.int32, (tp, 1), 0)
    sums = ext
    parts = []
    span = 1
    for w in POOL_WINDOWS:
        while span < w:
            sums = sums + pltpu.roll(sums, span, 0)
            span *= 2
        cnt = jnp.minimum(w, pos + 1).astype(F32)
        parts.append(sums[hb:, :] / cnt - u)
    pooled = _pool_select(parts)
    o_ref[...] = (_dot(pooled.astype(BF16), w_ref[...]) * sc_ref[...]).astype(o_ref.dtype)


def _pool_prompt(u3, lw, tp):
    b, t, _ = u3.shape
    hb = 16
    return pl.pallas_call(
        functools.partial(_pool_kernel, tp=tp),
        grid=(b, t // tp),
        in_specs=[pl.BlockSpec((None, tp, POOL_WIDTH), lambda bi, i: (bi, i, 0)),
                  pl.BlockSpec((None, hb, POOL_WIDTH),
                               lambda bi, i: (bi, jnp.maximum(i * (tp // hb) - 1, 0), 0)),
                  pl.BlockSpec((POOL_WIDTH, POOL_WIDTH), lambda bi, i: (0, 0)),
                  pl.BlockSpec((1, POOL_WIDTH), lambda bi, i: (0, 0))],
        out_specs=pl.BlockSpec((None, tp, POOL_WIDTH), lambda bi, i: (bi, i, 0)),
        out_shape=jax.ShapeDtypeStruct((b, t, POOL_WIDTH), BF16),
        compiler_params=_cparams(("parallel", "parallel"), 32),
        name="pool_prompt",
    )(u3, u3, lw["w_pool"], lw["pool_scale"])


def _mix_out(x, mla, gla, pool, wo_ref):
    return (x + _dot(mla.astype(BF16), wo_ref[0:MLA_OUT, :])
            + _dot(gla.astype(BF16), wo_ref[MLA_OUT:MLA_OUT + GLA_OUT, :])
            + _dot(pool.astype(BF16), wo_ref[MLA_OUT + GLA_OUT:, :]))


def _xa_query(x1, gxa, wxq, gxq):
    q = _dot(_rms(x1, gxa).astype(BF16), wxq)
    return _head_rms(q, gxq, XA_HEADS, XA_HEAD, XA_HEAD) * (XA_HEAD ** -0.5)


def _xa_attend(qb, kb, vb):
    outs = []
    for h in range(XA_HEADS):
        sl = slice(h * XA_HEAD, (h + 1) * XA_HEAD)
        s = _dot_nt(qb[:, sl], kb[:, sl])
        e = jnp.exp(s - jnp.max(s, axis=-1, keepdims=True))
        p = e / jnp.sum(e, axis=-1, keepdims=True)
        outs.append(_dot(p.astype(BF16), vb[:, sl]))
    return jnp.concatenate(outs, axis=1).astype(BF16)


def _post_kernel(x_ref, mla_ref, gla_ref, pool_ref, wo_ref, gxa_ref, wxq_ref, gxq_ref, mk_ref, mv_ref,
                 wxo_ref, o_ref):
    x1 = _mix_out(x_ref[...], mla_ref[...], gla_ref[...], pool_ref[...], wo_ref)
    qb = _xa_query(x1, gxa_ref[...], wxq_ref[...], gxq_ref[...]).astype(BF16)
    o = _xa_attend(qb, mk_ref[...], mv_ref[...])
    o_ref[...] = x1 + _dot(o, wxo_ref[...])


def _post_prompt(x2, mla, gla, pool, mk4, mv4, layer, lw, t, tm):
    m = x2.shape[0]
    per_seq = t // tm
    row = lambda i: (i, 0)
    fix = lambda i: (0, 0)
    mem = lambda i: (layer, i // per_seq, 0, 0)
    sq = (D_MODEL, D_MODEL)
    return pl.pallas_call(
        _post_kernel,
        grid=(m // tm,),
        in_specs=[pl.BlockSpec((tm, D_MODEL), row), pl.BlockSpec((tm, MLA_OUT), row),
                  pl.BlockSpec((tm, GLA_OUT), row), pl.BlockSpec((tm, POOL_WIDTH), row),
                  pl.BlockSpec(sq, fix), pl.BlockSpec((1, D_MODEL), fix), pl.BlockSpec(sq, fix),
                  pl.BlockSpec((1, D_MODEL), fix), pl.BlockSpec((None, None, N_MEM, D_MODEL), mem),
                  pl.BlockSpec((None, None, N_MEM, D_MODEL), mem), pl.BlockSpec(sq, fix)],
        out_specs=pl.BlockSpec((tm, D_MODEL), row),
        out_shape=jax.ShapeDtypeStruct((m, D_MODEL), F32),
        compiler_params=_cparams(("parallel",), 48),
        name="post_prompt",
    )(x2, mla, gla, pool, lw["w_out"], lw["norm_xa"], lw["w_xq"], lw["g_xq"], mk4, mv4, lw["w_xo"])


def _mlp_kernel(x_ref, g_ref, w1_ref, w2_ref, o_ref):
    x = x_ref[...]
    h = _rms(x, g_ref[...]).astype(BF16)
    a = jnp.maximum(_dot(h, w1_ref[...]), 0.0)
    o_ref[...] = x + _dot((a * a).astype(BF16), w2_ref[...])


def _mlp(x2, lw, tm):
    m = x2.shape[0]
    once = pl.Buffered(1)
    return pl.pallas_call(
        _mlp_kernel,
        grid=(m // tm,),
        in_specs=[pl.BlockSpec((tm, D_MODEL), lambda i: (i, 0)),
                  pl.BlockSpec((1, D_MODEL), lambda i: (0, 0)),
                  pl.BlockSpec((D_MODEL, D_FF), lambda i: (0, 0), pipeline_mode=once),
                  pl.BlockSpec((D_FF, D_MODEL), lambda i: (0, 0), pipeline_mode=once)],
        out_specs=pl.BlockSpec((tm, D_MODEL), lambda i: (i, 0)),
        out_shape=jax.ShapeDtypeStruct((m, D_MODEL), F32),
        compiler_params=_cparams(("parallel",), 56),
        name="mlp",
    )(x2, lw["norm_mlp"], lw["w_ff1"], lw["w_ff2"])


DEC_PAGES = 32
DEC_CHUNK = 4096
DEC_LHS_ROWS = MLA_HEADS * QK_NOPE + 16
DEC_SS_ROWS = 48


def _mla_decode_kernel(pt_ref, q_ref, k_ref, latn_ref, cos_ref, sin_ref, wukt_ref, gkn_ref, gcol_ref,
                       ssel_ref, wuv_ref, lat_hbm, krt_hbm, o_ref,
                       lat_buf, kr_buf, sem, m_sc, l_sc, acc_sc, lhs_sc, lb_sc, *, pp, nj, layer):
    t = pl.program_id(0)
    n_steps = pl.num_programs(0)
    j = t % nj
    slot = t % 2

    def page_copies(step, buf_slot):
        b = step // nj
        base = (step % nj) * pp
        copies = []
        for p in range(pp):
            page = pt_ref[b, base + p]
            copies.append(pltpu.make_async_copy(lat_hbm.at[page, layer], lat_buf.at[buf_slot, p],
                                                sem.at[buf_slot, 0]))
            copies.append(pltpu.make_async_copy(krt_hbm.at[page, layer],
                                                kr_buf.at[buf_slot, :, pl.ds(p * PAGE_SIZE, PAGE_SIZE)],
                                                sem.at[buf_slot, 1]))
        return copies

    for cp in page_copies(jnp.minimum(t + 1, n_steps - 1), 1 - slot):
        cp.start()

    @pl.when(t == 0)
    def _():
        for cp in page_copies(t, slot):
            cp.start()
        lhs_sc[0:MLA_HEADS * QK_NOPE, :] = wukt_ref[...]

    q16 = q_ref[...].astype(F32)

    @pl.when(j == 0)
    def _():
        m_sc[...] = jnp.full(m_sc.shape, -jnp.inf, F32)
        l_sc[...] = jnp.zeros(l_sc.shape, F32)
        acc_sc[...] = jnp.zeros(acc_sc.shape, F32)
        qn = (q16 * gkn_ref[...])[:, 0:QK_NOPE]
        tiled = jnp.concatenate([qn] * MLA_HEADS, axis=1)
        row = lax.broadcasted_iota(jnp.int32, tiled.shape, 0)
        lane = lax.broadcasted_iota(jnp.int32, tiled.shape, 1) // QK_NOPE
        qbd = jnp.where(row == lane, tiled, 0.0).astype(BF16)
        lhs_sc[MLA_HEADS * QK_NOPE:, :] = _dot(qbd, wukt_ref[...]).astype(BF16)

    for cp in page_copies(t, slot):
        cp.wait()

    qr = q16[:, QK_NOPE:QK_HEAD].astype(BF16)
    lhs = lhs_sc[...]
    gcol = gcol_ref[...]
    ssel = ssel_ref[...]
    half = QK_ROPE // 2
    nf = MLA_HEADS * QK_NOPE
    for p in range(pp):
        lb_sc[p * PAGE_SIZE:(p + 1) * PAGE_SIZE, :] = lat_buf[slot, p].astype(BF16)

    ck = DEC_CHUNK
    scores = []
    for c0 in range(0, pp * PAGE_SIZE, ck):
        res = _dot_nt(lhs, lb_sc[c0:c0 + ck, :])
        sq = res[0:nf, :] * res[0:nf, :]
        sn = res[nf:nf + 8, :]
        krt = kr_buf[slot, :, c0:c0 + ck]
        parts = [sq[h * QK_NOPE:(h + 1) * QK_NOPE, :].reshape(8, 8, ck).sum(axis=0) for h in range(MLA_HEADS)]
        parts.append((krt * krt).reshape(4, 8, ck).sum(axis=0))
        parts.append(jnp.zeros((8, ck), F32))
        p_hi, p_lo = _split_bf16(jnp.concatenate(parts, axis=0))
        ss = _dot(ssel, p_hi) + _dot(ssel, p_lo)
        krg = krt * gcol
        cs = cos_ref[:, c0:c0 + ck]
        sn_ = sin_ref[:, c0:c0 + ck]
        x1, x2 = krg[0:half, :], krg[half:, :]
        roped = jnp.concatenate([x1 * cs - x2 * sn_, x2 * cs + x1 * sn_], axis=0)
        sr = _dot(qr, roped.astype(BF16))[0:8, :]
        scores.append((sn + sr) * lax.rsqrt(ss * (1.0 / QK_HEAD) + EPS))
    s_all = jnp.concatenate(scores, axis=1)

    m_old = m_sc[...]
    m_new = jnp.maximum(m_old, jnp.max(s_all, axis=1, keepdims=True))
    alpha = jnp.exp(m_old - m_new)
    p = jnp.exp(s_all - m_new[:, 0:1])
    l_sc[...] = alpha * l_sc[...] + jnp.sum(p, axis=1, keepdims=True)
    acc_sc[...] = alpha[:, 0:1] * acc_sc[...] + _dot(p.astype(BF16), lb_sc[...])
    m_sc[...] = m_new

    @pl.when(t == n_steps - 1)
    def _():
        for cp in page_copies(t, 1 - slot):
            cp.wait()

    @pl.when(j == nj - 1)
    def _():
        s_self = jnp.sum((q16 * k_ref[...].astype(F32))[0:8, :], axis=1, keepdims=True)
        m_o = m_sc[...]
        m_n = jnp.maximum(m_o, s_self)
        al = jnp.exp(m_o - m_n)
        ps = jnp.exp(s_self - m_n)
        l = al * l_sc[...] + ps
        acc = al[:, 0:1] * acc_sc[...] + ps[:, 0:1] * latn_ref[...]
        o_lat = acc / l[:, 0:1]
        res = _dot(o_lat.astype(BF16), wuv_ref[...])
        row = lax.broadcasted_iota(jnp.int32, res.shape, 0)
        lane = lax.broadcasted_iota(jnp.int32, res.shape, 1) // V_HEAD
        o_ref[...] = jnp.sum(jnp.where(row == lane, res, 0.0), axis=0, keepdims=True).astype(o_ref.dtype)


def _mla_decode(page_table, q16, k16, lat_new3, cache_lat, cache_krt, layer, lw, tabs):
    pp = DEC_PAGES
    n = q16.shape[0]
    n_pages = page_table.shape[1]
    nj = n_pages // pp
    cos_t, sin_t = tabs
    fix2 = lambda t, pt: (0, 0)
    seq3 = lambda t, pt: (t // nj, 0, 0)
    tab = lambda t, pt: (0, t % nj)
    weights = [lw["w_uk_t"], lw["g_k_nope"], lw["g_k_rope_col"], lw["ss_select"], lw["w_uv"]]
    in_specs = [pl.BlockSpec((None, 16, HEAD_PAD), seq3), pl.BlockSpec((None, 16, HEAD_PAD), seq3),
                pl.BlockSpec((None, 1, KV_RANK), seq3),
                pl.BlockSpec((QK_ROPE // 2, pp * PAGE_SIZE), tab),
                pl.BlockSpec((QK_ROPE // 2, pp * PAGE_SIZE), tab)]
    in_specs += [pl.BlockSpec(w.shape, fix2) for w in weights]
    in_specs += [pl.BlockSpec(memory_space=pl.ANY), pl.BlockSpec(memory_space=pl.ANY)]
    grid_spec = pltpu.PrefetchScalarGridSpec(
        num_scalar_prefetch=1,
        grid=(n * nj,),
        in_specs=in_specs,
        out_specs=pl.BlockSpec((None, 1, MLA_OUT), seq3),
        scratch_shapes=[pltpu.VMEM((2, pp, PAGE_SIZE, KV_RANK), F32),
                        pltpu.VMEM((2, QK_ROPE, pp * PAGE_SIZE), F32),
                        pltpu.SemaphoreType.DMA((2, 2)),
                        pltpu.VMEM((8, 128), F32), pltpu.VMEM((8, 128), F32), pltpu.VMEM((8, KV_RANK), F32),
                        pltpu.VMEM((DEC_LHS_ROWS, KV_RANK), BF16),
                        pltpu.VMEM((pp * PAGE_SIZE, KV_RANK), BF16)])
    return pl.pallas_call(
        functools.partial(_mla_decode_kernel, pp=pp, nj=nj, layer=layer),
        grid_spec=grid_spec,
        out_shape=jax.ShapeDtypeStruct((n, 1, MLA_OUT), F32),
        compiler_params=_cparams(("arbitrary",), 40),
        name="mla_decode",
    )(page_table, q16, k16, lat_new3, cos_t, sin_t, *weights, cache_lat, cache_krt)


def _dec_mix_kernel(q_ref, k_ref, la_ref, v_ref, gr_ref, u_ref, s_ref, pb_ref, gout_ref, wp_ref, sc_ref,
                    s_o, gla_o, pool_o, pb_o, *, past):
    a = jnp.exp(la_ref[...])
    v4 = v_ref[...]
    vexp = jnp.concatenate([jnp.broadcast_to(v4[h:h + 1, :], (GLA_DK, GLA_DV)) for h in range(GLA_HEADS)],
                           axis=0)
    s_new = a * s_ref[...] + k_ref[...] * vexp
    s_o[...] = s_new
    prod = (q_ref[...] * (GLA_DK ** -0.5)) * s_new
    gr = gr_ref[...]
    gout = gout_ref[...]
    for h in range(GLA_HEADS):
        o = jnp.sum(prod[h * GLA_DK:(h + 1) * GLA_DK, :], axis=0, keepdims=True)
        on = _rms(o, gout)
        g = gr[h:h + 1, :]
        gla_o[h:h + 1, :] = (on * (g * _sigmoid(g))).astype(gla_o.dtype)

    st = pb_ref[...]
    u = u_ref[...]
    ridx = lax.broadcasted_iota(jnp.int32, st.shape, 0)
    parts = []
    for w in POOL_WINDOWS:
        tot = u + jnp.sum(jnp.where(ridx >= POOL_BUF + 1 - w, st, 0.0), axis=0, keepdims=True)
        parts.append(tot / float(min(w, past + 1)) - u)
    pooled = jnp.broadcast_to(_pool_select(parts), (8, POOL_WIDTH)).astype(BF16)
    y = _dot(pooled, wp_ref[...])[0:1, :] * sc_ref[...]
    pool_o[...] = y.astype(pool_o.dtype)
    pb_o[0:POOL_BUF - 1, :] = st[1:POOL_BUF, :]
    pb_o[POOL_BUF - 1:POOL_BUF, :] = u


def _dec_mix(gq, gk, la, gv, gr, u, state_gla4, state_pool, layer, lw, past):
    n = gq.shape[0]
    col = lambda x: x.reshape(n, GLA_QK, 1)
    b3 = lambda b: (b, 0, 0)
    fix = lambda b: (0, 0)
    return pl.pallas_call(
        functools.partial(_dec_mix_kernel, past=past),
        grid=(n,),
        in_specs=[pl.BlockSpec((None, GLA_QK, 1), b3)] * 3
        + [pl.BlockSpec((None, GLA_HEADS, GLA_DV), b3)] * 2
        + [pl.BlockSpec((None, 1, POOL_WIDTH), b3),
           pl.BlockSpec((None, None, GLA_QK, GLA_DV), lambda b: (b, layer, 0, 0)),
           pl.BlockSpec((None, None, POOL_BUF, POOL_WIDTH), lambda b: (b, layer, 0, 0)),
           pl.BlockSpec((1, GLA_DV), fix), pl.BlockSpec((POOL_WIDTH, POOL_WIDTH), fix),
           pl.BlockSpec((1, POOL_WIDTH), fix)],
        out_specs=[pl.BlockSpec((None, GLA_QK, GLA_DV), b3), pl.BlockSpec((None, GLA_HEADS, GLA_DV), b3),
                   pl.BlockSpec((None, 1, POOL_WIDTH), b3), pl.BlockSpec((None, POOL_BUF, POOL_WIDTH), b3)],
        out_shape=[jax.ShapeDtypeStruct((n, GLA_QK, GLA_DV), F32),
                   jax.ShapeDtypeStruct((n, GLA_HEADS, GLA_DV), F32),
                   jax.ShapeDtypeStruct((n, 1, POOL_WIDTH), F32),
                   jax.ShapeDtypeStruct((n, POOL_BUF, POOL_WIDTH), F32)],
        compiler_params=_cparams(("parallel",), 32),
        name="dec_mix",
    )(col(gq), col(gk), col(la), gv.reshape(n, GLA_HEADS, GLA_DV), gr.reshape(n, GLA_HEADS, GLA_DV),
      u.reshape(n, 1, POOL_WIDTH), state_gla4, state_pool, lw["g_gla_out64"], lw["w_pool"], lw["pool_scale"])


def _post_a_kernel(x_ref, mla_ref, gla_ref, pool_ref, wo_ref, gxa_ref, wxq_ref, gxq_ref, x1_ref, q_ref):
    x1 = _mix_out(x_ref[...], mla_ref[...], gla_ref[...], pool_ref[...], wo_ref)
    x1_ref[...] = x1
    q_ref[...] = _xa_query(x1, gxa_ref[...], wxq_ref[...], gxq_ref[...])


def _post_a(x2, mla, gla, pool, lw):
    m = x2.shape[0]
    return pl.pallas_call(
        _post_a_kernel,
        out_shape=[jax.ShapeDtypeStruct((m, D_MODEL), F32), jax.ShapeDtypeStruct((m, D_MODEL), F32)],
        compiler_params=pltpu.CompilerParams(vmem_limit_bytes=32 * MIB),
        name="post_a",
    )(x2, mla, gla, pool, lw["w_out"], lw["norm_xa"], lw["w_xq"], lw["g_xq"])


XA_HALVES = XA_HEAD // 128
XA_ROWS = XA_HALVES * XA_HEADS


def _xa_decode_kernel(q_ref, mk_ref, mv_ref, o_ref):
    k3 = mk_ref[...].reshape(N_MEM, XA_ROWS, 128)
    v3 = mv_ref[...].reshape(N_MEM, XA_ROWS, 128)
    s = jnp.sum(k3 * q_ref[...][None], axis=-1, keepdims=True)
    s = s[:, 0:XA_HEADS] + s[:, XA_HEADS:XA_ROWS]
    s = jnp.concatenate([s, s], axis=1)
    e = jnp.exp(s - jnp.max(s, axis=0, keepdims=True))
    p = e / jnp.sum(e, axis=0, keepdims=True)
    o_ref[...] = jnp.sum(p * v3, axis=0)


def _xa_decode(q3, cache_k, cache_v, layer):
    n = q3.shape[0]
    depth = cache_k.shape[1]
    view = lambda c: c.reshape(n, depth, N_MEM, XA_HEADS, XA_HALVES, 128).transpose(0, 1, 2, 4, 3, 5).reshape(
        n, depth, N_MEM * XA_ROWS, 128)
    q8 = q3.reshape(n, XA_HEADS, XA_HALVES, 128).transpose(0, 2, 1, 3).reshape(n, XA_ROWS, 128)
    b3 = lambda b: (b, 0, 0)
    mem = lambda b: (b, layer, 0, 0)
    o = pl.pallas_call(
        _xa_decode_kernel,
        grid=(n,),
        in_specs=[pl.BlockSpec((None, XA_ROWS, 128), b3),
                  pl.BlockSpec((None, None, N_MEM * XA_ROWS, 128), mem),
                  pl.BlockSpec((None, None, N_MEM * XA_ROWS, 128), mem)],
        out_specs=pl.BlockSpec((None, XA_ROWS, 128), b3),
        out_shape=jax.ShapeDtypeStruct((n, XA_ROWS, 128), F32),
        compiler_params=_cparams(("parallel",), 32),
        name="xa_decode",
    )(q8, view(cache_k), view(cache_v))
    return o.reshape(n, XA_HALVES, XA_HEADS, 128).transpose(0, 2, 1, 3).reshape(n, 1, D_MODEL)


def _post_b_kernel(x1_ref, o_ref, wxo_ref, x2_ref):
    x2_ref[...] = x1_ref[...] + _dot(o_ref[...].astype(BF16), wxo_ref[...])


def _post_b(x1, o, lw):
    return pl.pallas_call(
        _post_b_kernel,
        out_shape=jax.ShapeDtypeStruct(x1.shape, F32),
        compiler_params=pltpu.CompilerParams(vmem_limit_bytes=32 * MIB),
        name="post_b",
    )(x1, o, lw["w_xo"])


def _rope_tables_rows(pos):
    half = QK_ROPE // 2
    inv = ROPE_THETA ** (-jnp.arange(half, dtype=F32) / half)
    ang = pos.astype(F32)[:, None] * inv[None, :]
    cos, sin = jnp.cos(ang), jnp.sin(ang)
    n = pos.shape[0]
    one = jnp.ones((n, QK_NOPE), F32)
    zero = jnp.zeros((n, QK_NOPE), F32)
    pad1 = jnp.ones((n, HEAD_PAD - QK_HEAD), F32)
    pad0 = jnp.zeros((n, HEAD_PAD - QK_HEAD), F32)
    zh = jnp.zeros((n, half), F32)
    c = jnp.concatenate([one, cos, cos, pad1], axis=1)
    s1 = jnp.concatenate([zero, -sin, zh, pad0], axis=1)
    s2 = jnp.concatenate([zero, zh, sin, pad0], axis=1)
    return c, s1, s2


def _rope_tables_cols(pos):
    half = QK_ROPE // 2
    inv = ROPE_THETA ** (-jnp.arange(half, dtype=F32) / half)
    ang = pos.astype(F32)[:, None] * inv[None, :]
    return jnp.cos(ang).T, jnp.sin(ang).T


def _pad_heads(w, width):
    pad = [(0, 0)] * (w.ndim - 1) + [(0, HEAD_PAD - width)]
    w = jnp.pad(w, pad)
    return w.reshape(w.shape[:-2] + (w.shape[-2] * HEAD_PAD,))


def _layer_weights(l, p):
    w_in = p["w_in"][l]
    sizes = (Q_RANK, KV_RANK, QK_ROPE, GLA_QK, GLA_QK, GLA_OUT, GLA_GATE_RANK, GLA_OUT, POOL_WIDTH)
    offs = [0]
    for s in sizes:
        offs.append(offs[-1] + s)
    piece = lambda i: w_in[:, offs[i]:offs[i + 1]]
    w_in_p = jnp.concatenate([piece(0), piece(1), piece(3), piece(4), piece(5), piece(7), piece(8), piece(2),
                              piece(6), jnp.zeros((D_MODEL, IN_PAD - sum(sizes)), F32)], axis=1)
    g_q = _pad_heads(jnp.broadcast_to(p["g_qk_q"][l], (MLA_HEADS, QK_HEAD)), QK_HEAD)
    g_k = _pad_heads(jnp.broadcast_to(p["g_qk_k"][l], (MLA_HEADS, QK_HEAD)), QK_HEAD)
    place = jnp.zeros((128, QK_W), F32)
    j = jnp.arange(QK_ROPE)
    for h in range(MLA_HEADS):
        place = place.at[j, h * HEAD_PAD + QK_NOPE + j].set(1.0)
    w_gate = jnp.zeros((128, GLA_QK), F32).at[QK_ROPE:QK_ROPE + GLA_GATE_RANK].set(p["w_gate_up"][l])
    wp = p["w_pool"][l]
    w_pool = jnp.zeros((POOL_WIDTH, POOL_WIDTH), F32)
    for g in range(len(POOL_WINDOWS)):
        w_pool = w_pool.at[g * POOL_GROUP:(g + 1) * POOL_GROUP, g * POOL_GROUP:(g + 1) * POOL_GROUP].set(wp[g])
    blk = jnp.arange(GLA_OUT) // GLA_DV
    row = lambda v: v.reshape(1, -1).astype(F32)
    bf = lambda v: v.astype(BF16)
    w_uk2 = p["w_uk"][l].reshape(KV_RANK, MLA_HEADS * QK_NOPE)
    g_k_nope = jnp.zeros((1, HEAD_PAD), F32).at[0, :QK_NOPE].set(p["g_qk_k"][l][:QK_NOPE])
    col = jnp.arange(DEC_SS_ROWS)[None, :] // 8
    hrow = jnp.arange(8)[:, None]
    ss_select = ((hrow < MLA_HEADS) & ((col == hrow) | (col == MLA_HEADS))).astype(F32)
    return {
        "norm_mix": row(p["norm_mix"][l]), "w_in": bf(w_in_p), "g_q_lat": row(p["g_q_lat"][l]),
        "w_uq": bf(_pad_heads(p["w_uq"][l], QK_HEAD)), "g_kv_lat": row(p["g_kv_lat"][l]),
        "w_uk": bf(_pad_heads(p["w_uk"][l], QK_NOPE)), "w_uv": bf(p["w_uv"][l].reshape(KV_RANK, MLA_OUT)),        "g_qk_q": row(g_q), "g_qk_k": row(g_k), "place": bf(place), "w_gate": bf(w_gate),
        "b_gate": row(p["b_gate"][l]),
        "e64": bf((blk[:, None] == blk[None, :]).astype(F32)),
        "g_gla_out": row(jnp.tile(p["g_gla_out"][l], GLA_HEADS)), "g_gla_out64": row(p["g_gla_out"][l]),
        "w_pool": bf(w_pool), "pool_scale": row(p["pool_scale"][l]),
        "w_out": bf(p["w_out"][l]), "norm_xa": row(p["norm_xa"][l]), "w_xq": bf(p["w_xq"][l]),
        "g_xq": row(jnp.tile(p["g_xq"][l], XA_HEADS)), "w_xo": bf(p["w_xo"][l]),
        "norm_mlp": row(p["norm_mlp"][l]), "w_ff1": bf(p["w_ff1"][l]), "w_ff2": bf(p["w_ff2"][l]),
        "w_uk_t": bf(w_uk2.T), "g_k_nope": g_k_nope,
        "g_k_rope_col": jnp.broadcast_to(p["g_qk_k"][l][QK_NOPE:, None], (QK_ROPE, DEC_CHUNK)).astype(F32),
        "ss_select": bf(ss_select),
    }


def _prompt_layer(x2, layer, mk4, mv4, lw, ropes, b, t):
    lat, kr, q, k, v, gq, gk, gv, la, gr, u = _in_proj(x2, lw, ropes, 512)
    r3 = lambda a: a.reshape(b, t, a.shape[-1])
    mla = _mla_prompt(r3(q), r3(k), r3(v), 512).reshape(b * t, MLA_OUT)
    gla, st = _gla_prompt(r3(gq), r3(gk), r3(gv), r3(la), r3(gr), lw)
    pool = _pool_prompt(r3(u), lw, t).reshape(b * t, POOL_WIDTH)
    x2 = _post_prompt(x2, mla, gla.reshape(b * t, GLA_OUT), pool, mk4, mv4, layer, lw, t, 512)
    x2 = _mlp(x2, lw, 512)
    gla_state = jnp.stack([st[:, h * GLA_DV:(h + 1) * GLA_DV, h * GLA_DK:(h + 1) * GLA_DK]
                           for h in range(GLA_HEADS)], axis=1).transpose(0, 1, 3, 2)
    outs = (lat.reshape(b, t, KV_RANK), kr.reshape(b, t, QK_ROPE), gla_state,
            r3(u)[:, t - POOL_BUF:])
    return x2, outs


def _sample_layer(x2, layer, lw, ropes, tabs, page_table, cache_lat, cache_krt, state_gla4, state_pool,
                  cache_mk, cache_mv, past):
    n = x2.shape[0]
    lat, kr, q, k, v, gq, gk, gv, la, gr, u = _in_proj(x2, lw, ropes, n)
    pad16 = lambda a: jnp.pad(a.reshape(n, MLA_HEADS, HEAD_PAD), ((0, 0), (0, 16 - MLA_HEADS), (0, 0)))
    mla = _mla_decode(page_table, pad16(q), pad16(k), lat.reshape(n, 1, KV_RANK), cache_lat, cache_krt, layer,
                      lw, tabs).reshape(n, MLA_OUT)
    s_new, gla, pool, pb = _dec_mix(gq, gk, la, gv, gr, u, state_gla4, state_pool, layer, lw, past)
    x1, qx = _post_a(x2, mla, gla.reshape(n, GLA_OUT), pool.reshape(n, POOL_WIDTH), lw)
    ox = _xa_decode(qx.reshape(n, 1, D_MODEL), cache_mk, cache_mv, layer).reshape(n, D_MODEL)
    x2 = _mlp(_post_b(x1, ox, lw), lw, n)
    outs = (lat.reshape(n, 1, KV_RANK), kr.reshape(n, 1, QK_ROPE),
            s_new.reshape(n, GLA_HEADS, GLA_DK, GLA_DV), pb)
    return x2, outs


def kernel(x_prompt, x_sample, mem_prompt, cache_mla_latent, cache_mla_krope, state_gla, state_pool,
           cache_mem_k, cache_mem_v, page_table, norm_mix, w_in, g_q_lat, w_uq, g_kv_lat, w_uk, w_uv,
           g_qk_q, g_qk_k, w_gate_up, b_gate, g_gla_out, w_pool, pool_scale, w_out, norm_xa, norm_mem,
           w_xq, w_xk, w_xv, g_xq, g_xk, w_xo, norm_mlp, w_ff1, w_ff2):
    params = dict(norm_mix=norm_mix, w_in=w_in, g_q_lat=g_q_lat, w_uq=w_uq, g_kv_lat=g_kv_lat, w_uk=w_uk,
                  w_uv=w_uv, g_qk_q=g_qk_q, g_qk_k=g_qk_k, w_gate_up=w_gate_up, b_gate=b_gate,
                  g_gla_out=g_gla_out, w_pool=w_pool, pool_scale=pool_scale, w_out=w_out, norm_xa=norm_xa,
                  norm_mem=norm_mem, w_xq=w_xq, w_xk=w_xk, w_xv=w_xv, g_xq=g_xq, g_xk=g_xk, w_xo=w_xo,
                  norm_mlp=norm_mlp, w_ff1=w_ff1, w_ff2=w_ff2)
    b, t, _ = x_prompt.shape
    n, dec_seq, _ = x_sample.shape
    depth = norm_mix.shape[0]
    n_pages = page_table.shape[1]
    past = n_pages * PAGE_SIZE

    ropes_p = _rope_tables_rows(jnp.arange(t, dtype=jnp.int32))
    ropes_s = _rope_tables_rows(jnp.full((n,), past, jnp.int32))
    tabs = _rope_tables_cols(jnp.arange(past, dtype=jnp.int32))

    yp = x_prompt.reshape(b * t, D_MODEL)
    ys = x_sample.reshape(n * dec_seq, D_MODEL)
    mem2 = mem_prompt.reshape(b * N_MEM, D_MODEL)
    state_gla4 = state_gla.reshape(n, depth, GLA_QK, GLA_DV)
    cache_krt = cache_mla_krope.transpose(0, 1, 3, 2)

    mk, mv, mkb, mvb = _memory_kv(mem2, norm_mem.reshape(depth, 1, D_MODEL), w_xk.astype(BF16),
                                  w_xv.astype(BF16), jnp.tile(g_xk, (1, XA_HEADS)).reshape(depth, 1, D_MODEL))
    mk4 = mkb.reshape(depth, b, N_MEM, D_MODEL)
    mv4 = mvb.reshape(depth, b, N_MEM, D_MODEL)

    p_outs, s_outs = [], []
    for l in range(depth):
        lw = _layer_weights(l, params)
        yp, po = _prompt_layer(yp, l, mk4, mv4, lw, ropes_p, b, t)
        ys, so = _sample_layer(ys, l, lw, ropes_s, tabs, page_table, cache_mla_latent, cache_krt,
                               state_gla4, state_pool, cache_mem_k, cache_mem_v, past)
        p_outs.append(po)
        s_outs.append(so)

    stack = lambda outs, i: jnp.stack([o[i] for o in outs], axis=1)
    return (yp.reshape(b, t, D_MODEL), ys.reshape(n, dec_seq, D_MODEL),
            stack(p_outs, 0), stack(p_outs, 1), stack(p_outs, 2), stack(p_outs, 3),
            mk.reshape(b, depth, N_MEM, XA_HEADS, XA_HEAD), mv.reshape(b, depth, N_MEM, XA_HEADS, XA_HEAD),
            stack(s_outs, 0), stack(s_outs, 1), stack(s_outs, 2), stack(s_outs, 3))
```

```python
import functools

import jax
import jax.numpy as jnp
from jax import lax
from jax.experimental import pallas as pl
from jax.experimental.pallas import tpu as pltpu

F32 = jnp.float32
BF16 = jnp.bfloat16

D_MODEL = 1024
MLA_HEADS = 4
QK_NOPE = 64
QK_ROPE = 32
QK_HEAD = QK_NOPE + QK_ROPE
V_HEAD = 128
Q_RANK = 384
KV_RANK = 256
ROPE_THETA = 10000.0
GLA_HEADS = 4
GLA_DK = 32
GLA_DV = 64
GLA_GATE_RANK = 16
GLA_TAU = 16.0
GLA_CHUNK = 64
GLA_GROUP = 256
POOL_WINDOWS = (2, 4, 8, 16)
POOL_GROUP = 64
POOL_WIDTH = POOL_GROUP * len(POOL_WINDOWS)
POOL_BUF = max(POOL_WINDOWS) - 1
MLA_OUT = MLA_HEADS * V_HEAD
GLA_OUT = GLA_HEADS * GLA_DV
N_MEM = 256
XA_HEADS = 4
XA_HEAD = D_MODEL // XA_HEADS
D_FF = 4 * D_MODEL
PAGE_SIZE = 128
EPS = 1e-6

HEAD_PAD = 128
QK_W = MLA_HEADS * HEAD_PAD
GLA_QK = GLA_HEADS * GLA_DK
IN_PAD = 1792
MISC0 = 1664

MIB = 1024 * 1024


def _cparams(sem, vmem_mib):
    return pltpu.CompilerParams(dimension_semantics=sem, vmem_limit_bytes=vmem_mib * MIB)


def _rms(x, g):
    y = x * lax.rsqrt(jnp.mean(x * x, axis=-1, keepdims=True) + EPS)
    return y * g


def _head_rms(x, g, nh, width, count):
    outs = []
    for h in range(nh):
        p = x[:, h * width:(h + 1) * width]
        ms = jnp.sum(p * p, axis=-1, keepdims=True) * (1.0 / count)
        outs.append(p * lax.rsqrt(ms + EPS) * g[:, h * width:(h + 1) * width])
    return jnp.concatenate(outs, axis=1)


def _dot(a, b):
    return jnp.dot(a, b, preferred_element_type=F32)


def _dot_nt(a, b):
    return lax.dot_general(a, b, (((1,), (1,)), ((), ())), preferred_element_type=F32)


def _split_bf16(x):
    hi = x.astype(BF16)
    lo = (x - hi.astype(F32)).astype(BF16)
    return hi, lo


def _dot_split(x, w):
    hi, lo = _split_bf16(x)
    return _dot(hi, w) + _dot(lo, w)


def _norm_rope(x, xrot, g, grot, c, s):
    outs = []
    for h in range(MLA_HEADS):
        sl = slice(h * HEAD_PAD, (h + 1) * HEAD_PAD)
        p = x[:, sl]
        r = lax.rsqrt(jnp.sum(p * p, axis=-1, keepdims=True) * (1.0 / QK_HEAD) + EPS)
        outs.append((p * r * g[:, sl]) * c + (xrot[:, sl] * r * grot[:, sl]) * s)
    return jnp.concatenate(outs, axis=1)


def _log_sigmoid(x):
    return -(jnp.maximum(-x, 0.0) + jnp.log1p(jnp.exp(-jnp.abs(x))))


def _sigmoid(x):
    return 1.0 / (1.0 + jnp.exp(-x))


def _memkv_kernel(mem_ref, g_ref, wk_ref, wv_ref, gk_ref, k_ref, v_ref, kb_ref, vb_ref):
    m = _rms(mem_ref[...], g_ref[...]).astype(BF16)
    k = _head_rms(_dot(m, wk_ref[...]), gk_ref[...], XA_HEADS, XA_HEAD, XA_HEAD)
    v = _dot(m, wv_ref[...])
    k_ref[...] = k.reshape(k_ref.shape)
    v_ref[...] = v.reshape(v_ref.shape)
    kb_ref[...] = k.astype(BF16)
    vb_ref[...] = v.astype(BF16)


def _memory_kv(mem2, g, wk, wv, gk):
    m = mem2.shape[0]
    depth = wk.shape[0]
    tm = 512
    nb = tm // N_MEM
    vec = lambda l, i: (l, 0, 0)
    return pl.pallas_call(
        _memkv_kernel,
        grid=(depth, m // tm),
        in_specs=[pl.BlockSpec((tm, D_MODEL), lambda l, i: (i, 0)), pl.BlockSpec((None, 1, D_MODEL), vec),
                  pl.BlockSpec((None, D_MODEL, D_MODEL), vec), pl.BlockSpec((None, D_MODEL, D_MODEL), vec),
                  pl.BlockSpec((None, 1, D_MODEL), vec)],
        out_specs=[pl.BlockSpec((nb, None, N_MEM, D_MODEL), lambda l, i: (i, l, 0, 0))] * 2
        + [pl.BlockSpec((None, tm, D_MODEL), lambda l, i: (l, i, 0))] * 2,
        out_shape=[jax.ShapeDtypeStruct((m // N_MEM, depth, N_MEM, D_MODEL), F32)] * 2
        + [jax.ShapeDtypeStruct((depth, m, D_MODEL), BF16)] * 2,
        compiler_params=_cparams(("parallel", "parallel"), 40),
        name="memory_kv",
    )(mem2, g, wk, wv, gk)


def _inproj_kernel(x_ref, gmix_ref, win_ref, gql_ref, wuq_ref, gkv_ref, wuk_ref, wuv_ref,
                   gq_ref, gk_ref, place_ref, wg_ref, bg_ref, wuqr_ref, gqr_ref, gkr_ref, placer_ref,
                   c_ref, s_ref, lat_ref, kr_ref, q_ref, k_ref, v_ref, gq_o, gk_o, gv_o, la_o, gr_o, u_o):
    h = _rms(x_ref[...], gmix_ref[...]).astype(BF16)
    z = _dot(h, win_ref[...])
    c_q = z[:, 0:384]
    c_kv = z[:, 384:640]
    misc = z[:, MISC0:MISC0 + 128]
    gq_o[...] = z[:, 640:768]
    gk_o[...] = z[:, 768:896]
    gv_o[...] = z[:, 896:1152]
    gr_o[...] = z[:, 1152:1408]
    u_o[...] = z[:, 1408:1664]
    kr_ref[...] = misc[:, 0:QK_ROPE]

    lat = _rms(c_kv, gkv_ref[...])
    lat_ref[...] = lat
    latb = lat.astype(BF16)
    c, s = c_ref[...], s_ref[...]

    cqn = _rms(c_q, gql_ref[...]).astype(BF16)
    q = _norm_rope(_dot(cqn, wuq_ref[...]), _dot(cqn, wuqr_ref[...]), gq_ref[...], gqr_ref[...], c, s)
    q_ref[...] = (q * (QK_HEAD ** -0.5)).astype(BF16)

    m_hi, m_lo = _split_bf16(misc)
    k = _dot(latb, wuk_ref[...]) + _dot(m_hi, place_ref[...]) + _dot(m_lo, place_ref[...])
    krot = _dot(m_hi, placer_ref[...]) + _dot(m_lo, placer_ref[...])
    k_ref[...] = _norm_rope(k, krot, gk_ref[...], gkr_ref[...], c, s).astype(BF16)
    v_ref[...] = _dot(latb, wuv_ref[...]).astype(BF16)

    gate = _dot(misc.astype(BF16), wg_ref[...]) + bg_ref[...]
    la_o[...] = _log_sigmoid(gate) * (1.0 / GLA_TAU)


def _in_proj(x2, lw, ropes, tm):
    m = x2.shape[0]
    row = lambda i: (i, 0)
    fix = lambda i: (0, 0)
    c, s = ropes
    if c.shape[0] == m:
        rope_map = row
    else:
        nb = c.shape[0] // tm
        rope_map = lambda i: (i % nb, 0)
    weights = [lw["norm_mix"], lw["w_in"], lw["g_q_lat"], lw["w_uq"], lw["g_kv_lat"], lw["w_uk"],
               lw["w_uv"], lw["g_qk_q"], lw["g_qk_k"], lw["place"], lw["w_gate"], lw["b_gate"],
               lw["w_uq_rot"], lw["g_q_rot"], lw["g_k_rot"], lw["place_rot"]]
    in_specs = [pl.BlockSpec((tm, D_MODEL), row)]
    in_specs += [pl.BlockSpec(w.shape, fix) for w in weights]
    in_specs += [pl.BlockSpec((tm, HEAD_PAD), rope_map)] * 2
    widths = [(KV_RANK, F32), (QK_ROPE, F32), (QK_W, BF16), (QK_W, BF16), (MLA_OUT, BF16),
              (GLA_QK, F32), (GLA_QK, F32), (GLA_OUT, F32), (GLA_QK, F32), (GLA_OUT, F32),
              (POOL_WIDTH, F32)]
    return pl.pallas_call(
        _inproj_kernel,
        grid=(m // tm,),
        in_specs=in_specs,
        out_specs=[pl.BlockSpec((tm, w), row) for w, _ in widths],
        out_shape=[jax.ShapeDtypeStruct((m, w), dt) for w, dt in widths],
        compiler_params=_cparams(("parallel",), 48),
        name="in_proj",
    )(x2, *weights, c, s)


def _flash_kernel(q_ref, k_ref, v_ref, o_ref, *, tq):
    qi = pl.program_id(2)
    q = q_ref[...]

    def block(j, carry, masked):
        m, l, acc = carry
        start = pl.multiple_of(j * tq, tq)
        k = k_ref[pl.ds(start, tq), :]
        v = v_ref[pl.ds(start, tq), :]
        s = _dot_nt(q, k)
        if masked:
            rows = lax.broadcasted_iota(jnp.int32, (tq, tq), 0)
            cols = lax.broadcasted_iota(jnp.int32, (tq, tq), 1)
            s = jnp.where(cols <= rows, s, -jnp.inf)
        m_new = jnp.maximum(m, jnp.max(s, axis=-1, keepdims=True))
        alpha = jnp.exp(m - m_new)
        p = jnp.exp(s - m_new)
        l = alpha * l + jnp.sum(p, axis=-1, keepdims=True)
        acc = alpha * acc + _dot(p.astype(BF16), v)
        return m_new, l, acc

    init = (jnp.full((tq, 1), -jnp.inf, F32), jnp.zeros((tq, 1), F32), jnp.zeros((tq, V_HEAD), F32))
    carry = lax.fori_loop(0, qi, lambda j, cr: block(j, cr, False), init)
    _, l, acc = block(qi, carry, True)
    o_ref[...] = (acc / l).astype(o_ref.dtype)


def _mla_prompt(q3, k3, v3, tq):
    b, t, _ = q3.shape
    qmap = lambda bi, h, qi: (bi, qi, h)
    kmap = lambda bi, h, qi: (bi, 0, h)
    return pl.pallas_call(
        functools.partial(_flash_kernel, tq=tq),
        grid=(b, MLA_HEADS, t // tq),
        in_specs=[pl.BlockSpec((None, tq, HEAD_PAD), qmap), pl.BlockSpec((None, t, HEAD_PAD), kmap),
                  pl.BlockSpec((None, t, V_HEAD), kmap)],
        out_specs=pl.BlockSpec((None, tq, V_HEAD), qmap),
        out_shape=jax.ShapeDtypeStruct((b, t, MLA_OUT), BF16),
        compiler_params=_cparams(("parallel", "parallel", "arbitrary"), 40),
        name="mla_prompt",
    )(q3, k3, v3)


def _gla_kernel(q_ref, k_ref, v_ref, la_ref, gr_ref, tri_ref, ones_ref, e64_ref, gout_ref, o_ref, st_ref, *, t):
    c = GLA_CHUNK
    gt = GLA_GROUP
    tri = tri_ref[...]
    ones_bd = ones_ref[...]
    e64 = e64_ref[...]
    gout = gout_ref[...]
    lane_qk = lax.broadcasted_iota(jnp.int32, (gt, GLA_QK), 1) // GLA_DK
    lane_v = lax.broadcasted_iota(jnp.int32, (gt, GLA_OUT), 1) // GLA_DV
    row_chunk = lax.broadcasted_iota(jnp.int32, (gt, GLA_QK), 0) // c
    rr = lax.broadcasted_iota(jnp.int32, (gt, gt), 0)
    cc = lax.broadcasted_iota(jnp.int32, (gt, gt), 1)
    causal = (cc <= rr) & (rr // c == cc // c)
    bd = (lax.broadcasted_iota(jnp.int32, (GLA_OUT, GLA_QK), 0) // GLA_DV
          == lax.broadcasted_iota(jnp.int32, (GLA_OUT, GLA_QK), 1) // GLA_DK)

    def group(gi, st):
        r0 = pl.multiple_of(gi * gt, gt)
        q = q_ref[pl.ds(r0, gt), :] * (GLA_DK ** -0.5)
        k = k_ref[pl.ds(r0, gt), :]
        v = v_ref[pl.ds(r0, gt), :]
        a_hi, a_lo = _split_bf16(la_ref[pl.ds(r0, gt), :])
        b = _dot(tri, a_hi) + _dot(tri, a_lo)
        tot = _dot(ones_bd, a_hi) + _dot(ones_bd, a_lo)
        qd = q * jnp.exp(b)
        qdb = qd.astype(BF16)
        kd = (k * jnp.exp(-b)).astype(BF16)
        k2 = k * jnp.exp(tot - b)
        decay = jnp.exp(tot)
        vb = v.astype(BF16)
        vt = v.T.astype(BF16)
        o = jnp.zeros((gt, GLA_OUT), F32)
        for h in range(GLA_HEADS):
            qh = jnp.where(lane_qk == h, qd, 0.0).astype(BF16)
            att = jnp.where(causal, _dot_nt(qh, kd), 0.0)
            o = o + jnp.where(lane_v == h, _dot(att.astype(BF16), vb), 0.0)
        inter = []
        for ci in range(gt // c):
            inter.append(_dot_nt(qdb[ci * c:(ci + 1) * c, :], st.astype(BF16)))
            k2c = jnp.where(row_chunk == ci, k2, 0.0).astype(BF16)
            st = st * decay[ci * c:ci * c + 1, :] + jnp.where(bd, _dot(vt, k2c), 0.0)
        o = o + jnp.concatenate(inter, axis=0)
        ms = _dot_split(o * o, e64) * (1.0 / GLA_DV)
        on = o * lax.rsqrt(ms + EPS) * gout
        gr = gr_ref[pl.ds(r0, gt), :]
        o_ref[pl.ds(r0, gt), :] = (on * (gr * _sigmoid(gr))).astype(o_ref.dtype)
        return st

    st_ref[...] = lax.fori_loop(0, t // gt, group, jnp.zeros((GLA_OUT, GLA_QK), F32), unroll=2)


def _gla_prompt(gq3, gk3, gv3, la3, gr3, lw):
    b, t, _ = gq3.shape
    bmap = lambda i: (i, 0, 0)
    fix = lambda i: (0, 0)
    chunk_id = jnp.arange(GLA_GROUP) // GLA_CHUNK
    same = chunk_id[:, None] == chunk_id[None, :]
    tri = (same & (jnp.arange(GLA_GROUP)[None, :] <= jnp.arange(GLA_GROUP)[:, None])).astype(BF16)
    return pl.pallas_call(
        functools.partial(_gla_kernel, t=t),
        grid=(b,),
        in_specs=[pl.BlockSpec((None, t, GLA_QK), bmap), pl.BlockSpec((None, t, GLA_QK), bmap),
                  pl.BlockSpec((None, t, GLA_OUT), bmap), pl.BlockSpec((None, t, GLA_QK), bmap),
                  pl.BlockSpec((None, t, GLA_OUT), bmap), pl.BlockSpec((GLA_GROUP, GLA_GROUP), fix),
                  pl.BlockSpec((GLA_GROUP, GLA_GROUP), fix),
                  pl.BlockSpec((GLA_OUT, GLA_OUT), fix), pl.BlockSpec((1, GLA_OUT), fix)],
        out_specs=[pl.BlockSpec((None, t, GLA_OUT), bmap), pl.BlockSpec((None, GLA_OUT, GLA_QK), bmap)],
        out_shape=[jax.ShapeDtypeStruct((b, t, GLA_OUT), BF16),
                   jax.ShapeDtypeStruct((b, GLA_OUT, GLA_QK), F32)],
        compiler_params=_cparams(("parallel",), 40),
        name="gla_prompt",
    )(gq3, gk3, gv3, la3, gr3, tri, same.astype(BF16), lw["e64"], lw["g_gla_out"])


def _pool_select(parts):
    lane = lax.broadcasted_iota(jnp.int32, parts[0].shape, 1) // POOL_GROUP
    out = parts[-1]
    for g in range(len(parts) - 2, -1, -1):
        out = jnp.where(lane == g, parts[g], out)
    return out


def _pool_kernel(u_ref, halo_ref, w_ref, sc_ref, o_ref, *, tp):
    i = pl.program_id(1)
    u = u_ref[...]
    halo = jnp.where(i == 0, 0.0, halo_ref[...])
    ext = jnp.concatenate([halo, u], axis=0)
    hb = halo.shape[0]
    pos = i * tp + lax.broadcasted_iota(jnp.int32, (tp, 1), 0)
    sums = ext
    parts = []
    span = 1
    for w in POOL_WINDOWS:
        while span < w:
            sums = sums + pltpu.roll(sums, span, 0)
            span *= 2
        cnt = jnp.minimum(w, pos + 1).astype(F32)
        parts.append(sums[hb:, :] / cnt - u)
    pooled = _pool_select(parts)
    o_ref[...] = (_dot(pooled.astype(BF16), w_ref[...]) * sc_ref[...]).astype(o_ref.dtype)


def _pool_prompt(u3, lw, tp):
    b, t, _ = u3.shape
    hb = 16
    return pl.pallas_call(
        functools.partial(_pool_kernel, tp=tp),
        grid=(b, t // tp),
        in_specs=[pl.BlockSpec((None, tp, POOL_WIDTH), lambda bi, i: (bi, i, 0)),
                  pl.BlockSpec((None, hb, POOL_WIDTH),
                               lambda bi, i: (bi, jnp.maximum(i * (tp // hb) - 1, 0), 0)),
                  pl.BlockSpec((POOL_WIDTH, POOL_WIDTH), lambda bi, i: (0, 0)),
                  pl.BlockSpec((1, POOL_WIDTH), lambda bi, i: (0, 0))],
        out_specs=pl.BlockSpec((None, tp, POOL_WIDTH), lambda bi, i: (bi, i, 0)),
        out_shape=jax.ShapeDtypeStruct((b, t, POOL_WIDTH), BF16),
        compiler_params=_cparams(("parallel", "parallel"), 32),
        name="pool_prompt",
    )(u3, u3, lw["w_pool"], lw["pool_scale"])


def _mix_out(x, mla, gla, pool, wo_ref):
    return (x + _dot(mla.astype(BF16), wo_ref[0:MLA_OUT, :])
            + _dot(gla.astype(BF16), wo_ref[MLA_OUT:MLA_OUT + GLA_OUT, :])
            + _dot(pool.astype(BF16), wo_ref[MLA_OUT + GLA_OUT:, :]))


def _xa_query(x1, gxa, wxq, gxq):
    q = _dot(_rms(x1, gxa).astype(BF16), wxq)
    return _head_rms(q, gxq, XA_HEADS, XA_HEAD, XA_HEAD) * (XA_HEAD ** -0.5)


def _xa_attend(qb, kb, vb):
    outs = []
    for h in range(XA_HEADS):
        sl = slice(h * XA_HEAD, (h + 1) * XA_HEAD)
        s = _dot_nt(qb[:, sl], kb[:, sl])
        e = jnp.exp(s - jnp.max(s, axis=-1, keepdims=True))
        p = e / jnp.sum(e, axis=-1, keepdims=True)
        outs.append(_dot(p.astype(BF16), vb[:, sl]))
    return jnp.concatenate(outs, axis=1).astype(BF16)


def _post_kernel(x_ref, mla_ref, gla_ref, pool_ref, wo_ref, gxa_ref, wxq_ref, gxq_ref, mk_ref, mv_ref,
                 wxo_ref, o_ref):
    x1 = _mix_out(x_ref[...], mla_ref[...], gla_ref[...], pool_ref[...], wo_ref)
    qb = _xa_query(x1, gxa_ref[...], wxq_ref[...], gxq_ref[...]).astype(BF16)
    o = _xa_attend(qb, mk_ref[...], mv_ref[...])
    o_ref[...] = x1 + _dot(o, wxo_ref[...])


def _post_prompt(x2, mla, gla, pool, mk4, mv4, layer, lw, t, tm):
    m = x2.shape[0]
    per_seq = t // tm
    row = lambda i: (i, 0)
    fix = lambda i: (0, 0)
    mem = lambda i: (layer, i // per_seq, 0, 0)
    sq = (D_MODEL, D_MODEL)
    return pl.pallas_call(
        _post_kernel,
        grid=(m // tm,),
        in_specs=[pl.BlockSpec((tm, D_MODEL), row), pl.BlockSpec((tm, MLA_OUT), row),
                  pl.BlockSpec((tm, GLA_OUT), row), pl.BlockSpec((tm, POOL_WIDTH), row),
                  pl.BlockSpec(sq, fix), pl.BlockSpec((1, D_MODEL), fix), pl.BlockSpec(sq, fix),
                  pl.BlockSpec((1, D_MODEL), fix), pl.BlockSpec((None, None, N_MEM, D_MODEL), mem),
                  pl.BlockSpec((None, None, N_MEM, D_MODEL), mem), pl.BlockSpec(sq, fix)],
        out_specs=pl.BlockSpec((tm, D_MODEL), row),
        out_shape=jax.ShapeDtypeStruct((m, D_MODEL), F32),
        compiler_params=_cparams(("parallel",), 48),
        name="post_prompt",
    )(x2, mla, gla, pool, lw["w_out"], lw["norm_xa"], lw["w_xq"], lw["g_xq"], mk4, mv4, lw["w_xo"])


def _mlp_kernel(x_ref, g_ref, w1_ref, w2_ref, o_ref):
    x = x_ref[...]
    h = _rms(x, g_ref[...]).astype(BF16)
    a = jnp.maximum(_dot(h, w1_ref[...]), 0.0)
    o_ref[...] = x + _dot((a * a).astype(BF16), w2_ref[...])


def _mlp(x2, lw, tm):
    m = x2.shape[0]
    once = pl.Buffered(1)
    return pl.pallas_call(
        _mlp_kernel,
        grid=(m // tm,),
        in_specs=[pl.BlockSpec((tm, D_MODEL), lambda i: (i, 0)),
                  pl.BlockSpec((1, D_MODEL), lambda i: (0, 0)),
                  pl.BlockSpec((D_MODEL, D_FF), lambda i: (0, 0), pipeline_mode=once),
                  pl.BlockSpec((D_FF, D_MODEL), lambda i: (0, 0), pipeline_mode=once)],
        out_specs=pl.BlockSpec((tm, D_MODEL), lambda i: (i, 0)),
        out_shape=jax.ShapeDtypeStruct((m, D_MODEL), F32),
        compiler_params=_cparams(("parallel",), 56),
        name="mlp",
    )(x2, lw["norm_mlp"], lw["w_ff1"], lw["w_ff2"])


DEC_PAGES = 32
DEC_CHUNK = 4096
DEC_LHS_ROWS = MLA_HEADS * QK_NOPE + 16
DEC_SS_ROWS = 48


def _mla_decode_kernel(pt_ref, q_ref, k_ref, latn_ref, cos_ref, sin_ref, wukt_ref, gkn_ref, gcol_ref,
                       ssel_ref, wuv_ref, lat_hbm, krt_hbm, o_ref,
                       lat_buf, kr_buf, sem, m_sc, l_sc, acc_sc, lhs_sc, lb_sc, *, pp, nj, layer):
    t = pl.program_id(0)
    n_steps = pl.num_programs(0)
    j = t % nj
    slot = t % 2

    def page_copies(step, buf_slot):
        b = step // nj
        base = (step % nj) * pp
        copies = []
        for p in range(pp):
            page = pt_ref[b, base + p]
            copies.append(pltpu.make_async_copy(lat_hbm.at[page, layer], lat_buf.at[buf_slot, p],
                                                sem.at[buf_slot, 0]))
            copies.append(pltpu.make_async_copy(krt_hbm.at[page, layer],
                                                kr_buf.at[buf_slot, :, pl.ds(p * PAGE_SIZE, PAGE_SIZE)],
                                                sem.at[buf_slot, 1]))
        return copies

    def start_all(copies):
        for i, cp in enumerate(copies):
            cp.start(priority=(i // 2 + i) % 2)

    start_all(page_copies(jnp.minimum(t + 1, n_steps - 1), 1 - slot))

    @pl.when(t == 0)
    def _():
        start_all(page_copies(t, slot))
        lhs_sc[0:MLA_HEADS * QK_NOPE, :] = wukt_ref[...]

    q16 = q_ref[...].astype(F32)

    @pl.when(j == 0)
    def _():
        m_sc[...] = jnp.full(m_sc.shape, -jnp.inf, F32)
        l_sc[...] = jnp.zeros(l_sc.shape, F32)
        acc_sc[...] = jnp.zeros(acc_sc.shape, F32)
        qn = (q16 * gkn_ref[...])[:, 0:QK_NOPE]
        tiled = jnp.concatenate([qn] * MLA_HEADS, axis=1)
        row = lax.broadcasted_iota(jnp.int32, tiled.shape, 0)
        lane = lax.broadcasted_iota(jnp.int32, tiled.shape, 1) // QK_NOPE
        qbd = jnp.where(row == lane, tiled, 0.0).astype(BF16)
        lhs_sc[MLA_HEADS * QK_NOPE:, :] = _dot(qbd, wukt_ref[...]).astype(BF16)

    for cp in page_copies(t, slot):
        cp.wait()

    qr = q16[:, QK_NOPE:QK_HEAD].astype(BF16)
    lhs = lhs_sc[...]
    gcol = gcol_ref[...]
    ssel = ssel_ref[...]
    half = QK_ROPE // 2
    nf = MLA_HEADS * QK_NOPE
    for p in range(pp):
        lb_sc[p * PAGE_SIZE:(p + 1) * PAGE_SIZE, :] = lat_buf[slot, p].astype(BF16)

    ck = DEC_CHUNK
    scores = []
    for c0 in range(0, pp * PAGE_SIZE, ck):
        res = _dot_nt(lhs, lb_sc[c0:c0 + ck, :])
        sq = res[0:nf, :] * res[0:nf, :]
        sn = res[nf:nf + 8, :]
        krt = kr_buf[slot, :, c0:c0 + ck]
        parts = [sq[h * QK_NOPE:(h + 1) * QK_NOPE, :].reshape(8, 8, ck).sum(axis=0) for h in range(MLA_HEADS)]
        parts.append((krt * krt).reshape(4, 8, ck).sum(axis=0))
        parts.append(jnp.zeros((8, ck), F32))
        p_hi, p_lo = _split_bf16(jnp.concatenate(parts, axis=0))
        ss = _dot(ssel, p_hi) + _dot(ssel, p_lo)
        krg = krt * gcol
        cs = cos_ref[:, c0:c0 + ck]
        sn_ = sin_ref[:, c0:c0 + ck]
        x1, x2 = krg[0:half, :], krg[half:, :]
        roped = jnp.concatenate([x1 * cs - x2 * sn_, x2 * cs + x1 * sn_], axis=0)
        sr = _dot(qr, roped.astype(BF16))[0:8, :]
        scores.append((sn + sr) * lax.rsqrt(ss * (1.0 / QK_HEAD) + EPS))
    s_all = jnp.concatenate(scores, axis=1)

    m_old = m_sc[...]
    m_new = jnp.maximum(m_old, jnp.max(s_all, axis=1, keepdims=True))
    alpha = jnp.exp(m_old - m_new)
    p = jnp.exp(s_all - m_new[:, 0:1])
    l_sc[...] = alpha * l_sc[...] + jnp.sum(p, axis=1, keepdims=True)
    acc_sc[...] = alpha[:, 0:1] * acc_sc[...] + _dot(p.astype(BF16), lb_sc[...])
    m_sc[...] = m_new

    @pl.when(t == n_steps - 1)
    def _():
        for cp in page_copies(t, 1 - slot):
            cp.wait()

    @pl.when(j == nj - 1)
    def _():
        s_self = jnp.sum((q16 * k_ref[...].astype(F32))[0:8, :], axis=1, keepdims=True)
        m_o = m_sc[...]
        m_n = jnp.maximum(m_o, s_self)
        al = jnp.exp(m_o - m_n)
        ps = jnp.exp(s_self - m_n)
        l = al * l_sc[...] + ps
        acc = al[:, 0:1] * acc_sc[...] + ps[:, 0:1] * latn_ref[...]
        o_lat = acc / l[:, 0:1]
        res = _dot(o_lat.astype(BF16), wuv_ref[...])
        row = lax.broadcasted_iota(jnp.int32, res.shape, 0)
        lane = lax.broadcasted_iota(jnp.int32, res.shape, 1) // V_HEAD
        o_ref[...] = jnp.sum(jnp.where(row == lane, res, 0.0), axis=0, keepdims=True).astype(o_ref.dtype)


def _mla_decode(page_table, q16, k16, lat_new3, cache_lat, cache_krt, layer, lw, tabs):
    pp = DEC_PAGES
    n = q16.shape[0]
    n_pages = page_table.shape[1]
    nj = n_pages // pp
    cos_t, sin_t = tabs
    fix2 = lambda t, pt: (0, 0)
    seq3 = lambda t, pt: (t // nj, 0, 0)
    tab = lambda t, pt: (0, t % nj)
    weights = [lw["w_uk_t"], lw["g_k_nope"], lw["g_k_rope_col"], lw["ss_select"], lw["w_uv"]]
    in_specs = [pl.BlockSpec((None, 16, HEAD_PAD), seq3), pl.BlockSpec((None, 16, HEAD_PAD), seq3),
                pl.BlockSpec((None, 1, KV_RANK), seq3),
                pl.BlockSpec((QK_ROPE // 2, pp * PAGE_SIZE), tab),
                pl.BlockSpec((QK_ROPE // 2, pp * PAGE_SIZE), tab)]
    in_specs += [pl.BlockSpec(w.shape, fix2) for w in weights]
    in_specs += [pl.BlockSpec(memory_space=pl.ANY), pl.BlockSpec(memory_space=pl.ANY)]
    grid_spec = pltpu.PrefetchScalarGridSpec(
        num_scalar_prefetch=1,
        grid=(n * nj,),
        in_specs=in_specs,
        out_specs=pl.BlockSpec((None, 1, MLA_OUT), seq3),
        scratch_shapes=[pltpu.VMEM((2, pp, PAGE_SIZE, KV_RANK), F32),
                        pltpu.VMEM((2, QK_ROPE, pp * PAGE_SIZE), F32),
                        pltpu.SemaphoreType.DMA((2, 2)),
                        pltpu.VMEM((8, 128), F32), pltpu.VMEM((8, 128), F32), pltpu.VMEM((8, KV_RANK), F32),
                        pltpu.VMEM((DEC_LHS_ROWS, KV_RANK), BF16),
                        pltpu.VMEM((pp * PAGE_SIZE, KV_RANK), BF16)])
    return pl.pallas_call(
        functools.partial(_mla_decode_kernel, pp=pp, nj=nj, layer=layer),
        grid_spec=grid_spec,
        out_shape=jax.ShapeDtypeStruct((n, 1, MLA_OUT), F32),
        compiler_params=_cparams(("arbitrary",), 40),
        name="mla_decode",
    )(page_table, q16, k16, lat_new3, cos_t, sin_t, *weights, cache_lat, cache_krt)


def _dec_mix_kernel(q_ref, k_ref, la_ref, v_ref, gr_ref, u_ref, s_ref, pb_ref, gout_ref, wp_ref, sc_ref,
                    s_o, gla_o, pool_o, pb_o, *, past):
    a = jnp.exp(la_ref[...])
    v4 = v_ref[...]
    vexp = jnp.concatenate([jnp.broadcast_to(v4[h:h + 1, :], (GLA_DK, GLA_DV)) for h in range(GLA_HEADS)],
                           axis=0)
    s_new = a * s_ref[...] + k_ref[...] * vexp
    s_o[...] = s_new
    prod = (q_ref[...] * (GLA_DK ** -0.5)) * s_new
    gr = gr_ref[...]
    gout = gout_ref[...]
    for h in range(GLA_HEADS):
        o = jnp.sum(prod[h * GLA_DK:(h + 1) * GLA_DK, :], axis=0, keepdims=True)
        on = _rms(o, gout)
        g = gr[h:h + 1, :]
        gla_o[h:h + 1, :] = (on * (g * _sigmoid(g))).astype(gla_o.dtype)

    st = pb_ref[...]
    u = u_ref[...]
    ridx = lax.broadcasted_iota(jnp.int32, st.shape, 0)
    parts = []
    for w in POOL_WINDOWS:
        tot = u + jnp.sum(jnp.where(ridx >= POOL_BUF + 1 - w, st, 0.0), axis=0, keepdims=True)
        parts.append(tot / float(min(w, past + 1)) - u)
    pooled = jnp.broadcast_to(_pool_select(parts), (8, POOL_WIDTH)).astype(BF16)
    y = _dot(pooled, wp_ref[...])[0:1, :] * sc_ref[...]
    pool_o[...] = y.astype(pool_o.dtype)
    pb_o[0:POOL_BUF - 1, :] = st[1:POOL_BUF, :]
    pb_o[POOL_BUF - 1:POOL_BUF, :] = u


def _dec_mix(gq, gk, la, gv, gr, u, state_gla4, state_pool, layer, lw, past):
    n = gq.shape[0]
    col = lambda x: x.reshape(n, GLA_QK, 1)
    b3 = lambda b: (b, 0, 0)
    fix = lambda b: (0, 0)
    return pl.pallas_call(
        functools.partial(_dec_mix_kernel, past=past),
        grid=(n,),
        in_specs=[pl.BlockSpec((None, GLA_QK, 1), b3)] * 3
        + [pl.BlockSpec((None, GLA_HEADS, GLA_DV), b3)] * 2
        + [pl.BlockSpec((None, 1, POOL_WIDTH), b3),
           pl.BlockSpec((None, None, GLA_QK, GLA_DV), lambda b: (b, layer, 0, 0)),
           pl.BlockSpec((None, None, POOL_BUF, POOL_WIDTH), lambda b: (b, layer, 0, 0)),
           pl.BlockSpec((1, GLA_DV), fix), pl.BlockSpec((POOL_WIDTH, POOL_WIDTH), fix),
           pl.BlockSpec((1, POOL_WIDTH), fix)],
        out_specs=[pl.BlockSpec((None, GLA_QK, GLA_DV), b3), pl.BlockSpec((None, GLA_HEADS, GLA_DV), b3),
                   pl.BlockSpec((None, 1, POOL_WIDTH), b3), pl.BlockSpec((None, POOL_BUF, POOL_WIDTH), b3)],
        out_shape=[jax.ShapeDtypeStruct((n, GLA_QK, GLA_DV), F32),
                   jax.ShapeDtypeStruct((n, GLA_HEADS, GLA_DV), F32),
                   jax.ShapeDtypeStruct((n, 1, POOL_WIDTH), F32),
                   jax.ShapeDtypeStruct((n, POOL_BUF, POOL_WIDTH), F32)],
        compiler_params=_cparams(("parallel",), 32),
        name="dec_mix",
    )(col(gq), col(gk), col(la), gv.reshape(n, GLA_HEADS, GLA_DV), gr.reshape(n, GLA_HEADS, GLA_DV),
      u.reshape(n, 1, POOL_WIDTH), state_gla4, state_pool, lw["g_gla_out64"], lw["w_pool"], lw["pool_scale"])


def _post_a_kernel(x_ref, mla_ref, gla_ref, pool_ref, wo_ref, gxa_ref, wxq_ref, gxq_ref, x1_ref, q_ref):
    x1 = _mix_out(x_ref[...], mla_ref[...], gla_ref[...], pool_ref[...], wo_ref)
    x1_ref[...] = x1
    q_ref[...] = _xa_query(x1, gxa_ref[...], wxq_ref[...], gxq_ref[...])


def _post_a(x2, mla, gla, pool, lw):
    m = x2.shape[0]
    return pl.pallas_call(
        _post_a_kernel,
        out_shape=[jax.ShapeDtypeStruct((m, D_MODEL), F32), jax.ShapeDtypeStruct((m, D_MODEL), F32)],
        compiler_params=pltpu.CompilerParams(vmem_limit_bytes=32 * MIB),
        name="post_a",
    )(x2, mla, gla, pool, lw["w_out"], lw["norm_xa"], lw["w_xq"], lw["g_xq"])


XA_HALVES = XA_HEAD // 128
XA_ROWS = XA_HALVES * XA_HEADS


def _xa_decode_kernel(q_ref, mk_ref, mv_ref, o_ref):
    k3 = mk_ref[...].reshape(N_MEM, XA_ROWS, 128)
    v3 = mv_ref[...].reshape(N_MEM, XA_ROWS, 128)
    s = jnp.sum(k3 * q_ref[...][None], axis=-1, keepdims=True)
    s = s[:, 0:XA_HEADS] + s[:, XA_HEADS:XA_ROWS]
    s = jnp.concatenate([s, s], axis=1)
    e = jnp.exp(s - jnp.max(s, axis=0, keepdims=True))
    p = e / jnp.sum(e, axis=0, keepdims=True)
    o_ref[...] = jnp.sum(p * v3, axis=0)


def _xa_decode(q3, cache_k, cache_v, layer):
    n = q3.shape[0]
    depth = cache_k.shape[1]
    view = lambda c: c.reshape(n, depth, N_MEM, XA_HEADS, XA_HALVES, 128).transpose(0, 1, 2, 4, 3, 5).reshape(
        n, depth, N_MEM * XA_ROWS, 128)
    q8 = q3.reshape(n, XA_HEADS, XA_HALVES, 128).transpose(0, 2, 1, 3).reshape(n, XA_ROWS, 128)
    b3 = lambda b: (b, 0, 0)
    mem = lambda b: (b, layer, 0, 0)
    o = pl.pallas_call(
        _xa_decode_kernel,
        grid=(n,),
        in_specs=[pl.BlockSpec((None, XA_ROWS, 128), b3),
                  pl.BlockSpec((None, None, N_MEM * XA_ROWS, 128), mem),
                  pl.BlockSpec((None, None, N_MEM * XA_ROWS, 128), mem)],
        out_specs=pl.BlockSpec((None, XA_ROWS, 128), b3),
        out_shape=jax.ShapeDtypeStruct((n, XA_ROWS, 128), F32),
        compiler_params=_cparams(("parallel",), 32),
        name="xa_decode",
    )(q8, view(cache_k), view(cache_v))
    return o.reshape(n, XA_HALVES, XA_HEADS, 128).transpose(0, 2, 1, 3).reshape(n, 1, D_MODEL)


def _post_b_kernel(x1_ref, o_ref, wxo_ref, x2_ref):
    x2_ref[...] = x1_ref[...] + _dot(o_ref[...].astype(BF16), wxo_ref[...])


def _post_b(x1, o, lw):
    return pl.pallas_call(
        _post_b_kernel,
        out_shape=jax.ShapeDtypeStruct(x1.shape, F32),
        compiler_params=pltpu.CompilerParams(vmem_limit_bytes=32 * MIB),
        name="post_b",
    )(x1, o, lw["w_xo"])


def _rope_tables_rows(pos):
    half = QK_ROPE // 2
    inv = ROPE_THETA ** (-jnp.arange(half, dtype=F32) / half)
    ang = pos.astype(F32)[:, None] * inv[None, :]
    cos, sin = jnp.cos(ang), jnp.sin(ang)
    n = pos.shape[0]
    c = jnp.concatenate([jnp.ones((n, QK_NOPE), F32), cos, cos, jnp.ones((n, HEAD_PAD - QK_HEAD), F32)], axis=1)
    s = jnp.concatenate([jnp.zeros((n, QK_NOPE), F32), sin, sin, jnp.zeros((n, HEAD_PAD - QK_HEAD), F32)],
                        axis=1)
    return c, s


def _rope_tables_cols(pos):
    half = QK_ROPE // 2
    inv = ROPE_THETA ** (-jnp.arange(half, dtype=F32) / half)
    ang = pos.astype(F32)[:, None] * inv[None, :]
    return jnp.cos(ang).T, jnp.sin(ang).T


def _pad_heads(w, width):
    pad = [(0, 0)] * (w.ndim - 1) + [(0, HEAD_PAD - width)]
    w = jnp.pad(w, pad)
    return w.reshape(w.shape[:-2] + (w.shape[-2] * HEAD_PAD,))


def _layer_weights(l, p):
    w_in = p["w_in"][l]
    sizes = (Q_RANK, KV_RANK, QK_ROPE, GLA_QK, GLA_QK, GLA_OUT, GLA_GATE_RANK, GLA_OUT, POOL_WIDTH)
    offs = [0]
    for s in sizes:
        offs.append(offs[-1] + s)
    piece = lambda i: w_in[:, offs[i]:offs[i + 1]]
    w_in_p = jnp.concatenate([piece(0), piece(1), piece(3), piece(4), piece(5), piece(7), piece(8), piece(2),
                              piece(6), jnp.zeros((D_MODEL, IN_PAD - sum(sizes)), F32)], axis=1)
    g_q = _pad_heads(jnp.broadcast_to(p["g_qk_q"][l], (MLA_HEADS, QK_HEAD)), QK_HEAD)
    g_k = _pad_heads(jnp.broadcast_to(p["g_qk_k"][l], (MLA_HEADS, QK_HEAD)), QK_HEAD)
    place = jnp.zeros((128, QK_W), F32)
    j = jnp.arange(QK_ROPE)
    for h in range(MLA_HEADS):
        place = place.at[j, h * HEAD_PAD + QK_NOPE + j].set(1.0)
    w_gate = jnp.zeros((128, GLA_QK), F32).at[QK_ROPE:QK_ROPE + GLA_GATE_RANK].set(p["w_gate_up"][l])
    half = QK_ROPE // 2

    def partner(w):
        w4 = w.reshape(w.shape[:-1] + (MLA_HEADS, HEAD_PAD))
        x1 = w4[..., QK_NOPE:QK_NOPE + half]
        x2 = w4[..., QK_NOPE + half:QK_HEAD]
        out = jnp.zeros_like(w4).at[..., QK_NOPE:QK_NOPE + half].set(x2).at[..., QK_NOPE + half:QK_HEAD].set(x1)
        return out.reshape(w.shape)

    sign = jnp.tile(jnp.concatenate([jnp.zeros((QK_NOPE,), F32), -jnp.ones((half,), F32), jnp.ones((half,), F32),
                                     jnp.zeros((HEAD_PAD - QK_HEAD,), F32)]), MLA_HEADS)
    w_uq_pad = _pad_heads(p["w_uq"][l], QK_HEAD)
    wp = p["w_pool"][l]
    w_pool = jnp.zeros((POOL_WIDTH, POOL_WIDTH), F32)
    for g in range(len(POOL_WINDOWS)):
        w_pool = w_pool.at[g * POOL_GROUP:(g + 1) * POOL_GROUP, g * POOL_GROUP:(g + 1) * POOL_GROUP].set(wp[g])
    blk = jnp.arange(GLA_OUT) // GLA_DV
    row = lambda v: v.reshape(1, -1).astype(F32)
    bf = lambda v: v.astype(BF16)
    w_uk2 = p["w_uk"][l].reshape(KV_RANK, MLA_HEADS * QK_NOPE)
    g_k_nope = jnp.zeros((1, HEAD_PAD), F32).at[0, :QK_NOPE].set(p["g_qk_k"][l][:QK_NOPE])
    col = jnp.arange(DEC_SS_ROWS)[None, :] // 8
    hrow = jnp.arange(8)[:, None]
    ss_select = ((hrow < MLA_HEADS) & ((col == hrow) | (col == MLA_HEADS))).astype(F32)
    return {
        "norm_mix": row(p["norm_mix"][l]), "w_in": bf(w_in_p), "g_q_lat": row(p["g_q_lat"][l]),
        "w_uq": bf(_pad_heads(p["w_uq"][l], QK_HEAD)), "g_kv_lat": row(p["g_kv_lat"][l]),
        "w_uk": bf(_pad_heads(p["w_uk"][l], QK_NOPE)), "w_uv": bf(p["w_uv"][l].reshape(KV_RANK, MLA_OUT)),
        "g_qk_q": row(g_q), "g_qk_k": row(g_k), "place": bf(place), "w_gate": bf(w_gate),
        "w_uq_rot": bf(partner(w_uq_pad) * sign), "g_q_rot": row(partner(g_q)), "g_k_rot": row(partner(g_k)),
        "place_rot": bf(partner(place) * sign),
        "b_gate": row(p["b_gate"][l]),
        "e64": bf((blk[:, None] == blk[None, :]).astype(F32)),
        "g_gla_out": row(jnp.tile(p["g_gla_out"][l], GLA_HEADS)), "g_gla_out64": row(p["g_gla_out"][l]),
        "w_pool": bf(w_pool), "pool_scale": row(p["pool_scale"][l]),
        "w_out": bf(p["w_out"][l]), "norm_xa": row(p["norm_xa"][l]), "w_xq": bf(p["w_xq"][l]),
        "g_xq": row(jnp.tile(p["g_xq"][l], XA_HEADS)), "w_xo": bf(p["w_xo"][l]),
        "norm_mlp": row(p["norm_mlp"][l]), "w_ff1": bf(p["w_ff1"][l]), "w_ff2": bf(p["w_ff2"][l]),
        "w_uk_t": bf(w_uk2.T), "g_k_nope": g_k_nope,
        "g_k_rope_col": jnp.broadcast_to(p["g_qk_k"][l][QK_NOPE:, None], (QK_ROPE, DEC_CHUNK)).astype(F32),
        "ss_select": bf(ss_select),
    }


def _prompt_layer(x2, layer, mk4, mv4, lw, ropes, b, t):
    lat, kr, q, k, v, gq, gk, gv, la, gr, u = _in_proj(x2, lw, ropes, 512)
    r3 = lambda a: a.reshape(b, t, a.shape[-1])
    mla = _mla_prompt(r3(q), r3(k), r3(v), 512).reshape(b * t, MLA_OUT)
    gla, st = _gla_prompt(r3(gq), r3(gk), r3(gv), r3(la), r3(gr), lw)
    pool = _pool_prompt(r3(u), lw, t).reshape(b * t, POOL_WIDTH)
    x2 = _post_prompt(x2, mla, gla.reshape(b * t, GLA_OUT), pool, mk4, mv4, layer, lw, t, 512)
    x2 = _mlp(x2, lw, 512)
    gla_state = jnp.stack([st[:, h * GLA_DV:(h + 1) * GLA_DV, h * GLA_DK:(h + 1) * GLA_DK]
                           for h in range(GLA_HEADS)], axis=1).transpose(0, 1, 3, 2)
    outs = (lat.reshape(b, t, KV_RANK), kr.reshape(b, t, QK_ROPE), gla_state,
            r3(u)[:, t - POOL_BUF:])
    return x2, outs


def _sample_layer(x2, layer, lw, ropes, tabs, page_table, cache_lat, cache_krt, state_gla4, state_pool,
                  cache_mk, cache_mv, past):
    n = x2.shape[0]
    lat, kr, q, k, v, gq, gk, gv, la, gr, u = _in_proj(x2, lw, ropes, n)
    pad16 = lambda a: jnp.pad(a.reshape(n, MLA_HEADS, HEAD_PAD), ((0, 0), (0, 16 - MLA_HEADS), (0, 0)))
    mla = _mla_decode(page_table, pad16(q), pad16(k), lat.reshape(n, 1, KV_RANK), cache_lat, cache_krt, layer,
                      lw, tabs).reshape(n, MLA_OUT)
    s_new, gla, pool, pb = _dec_mix(gq, gk, la, gv, gr, u, state_gla4, state_pool, layer, lw, past)
    x1, qx = _post_a(x2, mla, gla.reshape(n, GLA_OUT), pool.reshape(n, POOL_WIDTH), lw)
    ox = _xa_decode(qx.reshape(n, 1, D_MODEL), cache_mk, cache_mv, layer).reshape(n, D_MODEL)
    x2 = _mlp(_post_b(x1, ox, lw), lw, n)
    outs = (lat.reshape(n, 1, KV_RANK), kr.reshape(n, 1, QK_ROPE),
            s_new.reshape(n, GLA_HEADS, GLA_DK, GLA_DV), pb)
    return x2, outs


def kernel(x_prompt, x_sample, mem_prompt, cache_mla_latent, cache_mla_krope, state_gla, state_pool,
           cache_mem_k, cache_mem_v, page_table, norm_mix, w_in, g_q_lat, w_uq, g_kv_lat, w_uk, w_uv,
           g_qk_q, g_qk_k, w_gate_up, b_gate, g_gla_out, w_pool, pool_scale, w_out, norm_xa, norm_mem,
           w_xq, w_xk, w_xv, g_xq, g_xk, w_xo, norm_mlp, w_ff1, w_ff2):
    params = dict(norm_mix=norm_mix, w_in=w_in, g_q_lat=g_q_lat, w_uq=w_uq, g_kv_lat=g_kv_lat, w_uk=w_uk,
                  w_uv=w_uv, g_qk_q=g_qk_q, g_qk_k=g_qk_k, w_gate_up=w_gate_up, b_gate=b_gate,
                  g_gla_out=g_gla_out, w_pool=w_pool, pool_scale=pool_scale, w_out=w_out, norm_xa=norm_xa,
                  norm_mem=norm_mem, w_xq=w_xq, w_xk=w_xk, w_xv=w_xv, g_xq=g_xq, g_xk=g_xk, w_xo=w_xo,
                  norm_mlp=norm_mlp, w_ff1=w_ff1, w_ff2=w_ff2)
    b, t, _ = x_prompt.shape
    n, dec_seq, _ = x_sample.shape
    depth = norm_mix.shape[0]
    n_pages = page_table.shape[1]
    past = n_pages * PAGE_SIZE

    ropes_p = _rope_tables_rows(jnp.arange(t, dtype=jnp.int32))
    ropes_s = _rope_tables_rows(jnp.full((n,), past, jnp.int32))
    tabs = _rope_tables_cols(jnp.arange(past, dtype=jnp.int32))

    yp = x_prompt.reshape(b * t, D_MODEL)
    ys = x_sample.reshape(n * dec_seq, D_MODEL)
    mem2 = mem_prompt.reshape(b * N_MEM, D_MODEL)
    state_gla4 = state_gla.reshape(n, depth, GLA_QK, GLA_DV)
    cache_krt = cache_mla_krope.transpose(0, 1, 3, 2)

    mk, mv, mkb, mvb = _memory_kv(mem2, norm_mem.reshape(depth, 1, D_MODEL), w_xk.astype(BF16),
                                  w_xv.astype(BF16), jnp.tile(g_xk, (1, XA_HEADS)).reshape(depth, 1, D_MODEL))
    mk4 = mkb.reshape(depth, b, N_MEM, D_MODEL)
    mv4 = mvb.reshape(depth, b, N_MEM, D_MODEL)

    p_outs, s_outs = [], []
    for l in range(depth):
        lw = _layer_weights(l, params)
        yp, po = _prompt_layer(yp, l, mk4, mv4, lw, ropes_p, b, t)
        ys, so = _sample_layer(ys, l, lw, ropes_s, tabs, page_table, cache_mla_latent, cache_krt,
                               state_gla4, state_pool, cache_mem_k, cache_mem_v, past)
        p_outs.append(po)
        s_outs.append(so)

    stack = lambda outs, i: jnp.stack([o[i] for o in outs], axis=1)
    return (yp.reshape(b, t, D_MODEL), ys.reshape(n, dec_seq, D_MODEL),
            stack(p_outs, 0), stack(p_outs, 1), stack(p_outs, 2), stack(p_outs, 3),
            mk.reshape(b, depth, N_MEM, XA_HEADS, XA_HEAD), mv.reshape(b, depth, N_MEM, XA_HEADS, XA_HEAD),
            stack(s_outs, 0), stack(s_outs, 1), stack(s_outs, 2), stack(s_outs, 3))
```

```python
import functools

import jax
import jax.numpy as jnp
from jax import lax
from jax.experimental import pallas as pl
from jax.experimental.pallas import tpu as pltpu

F32 = jnp.float32
BF16 = jnp.bfloat16

D_MODEL = 1024
MLA_HEADS = 4
QK_NOPE = 64
QK_ROPE = 32
QK_HEAD = QK_NOPE + QK_ROPE
V_HEAD = 128
Q_RANK = 384
KV_RANK = 256
ROPE_THETA = 10000.0
GLA_HEADS = 4
GLA_DK = 32
GLA_DV = 64
GLA_GATE_RANK = 16
GLA_TAU = 16.0
GLA_CHUNK = 64
GLA_GROUP = 256
POOL_WINDOWS = (2, 4, 8, 16)
POOL_GROUP = 64
POOL_WIDTH = POOL_GROUP * len(POOL_WINDOWS)
POOL_BUF = max(POOL_WINDOWS) - 1
MLA_OUT = MLA_HEADS * V_HEAD
GLA_OUT = GLA_HEADS * GLA_DV
N_MEM = 256
XA_HEADS = 4
XA_HEAD = D_MODEL // XA_HEADS
D_FF = 4 * D_MODEL
PAGE_SIZE = 128
EPS = 1e-6

HEAD_PAD = 128
QK_W = MLA_HEADS * HEAD_PAD
GLA_QK = GLA_HEADS * GLA_DK
IN_PAD = 1792
MISC0 = 1664

MIB = 1024 * 1024


def _cparams(sem, vmem_mib):
    return pltpu.CompilerParams(dimension_semantics=sem, vmem_limit_bytes=vmem_mib * MIB)


def _rms(x, g):
    y = x * lax.rsqrt(jnp.mean(x * x, axis=-1, keepdims=True) + EPS)
    return y * g


def _head_rms(x, g, nh, width, count):
    outs = []
    for h in range(nh):
        p = x[:, h * width:(h + 1) * width]
        ms = jnp.sum(p * p, axis=-1, keepdims=True) * (1.0 / count)
        outs.append(p * lax.rsqrt(ms + EPS) * g[:, h * width:(h + 1) * width])
    return jnp.concatenate(outs, axis=1)


def _dot(a, b):
    return jnp.dot(a, b, preferred_element_type=F32)


def _dot_nt(a, b):
    return lax.dot_general(a, b, (((1,), (1,)), ((), ())), preferred_element_type=F32)


def _split_bf16(x):
    hi = x.astype(BF16)
    lo = (x - hi.astype(F32)).astype(BF16)
    return hi, lo


def _dot_split(x, w):
    hi, lo = _split_bf16(x)
    return _dot(hi, w) + _dot(lo, w)


def _norm_rope(x, xrot, g, grot, c, s):
    outs = []
    for h in range(MLA_HEADS):
        sl = slice(h * HEAD_PAD, (h + 1) * HEAD_PAD)
        p = x[:, sl]
        r = lax.rsqrt(jnp.sum(p * p, axis=-1, keepdims=True) * (1.0 / QK_HEAD) + EPS)
        outs.append((p * r * g[:, sl]) * c + (xrot[:, sl] * r * grot[:, sl]) * s)
    return jnp.concatenate(outs, axis=1)


def _log_sigmoid(x):
    return -(jnp.maximum(-x, 0.0) + jnp.log1p(jnp.exp(-jnp.abs(x))))


def _sigmoid(x):
    return 1.0 / (1.0 + jnp.exp(-x))


def _memkv_kernel(mem_ref, g_ref, wk_ref, wv_ref, gk_ref, k_ref, v_ref, kb_ref, vb_ref):
    m = _rms(mem_ref[...], g_ref[...]).astype(BF16)
    k = _head_rms(_dot(m, wk_ref[...]), gk_ref[...], XA_HEADS, XA_HEAD, XA_HEAD)
    v = _dot(m, wv_ref[...])
    k_ref[...] = k.reshape(k_ref.shape)
    v_ref[...] = v.reshape(v_ref.shape)
    kb_ref[...] = k.astype(BF16)
    vb_ref[...] = v.astype(BF16)


def _memory_kv(mem2, g, wk, wv, gk):
    m = mem2.shape[0]
    depth = wk.shape[0]
    tm = 512
    nb = tm // N_MEM
    vec = lambda l, i: (l, 0, 0)
    return pl.pallas_call(
        _memkv_kernel,
        grid=(depth, m // tm),
        in_specs=[pl.BlockSpec((tm, D_MODEL), lambda l, i: (i, 0)), pl.BlockSpec((None, 1, D_MODEL), vec),
                  pl.BlockSpec((None, D_MODEL, D_MODEL), vec), pl.BlockSpec((None, D_MODEL, D_MODEL), vec),
                  pl.BlockSpec((None, 1, D_MODEL), vec)],
        out_specs=[pl.BlockSpec((nb, None, N_MEM, D_MODEL), lambda l, i: (i, l, 0, 0))] * 2
        + [pl.BlockSpec((None, tm, D_MODEL), lambda l, i: (l, i, 0))] * 2,
        out_shape=[jax.ShapeDtypeStruct((m // N_MEM, depth, N_MEM, D_MODEL), F32)] * 2
        + [jax.ShapeDtypeStruct((depth, m, D_MODEL), BF16)] * 2,
        compiler_params=_cparams(("parallel", "parallel"), 40),
        name="memory_kv",
    )(mem2, g, wk, wv, gk)


def _inproj_kernel(x_ref, gmix_ref, win_ref, gql_ref, wuq_ref, gkv_ref, wuk_ref, wuv_ref,
                   gq_ref, gk_ref, place_ref, wg_ref, bg_ref, wuqr_ref, gqr_ref, gkr_ref, placer_ref,
                   c_ref, s_ref, lat_ref, kr_ref, q_ref, k_ref, v_ref, gq_o, gk_o, gv_o, la_o, gr_o, u_o):
    h = _rms(x_ref[...], gmix_ref[...]).astype(BF16)
    z = _dot(h, win_ref[...])
    c_q = z[:, 0:384]
    c_kv = z[:, 384:640]
    misc = z[:, MISC0:MISC0 + 128]
    gq_o[...] = z[:, 640:768]
    gk_o[...] = z[:, 768:896]
    gv_o[...] = z[:, 896:1152]
    gr_o[...] = z[:, 1152:1408]
    u_o[...] = z[:, 1408:1664]
    kr_ref[...] = misc[:, 0:QK_ROPE]

    lat = _rms(c_kv, gkv_ref[...])
    lat_ref[...] = lat
    latb = lat.astype(BF16)
    c, s = c_ref[...], s_ref[...]

    cqn = _rms(c_q, gql_ref[...]).astype(BF16)
    q = _norm_rope(_dot(cqn, wuq_ref[...]), _dot(cqn, wuqr_ref[...]), gq_ref[...], gqr_ref[...], c, s)
    q_ref[...] = (q * (QK_HEAD ** -0.5)).astype(BF16)

    m_hi, m_lo = _split_bf16(misc)
    k = _dot(latb, wuk_ref[...]) + _dot(m_hi, place_ref[...]) + _dot(m_lo, place_ref[...])
    krot = _dot(m_hi, placer_ref[...]) + _dot(m_lo, placer_ref[...])
    k_ref[...] = _norm_rope(k, krot, gk_ref[...], gkr_ref[...], c, s).astype(BF16)
    v_ref[...] = _dot(latb, wuv_ref[...]).astype(BF16)

    gate = _dot(misc.astype(BF16), wg_ref[...]) + bg_ref[...]
    la_o[...] = _log_sigmoid(gate) * (1.0 / GLA_TAU)


def _in_proj(x2, lw, ropes, tm):
    m = x2.shape[0]
    row = lambda i: (i, 0)
    fix = lambda i: (0, 0)
    c, s = ropes
    if c.shape[0] == m:
        rope_map = row
    else:
        nb = c.shape[0] // tm
        rope_map = lambda i: (i % nb, 0)
    weights = [lw["norm_mix"], lw["w_in"], lw["g_q_lat"], lw["w_uq"], lw["g_kv_lat"], lw["w_uk"],
               lw["w_uv"], lw["g_qk_q"], lw["g_qk_k"], lw["place"], lw["w_gate"], lw["b_gate"],
               lw["w_uq_rot"], lw["g_q_rot"], lw["g_k_rot"], lw["place_rot"]]
    in_specs = [pl.BlockSpec((tm, D_MODEL), row)]
    in_specs += [pl.BlockSpec(w.shape, fix) for w in weights]
    in_specs += [pl.BlockSpec((tm, HEAD_PAD), rope_map)] * 2
    widths = [(KV_RANK, F32), (QK_ROPE, F32), (QK_W, BF16), (QK_W, BF16), (MLA_OUT, BF16),
              (GLA_QK, F32), (GLA_QK, F32), (GLA_OUT, F32), (GLA_QK, F32), (GLA_OUT, F32),
              (POOL_WIDTH, F32)]
    return pl.pallas_call(
        _inproj_kernel,
        grid=(m // tm,),
        in_specs=in_specs,
        out_specs=[pl.BlockSpec((tm, w), row) for w, _ in widths],
        out_shape=[jax.ShapeDtypeStruct((m, w), dt) for w, dt in widths],
        compiler_params=_cparams(("parallel",), 48),
        name="in_proj",
    )(x2, *weights, c, s)


def _flash_kernel(q_ref, k_ref, v_ref, o_ref, *, tq):
    qi = pl.program_id(2)
    q = q_ref[...]

    def block(j, carry, masked):
        m, l, acc = carry
        start = pl.multiple_of(j * tq, tq)
        k = k_ref[pl.ds(start, tq), :]
        v = v_ref[pl.ds(start, tq), :]
        s = _dot_nt(q, k)
        if masked:
            rows = lax.broadcasted_iota(jnp.int32, (tq, tq), 0)
            cols = lax.broadcasted_iota(jnp.int32, (tq, tq), 1)
            s = jnp.where(cols <= rows, s, -jnp.inf)
        m_new = jnp.maximum(m, jnp.max(s, axis=-1, keepdims=True))
        alpha = jnp.exp(m - m_new)
        p = jnp.exp(s - m_new)
        l = alpha * l + jnp.sum(p, axis=-1, keepdims=True)
        acc = alpha * acc + _dot(p.astype(BF16), v)
        return m_new, l, acc

    init = (jnp.full((tq, 1), -jnp.inf, F32), jnp.zeros((tq, 1), F32), jnp.zeros((tq, V_HEAD), F32))
    carry = lax.fori_loop(0, qi, lambda j, cr: block(j, cr, False), init)
    _, l, acc = block(qi, carry, True)
    o_ref[...] = (acc / l).astype(o_ref.dtype)


def _mla_prompt(q3, k3, v3, tq):
    b, t, _ = q3.shape
    qmap = lambda bi, h, qi: (bi, qi, h)
    kmap = lambda bi, h, qi: (bi, 0, h)
    return pl.pallas_call(
        functools.partial(_flash_kernel, tq=tq),
        grid=(b, MLA_HEADS, t // tq),
        in_specs=[pl.BlockSpec((None, tq, HEAD_PAD), qmap), pl.BlockSpec((None, t, HEAD_PAD), kmap),
                  pl.BlockSpec((None, t, V_HEAD), kmap)],
        out_specs=pl.BlockSpec((None, tq, V_HEAD), qmap),
        out_shape=jax.ShapeDtypeStruct((b, t, MLA_OUT), BF16),
        compiler_params=_cparams(("parallel", "parallel", "arbitrary"), 40),
        name="mla_prompt",
    )(q3, k3, v3)


def _gla_kernel(q_ref, k_ref, v_ref, la_ref, gr_ref, tri_ref, ones_ref, e64_ref, gout_ref, o_ref, st_ref, *, t):
    c = GLA_CHUNK
    gt = GLA_GROUP
    tri = tri_ref[...]
    ones_bd = ones_ref[...]
    e64 = e64_ref[...]
    gout = gout_ref[...]
    lane_qk = lax.broadcasted_iota(jnp.int32, (gt, GLA_QK), 1) // GLA_DK
    lane_v = lax.broadcasted_iota(jnp.int32, (gt, GLA_OUT), 1) // GLA_DV
    row_chunk = lax.broadcasted_iota(jnp.int32, (gt, GLA_QK), 0) // c
    rr = lax.broadcasted_iota(jnp.int32, (gt, gt), 0)
    cc = lax.broadcasted_iota(jnp.int32, (gt, gt), 1)
    causal = (cc <= rr) & (rr // c == cc // c)
    bd = (lax.broadcasted_iota(jnp.int32, (GLA_OUT, GLA_QK), 0) // GLA_DV
          == lax.broadcasted_iota(jnp.int32, (GLA_OUT, GLA_QK), 1) // GLA_DK)

    def group(gi, st):
        r0 = pl.multiple_of(gi * gt, gt)
        q = q_ref[pl.ds(r0, gt), :] * (GLA_DK ** -0.5)
        k = k_ref[pl.ds(r0, gt), :]
        v = v_ref[pl.ds(r0, gt), :]
        a_hi, a_lo = _split_bf16(la_ref[pl.ds(r0, gt), :])
        b = _dot(tri, a_hi) + _dot(tri, a_lo)
        tot = _dot(ones_bd, a_hi) + _dot(ones_bd, a_lo)
        qd = q * jnp.exp(b)
        qdb = qd.astype(BF16)
        kd = (k * jnp.exp(-b)).astype(BF16)
        k2 = k * jnp.exp(tot - b)
        decay = jnp.exp(tot)
        vb = v.astype(BF16)
        vt = v.T.astype(BF16)
        o = jnp.zeros((gt, GLA_OUT), F32)
        for h in range(GLA_HEADS):
            qh = jnp.where(lane_qk == h, qd, 0.0).astype(BF16)
            att = jnp.where(causal, _dot_nt(qh, kd), 0.0)
            o = o + jnp.where(lane_v == h, _dot(att.astype(BF16), vb), 0.0)
        inter = []
        for ci in range(gt // c):
            inter.append(_dot_nt(qdb[ci * c:(ci + 1) * c, :], st.astype(BF16)))
            k2c = jnp.where(row_chunk == ci, k2, 0.0).astype(BF16)
            st = st * decay[ci * c:ci * c + 1, :] + jnp.where(bd, _dot(vt, k2c), 0.0)
        o = o + jnp.concatenate(inter, axis=0)
        ms = _dot_split(o * o, e64) * (1.0 / GLA_DV)
        on = o * lax.rsqrt(ms + EPS) * gout
        gr = gr_ref[pl.ds(r0, gt), :]
        o_ref[pl.ds(r0, gt), :] = (on * (gr * _sigmoid(gr))).astype(o_ref.dtype)
        return st

    st_ref[...] = lax.fori_loop(0, t // gt, group, jnp.zeros((GLA_OUT, GLA_QK), F32), unroll=2)


def _gla_prompt(gq3, gk3, gv3, la3, gr3, lw):
    b, t, _ = gq3.shape
    bmap = lambda i: (i, 0, 0)
    fix = lambda i: (0, 0)
    chunk_id = jnp.arange(GLA_GROUP) // GLA_CHUNK
    same = chunk_id[:, None] == chunk_id[None, :]
    tri = (same & (jnp.arange(GLA_GROUP)[None, :] <= jnp.arange(GLA_GROUP)[:, None])).astype(BF16)
    return pl.pallas_call(
        functools.partial(_gla_kernel, t=t),
        grid=(b,),
        in_specs=[pl.BlockSpec((None, t, GLA_QK), bmap), pl.BlockSpec((None, t, GLA_QK), bmap),
                  pl.BlockSpec((None, t, GLA_OUT), bmap), pl.BlockSpec((None, t, GLA_QK), bmap),
                  pl.BlockSpec((None, t, GLA_OUT), bmap), pl.BlockSpec((GLA_GROUP, GLA_GROUP), fix),
                  pl.BlockSpec((GLA_GROUP, GLA_GROUP), fix),
                  pl.BlockSpec((GLA_OUT, GLA_OUT), fix), pl.BlockSpec((1, GLA_OUT), fix)],
        out_specs=[pl.BlockSpec((None, t, GLA_OUT), bmap), pl.BlockSpec((None, GLA_OUT, GLA_QK), bmap)],
        out_shape=[jax.ShapeDtypeStruct((b, t, GLA_OUT), BF16),
                   jax.ShapeDtypeStruct((b, GLA_OUT, GLA_QK), F32)],
        compiler_params=_cparams(("parallel",), 40),
        name="gla_prompt",
    )(gq3, gk3, gv3, la3, gr3, tri, same.astype(BF16), lw["e64"], lw["g_gla_out"])


def _pool_select(parts):
    lane = lax.broadcasted_iota(jnp.int32, parts[0].shape, 1) // POOL_GROUP
    out = parts[-1]
    for g in range(len(parts) - 2, -1, -1):
        out = jnp.where(lane == g, parts[g], out)
    return out


def _pool_kernel(u_ref, halo_ref, w_ref, sc_ref, o_ref, *, tp):
    i = pl.program_id(1)
    u = u_ref[...]
    halo = jnp.where(i == 0, 0.0, halo_ref[...])
    ext = jnp.concatenate([halo, u], axis=0)
    hb = halo.shape[0]
    pos = i * tp + lax.broadcasted_iota(jnp.int32, (tp, 1), 0)
    sums = ext
    parts = []
    span = 1
    for w in POOL_WINDOWS:
        while span < w:
            sums = sums + pltpu.roll(sums, span, 0)
            span *= 2
        cnt = jnp.minimum(w, pos + 1).astype(F32)
        parts.append(sums[hb:, :] / cnt - u)
    pooled = _pool_select(parts)
    o_ref[...] = (_dot(pooled.astype(BF16), w_ref[...]) * sc_ref[...]).astype(o_ref.dtype)


def _pool_prompt(u3, lw, tp):
    b, t, _ = u3.shape
    hb = 16
    return pl.pallas_call(
        functools.partial(_pool_kernel, tp=tp),
        grid=(b, t // tp),
        in_specs=[pl.BlockSpec((None, tp, POOL_WIDTH), lambda bi, i: (bi, i, 0)),
                  pl.BlockSpec((None, hb, POOL_WIDTH),
                               lambda bi, i: (bi, jnp.maximum(i * (tp // hb) - 1, 0), 0)),
                  pl.BlockSpec((POOL_WIDTH, POOL_WIDTH), lambda bi, i: (0, 0)),
                  pl.BlockSpec((1, POOL_WIDTH), lambda bi, i: (0, 0))],
        out_specs=pl.BlockSpec((None, tp, POOL_WIDTH), lambda bi, i: (bi, i, 0)),
        out_shape=jax.ShapeDtypeStruct((b, t, POOL_WIDTH), BF16),
        compiler_params=_cparams(("parallel", "parallel"), 32),
        name="pool_prompt",
    )(u3, u3, lw["w_pool"], lw["pool_scale"])


def _mix_out(x, mla, gla, pool, wo_ref):
    return (x + _dot(mla.astype(BF16), wo_ref[0:MLA_OUT, :])
            + _dot(gla.astype(BF16), wo_ref[MLA_OUT:MLA_OUT + GLA_OUT, :])
            + _dot(pool.astype(BF16), wo_ref[MLA_OUT + GLA_OUT:, :]))


def _xa_query(x1, gxa, wxq, gxq):
    q = _dot(_rms(x1, gxa).astype(BF16), wxq)
    return _head_rms(q, gxq, XA_HEADS, XA_HEAD, XA_HEAD) * (XA_HEAD ** -0.5)


def _xa_attend(qb, kb, vb):
    outs = []
    for h in range(XA_HEADS):
        sl = slice(h * XA_HEAD, (h + 1) * XA_HEAD)
        s = _dot_nt(qb[:, sl], kb[:, sl])
        e = jnp.exp(s - jnp.max(s, axis=-1, keepdims=True))
        p = e / jnp.sum(e, axis=-1, keepdims=True)
        outs.append(_dot(p.astype(BF16), vb[:, sl]))
    return jnp.concatenate(outs, axis=1).astype(BF16)


def _post_kernel(x_ref, mla_ref, gla_ref, pool_ref, wo_ref, gxa_ref, wxq_ref, gxq_ref, mk_ref, mv_ref,
                 wxo_ref, o_ref):
    x1 = _mix_out(x_ref[...], mla_ref[...], gla_ref[...], pool_ref[...], wo_ref)
    qb = _xa_query(x1, gxa_ref[...], wxq_ref[...], gxq_ref[...]).astype(BF16)
    o = _xa_attend(qb, mk_ref[...], mv_ref[...])
    o_ref[...] = x1 + _dot(o, wxo_ref[...])


def _post_prompt(x2, mla, gla, pool, mk4, mv4, layer, lw, t, tm):
    m = x2.shape[0]
    per_seq = t // tm
    row = lambda i: (i, 0)
    fix = lambda i: (0, 0)
    mem = lambda i: (layer, i // per_seq, 0, 0)
    sq = (D_MODEL, D_MODEL)
    return pl.pallas_call(
        _post_kernel,
        grid=(m // tm,),
        in_specs=[pl.BlockSpec((tm, D_MODEL), row), pl.BlockSpec((tm, MLA_OUT), row),
                  pl.BlockSpec((tm, GLA_OUT), row), pl.BlockSpec((tm, POOL_WIDTH), row),
                  pl.BlockSpec(sq, fix), pl.BlockSpec((1, D_MODEL), fix), pl.BlockSpec(sq, fix),
                  pl.BlockSpec((1, D_MODEL), fix), pl.BlockSpec((None, None, N_MEM, D_MODEL), mem),
                  pl.BlockSpec((None, None, N_MEM, D_MODEL), mem), pl.BlockSpec(sq, fix)],
        out_specs=pl.BlockSpec((tm, D_MODEL), row),
        out_shape=jax.ShapeDtypeStruct((m, D_MODEL), F32),
        compiler_params=_cparams(("parallel",), 48),
        name="post_prompt",
    )(x2, mla, gla, pool, lw["w_out"], lw["norm_xa"], lw["w_xq"], lw["g_xq"], mk4, mv4, lw["w_xo"])


def _mlp_kernel(x_ref, g_ref, w1_ref, w2_ref, o_ref):
    x = x_ref[...]
    h = _rms(x, g_ref[...]).astype(BF16)
    a = jnp.maximum(_dot(h, w1_ref[...]), 0.0)
    o_ref[...] = x + _dot((a * a).astype(BF16), w2_ref[...])


def _mlp(x2, lw, tm):
    m = x2.shape[0]
    once = pl.Buffered(1)
    layer = lw["layer"]
    return pl.pallas_call(
        _mlp_kernel,
        grid=(m // tm,),
        in_specs=[pl.BlockSpec((tm, D_MODEL), lambda i: (i, 0)),
                  pl.BlockSpec((1, D_MODEL), lambda i: (0, 0)),
                  pl.BlockSpec((None, D_MODEL, D_FF), lambda i: (layer, 0, 0), pipeline_mode=once),
                  pl.BlockSpec((None, D_FF, D_MODEL), lambda i: (layer, 0, 0), pipeline_mode=once)],
        out_specs=pl.BlockSpec((tm, D_MODEL), lambda i: (i, 0)),
        out_shape=jax.ShapeDtypeStruct((m, D_MODEL), F32),
        compiler_params=_cparams(("parallel",), 56),
        name="mlp",
    )(x2, lw["norm_mlp"], lw["w_ff1"], lw["w_ff2"])


DEC_PAGES = 64
DEC_CHUNK = 8192
DEC_LHS_ROWS = MLA_HEADS * QK_NOPE + 16
DEC_SS_ROWS = 48


def _mla_decode_kernel(pt_ref, q_ref, k_ref, latn_ref, cos_ref, sin_ref, wukt_ref, gkn_ref, gcol_ref,
                       ssel_ref, wuv_ref, lat_hbm, krt_hbm, o_ref,
                       lat_buf, kr_buf, sem, m_sc, l_sc, acc_sc, lhs_sc, lb_sc, *, pp, nj, layer):
    t = pl.program_id(0)
    n_steps = pl.num_programs(0)
    j = t % nj
    slot = t % 2

    def page_copies(step, buf_slot):
        b = step // nj
        base = (step % nj) * pp
        copies = []
        for p in range(pp):
            page = pt_ref[b, base + p]
            copies.append(pltpu.make_async_copy(lat_hbm.at[page, layer], lat_buf.at[buf_slot, p],
                                                sem.at[buf_slot, 0]))
            copies.append(pltpu.make_async_copy(krt_hbm.at[page, layer],
                                                kr_buf.at[buf_slot, :, pl.ds(p * PAGE_SIZE, PAGE_SIZE)],
                                                sem.at[buf_slot, 1]))
        return copies

    def start_all(copies):
        for i, cp in enumerate(copies):
            cp.start(priority=(i // 2 + i) % 2)

    start_all(page_copies(jnp.minimum(t + 1, n_steps - 1), 1 - slot))

    @pl.when(t == 0)
    def _():
        start_all(page_copies(t, slot))
        lhs_sc[0:MLA_HEADS * QK_NOPE, :] = wukt_ref[...]

    q16 = q_ref[...].astype(F32)

    @pl.when(j == 0)
    def _():
        m_sc[...] = jnp.full(m_sc.shape, -jnp.inf, F32)
        l_sc[...] = jnp.zeros(l_sc.shape, F32)
        acc_sc[...] = jnp.zeros(acc_sc.shape, F32)
        qn = (q16 * gkn_ref[...])[:, 0:QK_NOPE]
        tiled = jnp.concatenate([qn] * MLA_HEADS, axis=1)
        row = lax.broadcasted_iota(jnp.int32, tiled.shape, 0)
        lane = lax.broadcasted_iota(jnp.int32, tiled.shape, 1) // QK_NOPE
        qbd = jnp.where(row == lane, tiled, 0.0).astype(BF16)
        lhs_sc[MLA_HEADS * QK_NOPE:, :] = _dot(qbd, wukt_ref[...]).astype(BF16)

    for cp in page_copies(t, slot):
        cp.wait()

    qr = q16[:, QK_NOPE:QK_HEAD].astype(BF16)
    lhs = lhs_sc[...]
    gcol = gcol_ref[...]
    ssel = ssel_ref[...]
    half = QK_ROPE // 2
    nf = MLA_HEADS * QK_NOPE
    for p in range(pp):
        lb_sc[p * PAGE_SIZE:(p + 1) * PAGE_SIZE, :] = lat_buf[slot, p].astype(BF16)

    ck = DEC_CHUNK
    scores = []
    for c0 in range(0, pp * PAGE_SIZE, ck):
        res = _dot_nt(lhs, lb_sc[c0:c0 + ck, :])
        sq = res[0:nf, :] * res[0:nf, :]
        sn = res[nf:nf + 8, :]
        krt = kr_buf[slot, :, c0:c0 + ck]
        parts = [sq[h * QK_NOPE:(h + 1) * QK_NOPE, :].reshape(8, 8, ck).sum(axis=0) for h in range(MLA_HEADS)]
        parts.append((krt * krt).reshape(4, 8, ck).sum(axis=0))
        parts.append(jnp.zeros((8, ck), F32))
        p_hi, p_lo = _split_bf16(jnp.concatenate(parts, axis=0))
        ss = _dot(ssel, p_hi) + _dot(ssel, p_lo)
        krg = krt * gcol
        cs = cos_ref[:, c0:c0 + ck]
        sn_ = sin_ref[:, c0:c0 + ck]
        x1, x2 = krg[0:half, :], krg[half:, :]
        roped = jnp.concatenate([x1 * cs - x2 * sn_, x2 * cs + x1 * sn_], axis=0)
        sr = _dot(qr, roped.astype(BF16))[0:8, :]
        scores.append((sn + sr) * lax.rsqrt(ss * (1.0 / QK_HEAD) + EPS))
    s_all = jnp.concatenate(scores, axis=1)

    m_old = m_sc[...]
    m_new = jnp.maximum(m_old, jnp.max(s_all, axis=1, keepdims=True))
    alpha = jnp.exp(m_old - m_new)
    p = jnp.exp(s_all - m_new[:, 0:1])
    l_sc[...] = alpha * l_sc[...] + jnp.sum(p, axis=1, keepdims=True)
    acc_sc[...] = alpha[:, 0:1] * acc_sc[...] + _dot(p.astype(BF16), lb_sc[...])
    m_sc[...] = m_new

    @pl.when(t == n_steps - 1)
    def _():
        for cp in page_copies(t, 1 - slot):
            cp.wait()

    @pl.when(j == nj - 1)
    def _():
        s_self = jnp.sum((q16 * k_ref[...].astype(F32))[0:8, :], axis=1, keepdims=True)
        m_o = m_sc[...]
        m_n = jnp.maximum(m_o, s_self)
        al = jnp.exp(m_o - m_n)
        ps = jnp.exp(s_self - m_n)
        l = al * l_sc[...] + ps
        acc = al[:, 0:1] * acc_sc[...] + ps[:, 0:1] * latn_ref[...]
        o_lat = acc / l[:, 0:1]
        res = _dot(o_lat.astype(BF16), wuv_ref[...])
        row = lax.broadcasted_iota(jnp.int32, res.shape, 0)
        lane = lax.broadcasted_iota(jnp.int32, res.shape, 1) // V_HEAD
        o_ref[...] = jnp.sum(jnp.where(row == lane, res, 0.0), axis=0, keepdims=True).astype(o_ref.dtype)


def _mla_decode(page_table, q16, k16, lat_new3, cache_lat, cache_krt, layer, lw, tabs):
    pp = DEC_PAGES
    n = q16.shape[0]
    n_pages = page_table.shape[1]
    nj = n_pages // pp
    cos_t, sin_t = tabs
    fix2 = lambda t, pt: (0, 0)
    seq3 = lambda t, pt: (t // nj, 0, 0)
    tab = lambda t, pt: (0, t % nj)
    weights = [lw["w_uk_t"], lw["g_k_nope"], lw["g_k_rope_col"], lw["ss_select"], lw["w_uv"]]
    in_specs = [pl.BlockSpec((None, 16, HEAD_PAD), seq3), pl.BlockSpec((None, 16, HEAD_PAD), seq3),
                pl.BlockSpec((None, 1, KV_RANK), seq3),
                pl.BlockSpec((QK_ROPE // 2, pp * PAGE_SIZE), tab),
                pl.BlockSpec((QK_ROPE // 2, pp * PAGE_SIZE), tab)]
    in_specs += [pl.BlockSpec(w.shape, fix2) for w in weights]
    in_specs += [pl.BlockSpec(memory_space=pl.ANY), pl.BlockSpec(memory_space=pl.ANY)]
    grid_spec = pltpu.PrefetchScalarGridSpec(
        num_scalar_prefetch=1,
        grid=(n * nj,),
        in_specs=in_specs,
        out_specs=pl.BlockSpec((None, 1, MLA_OUT), seq3),
        scratch_shapes=[pltpu.VMEM((2, pp, PAGE_SIZE, KV_RANK), F32),
                        pltpu.VMEM((2, QK_ROPE, pp * PAGE_SIZE), F32),
                        pltpu.SemaphoreType.DMA((2, 2)),
                        pltpu.VMEM((8, 128), F32), pltpu.VMEM((8, 128), F32), pltpu.VMEM((8, KV_RANK), F32),
                        pltpu.VMEM((DEC_LHS_ROWS, KV_RANK), BF16),
                        pltpu.VMEM((pp * PAGE_SIZE, KV_RANK), BF16)])
    return pl.pallas_call(
        functools.partial(_mla_decode_kernel, pp=pp, nj=nj, layer=layer),
        grid_spec=grid_spec,
        out_shape=jax.ShapeDtypeStruct((n, 1, MLA_OUT), F32),
        compiler_params=_cparams(("arbitrary",), 52),
        name="mla_decode",
    )(page_table, q16, k16, lat_new3, cos_t, sin_t, *weights, cache_lat, cache_krt)


def _dec_mix_kernel(q_ref, k_ref, la_ref, v_ref, gr_ref, u_ref, s_ref, pb_ref, gout_ref, wp_ref, sc_ref,
                    s_o, gla_o, pool_o, pb_o, *, past):
    a = jnp.exp(la_ref[...])
    v4 = v_ref[...]
    vexp = jnp.concatenate([jnp.broadcast_to(v4[h:h + 1, :], (GLA_DK, GLA_DV)) for h in range(GLA_HEADS)],
                           axis=0)
    s_new = a * s_ref[...] + k_ref[...] * vexp
    s_o[...] = s_new
    prod = (q_ref[...] * (GLA_DK ** -0.5)) * s_new
    gr = gr_ref[...]
    gout = gout_ref[...]
    for h in range(GLA_HEADS):
        o = jnp.sum(prod[h * GLA_DK:(h + 1) * GLA_DK, :], axis=0, keepdims=True)
        on = _rms(o, gout)
        g = gr[h:h + 1, :]
        gla_o[h:h + 1, :] = (on * (g * _sigmoid(g))).astype(gla_o.dtype)

    st = pb_ref[...]
    u = u_ref[...]
    ridx = lax.broadcasted_iota(jnp.int32, st.shape, 0)
    parts = []
    for w in POOL_WINDOWS:
        tot = u + jnp.sum(jnp.where(ridx >= POOL_BUF + 1 - w, st, 0.0), axis=0, keepdims=True)
        parts.append(tot / float(min(w, past + 1)) - u)
    pooled = jnp.broadcast_to(_pool_select(parts), (8, POOL_WIDTH)).astype(BF16)
    y = _dot(pooled, wp_ref[...])[0:1, :] * sc_ref[...]
    pool_o[...] = y.astype(pool_o.dtype)
    pb_o[0:POOL_BUF - 1, :] = st[1:POOL_BUF, :]
    pb_o[POOL_BUF - 1:POOL_BUF, :] = u


def _dec_mix(gq, gk, la, gv, gr, u, state_gla4, state_pool, layer, lw, past):
    n = gq.shape[0]
    col = lambda x: x.reshape(n, GLA_QK, 1)
    b3 = lambda b: (b, 0, 0)
    fix = lambda b: (0, 0)
    return pl.pallas_call(
        functools.partial(_dec_mix_kernel, past=past),
        grid=(n,),
        in_specs=[pl.BlockSpec((None, GLA_QK, 1), b3)] * 3
        + [pl.BlockSpec((None, GLA_HEADS, GLA_DV), b3)] * 2
        + [pl.BlockSpec((None, 1, POOL_WIDTH), b3),
           pl.BlockSpec((None, None, GLA_QK, GLA_DV), lambda b: (b, layer, 0, 0)),
           pl.BlockSpec((None, None, POOL_BUF, POOL_WIDTH), lambda b: (b, layer, 0, 0)),
           pl.BlockSpec((1, GLA_DV), fix), pl.BlockSpec((POOL_WIDTH, POOL_WIDTH), fix),
           pl.BlockSpec((1, POOL_WIDTH), fix)],
        out_specs=[pl.BlockSpec((None, GLA_QK, GLA_DV), b3), pl.BlockSpec((None, GLA_HEADS, GLA_DV), b3),
                   pl.BlockSpec((None, 1, POOL_WIDTH), b3), pl.BlockSpec((None, POOL_BUF, POOL_WIDTH), b3)],
        out_shape=[jax.ShapeDtypeStruct((n, GLA_QK, GLA_DV), F32),
                   jax.ShapeDtypeStruct((n, GLA_HEADS, GLA_DV), F32),
                   jax.ShapeDtypeStruct((n, 1, POOL_WIDTH), F32),
                   jax.ShapeDtypeStruct((n, POOL_BUF, POOL_WIDTH), F32)],
        compiler_params=_cparams(("parallel",), 32),
        name="dec_mix",
    )(col(gq), col(gk), col(la), gv.reshape(n, GLA_HEADS, GLA_DV), gr.reshape(n, GLA_HEADS, GLA_DV),
      u.reshape(n, 1, POOL_WIDTH), state_gla4, state_pool, lw["g_gla_out64"], lw["w_pool"], lw["pool_scale"])


def _post_a_kernel(x_ref, mla_ref, gla_ref, pool_ref, wo_ref, gxa_ref, wxq_ref, gxq_ref, x1_ref, q_ref):
    x1 = _mix_out(x_ref[...], mla_ref[...], gla_ref[...], pool_ref[...], wo_ref)
    x1_ref[...] = x1
    q_ref[...] = _xa_query(x1, gxa_ref[...], wxq_ref[...], gxq_ref[...])


def _post_a(x2, mla, gla, pool, lw):
    m = x2.shape[0]
    return pl.pallas_call(
        _post_a_kernel,
        out_shape=[jax.ShapeDtypeStruct((m, D_MODEL), F32), jax.ShapeDtypeStruct((m, D_MODEL), F32)],
        compiler_params=pltpu.CompilerParams(vmem_limit_bytes=32 * MIB),
        name="post_a",
    )(x2, mla, gla, pool, lw["w_out"], lw["norm_xa"], lw["w_xq"], lw["g_xq"])


XA_HALVES = XA_HEAD // 128
XA_ROWS = XA_HALVES * XA_HEADS


def _xa_decode_kernel(q_ref, mk_ref, mv_ref, o_ref):
    k3 = mk_ref[...].reshape(N_MEM, XA_ROWS, 128)
    v3 = mv_ref[...].reshape(N_MEM, XA_ROWS, 128)
    s = jnp.sum(k3 * q_ref[...][None], axis=-1, keepdims=True)
    s = s[:, 0:XA_HEADS] + s[:, XA_HEADS:XA_ROWS]
    s = jnp.concatenate([s, s], axis=1)
    e = jnp.exp(s - jnp.max(s, axis=0, keepdims=True))
    p = e / jnp.sum(e, axis=0, keepdims=True)
    o_ref[...] = jnp.sum(p * v3, axis=0)


def _xa_decode(q3, cache_k, cache_v, layer):
    n = q3.shape[0]
    depth = cache_k.shape[1]
    view = lambda c: c.reshape(n, depth, N_MEM, XA_HEADS, XA_HALVES, 128).transpose(0, 1, 2, 4, 3, 5).reshape(
        n, depth, N_MEM * XA_ROWS, 128)
    q8 = q3.reshape(n, XA_HEADS, XA_HALVES, 128).transpose(0, 2, 1, 3).reshape(n, XA_ROWS, 128)
    b3 = lambda b: (b, 0, 0)
    mem = lambda b: (b, layer, 0, 0)
    o = pl.pallas_call(
        _xa_decode_kernel,
        grid=(n,),
        in_specs=[pl.BlockSpec((None, XA_ROWS, 128), b3),
                  pl.BlockSpec((None, None, N_MEM * XA_ROWS, 128), mem),
                  pl.BlockSpec((None, None, N_MEM * XA_ROWS, 128), mem)],
        out_specs=pl.BlockSpec((None, XA_ROWS, 128), b3),
        out_shape=jax.ShapeDtypeStruct((n, XA_ROWS, 128), F32),
        compiler_params=_cparams(("parallel",), 32),
        name="xa_decode",
    )(q8, view(cache_k), view(cache_v))
    return o.reshape(n, XA_HALVES, XA_HEADS, 128).transpose(0, 2, 1, 3).reshape(n, 1, D_MODEL)


def _post_b_kernel(x1_ref, o_ref, wxo_ref, x2_ref):
    x2_ref[...] = x1_ref[...] + _dot(o_ref[...].astype(BF16), wxo_ref[...])


def _post_b(x1, o, lw):
    return pl.pallas_call(
        _post_b_kernel,
        out_shape=jax.ShapeDtypeStruct(x1.shape, F32),
        compiler_params=pltpu.CompilerParams(vmem_limit_bytes=32 * MIB),
        name="post_b",
    )(x1, o, lw["w_xo"])


def _rope_tables_rows(pos):
    half = QK_ROPE // 2
    inv = ROPE_THETA ** (-jnp.arange(half, dtype=F32) / half)
    ang = pos.astype(F32)[:, None] * inv[None, :]
    cos, sin = jnp.cos(ang), jnp.sin(ang)
    n = pos.shape[0]
    c = jnp.concatenate([jnp.ones((n, QK_NOPE), F32), cos, cos, jnp.ones((n, HEAD_PAD - QK_HEAD), F32)], axis=1)
    s = jnp.concatenate([jnp.zeros((n, QK_NOPE), F32), sin, sin, jnp.zeros((n, HEAD_PAD - QK_HEAD), F32)],
                        axis=1)
    return c, s


def _rope_tables_cols(pos):
    half = QK_ROPE // 2
    inv = ROPE_THETA ** (-jnp.arange(half, dtype=F32) / half)
    ang = pos.astype(F32)[:, None] * inv[None, :]
    return jnp.cos(ang).T, jnp.sin(ang).T


def _pad_heads(w, width):
    pad = [(0, 0)] * (w.ndim - 1) + [(0, HEAD_PAD - width)]
    w = jnp.pad(w, pad)
    return w.reshape(w.shape[:-2] + (w.shape[-2] * HEAD_PAD,))


def _layer_weights(l, p):
    w_in = p["w_in"][l]
    sizes = (Q_RANK, KV_RANK, QK_ROPE, GLA_QK, GLA_QK, GLA_OUT, GLA_GATE_RANK, GLA_OUT, POOL_WIDTH)
    offs = [0]
    for s in sizes:
        offs.append(offs[-1] + s)
    piece = lambda i: w_in[:, offs[i]:offs[i + 1]]
    w_in_p = jnp.concatenate([piece(0), piece(1), piece(3), piece(4), piece(5), piece(7), piece(8), piece(2),
                              piece(6), jnp.zeros((D_MODEL, IN_PAD - sum(sizes)), F32)], axis=1)
    g_q = _pad_heads(jnp.broadcast_to(p["g_qk_q"][l], (MLA_HEADS, QK_HEAD)), QK_HEAD)
    g_k = _pad_heads(jnp.broadcast_to(p["g_qk_k"][l], (MLA_HEADS, QK_HEAD)), QK_HEAD)
    place = jnp.zeros((128, QK_W), F32)
    j = jnp.arange(QK_ROPE)
    for h in range(MLA_HEADS):
        place = place.at[j, h * HEAD_PAD + QK_NOPE + j].set(1.0)
    w_gate = jnp.zeros((128, GLA_QK), F32).at[QK_ROPE:QK_ROPE + GLA_GATE_RANK].set(p["w_gate_up"][l])
    half = QK_ROPE // 2

    def partner(w):
        w4 = w.reshape(w.shape[:-1] + (MLA_HEADS, HEAD_PAD))
        x1 = w4[..., QK_NOPE:QK_NOPE + half]
        x2 = w4[..., QK_NOPE + half:QK_HEAD]
        out = jnp.zeros_like(w4).at[..., QK_NOPE:QK_NOPE + half].set(x2).at[..., QK_NOPE + half:QK_HEAD].set(x1)
        return out.reshape(w.shape)

    sign = jnp.tile(jnp.concatenate([jnp.zeros((QK_NOPE,), F32), -jnp.ones((half,), F32), jnp.ones((half,), F32),
                                     jnp.zeros((HEAD_PAD - QK_HEAD,), F32)]), MLA_HEADS)
    w_uq_pad = _pad_heads(p["w_uq"][l], QK_HEAD)
    wp = p["w_pool"][l]
    w_pool = jnp.zeros((POOL_WIDTH, POOL_WIDTH), F32)
    for g in range(len(POOL_WINDOWS)):
        w_pool = w_pool.at[g * POOL_GROUP:(g + 1) * POOL_GROUP, g * POOL_GROUP:(g + 1) * POOL_GROUP].set(wp[g])
    blk = jnp.arange(GLA_OUT) // GLA_DV
    row = lambda v: v.reshape(1, -1).astype(F32)
    bf = lambda v: v.astype(BF16)
    w_uk2 = p["w_uk"][l].reshape(KV_RANK, MLA_HEADS * QK_NOPE)
    g_k_nope = jnp.zeros((1, HEAD_PAD), F32).at[0, :QK_NOPE].set(p["g_qk_k"][l][:QK_NOPE])
    col = jnp.arange(DEC_SS_ROWS)[None, :] // 8
    hrow = jnp.arange(8)[:, None]
    ss_select = ((hrow < MLA_HEADS) & ((col == hrow) | (col == MLA_HEADS))).astype(F32)
    return {
        "norm_mix": row(p["norm_mix"][l]), "w_in": bf(w_in_p), "g_q_lat": row(p["g_q_lat"][l]),
        "w_uq": bf(_pad_heads(p["w_uq"][l], QK_HEAD)), "g_kv_lat": row(p["g_kv_lat"][l]),
        "w_uk": bf(_pad_heads(p["w_uk"][l], QK_NOPE)), "w_uv": bf(p["w_uv"][l].reshape(KV_RANK, MLA_OUT)),
        "g_qk_q": row(g_q), "g_qk_k": row(g_k), "place": bf(place), "w_gate": bf(w_gate),
        "w_uq_rot": bf(partner(w_uq_pad) * sign), "g_q_rot": row(partner(g_q)), "g_k_rot": row(partner(g_k)),
        "place_rot": bf(partner(place) * sign),
        "b_gate": row(p["b_gate"][l]),
        "e64": bf((blk[:, None] == blk[None, :]).astype(F32)),
        "g_gla_out": row(jnp.tile(p["g_gla_out"][l], GLA_HEADS)), "g_gla_out64": row(p["g_gla_out"][l]),
        "w_pool": bf(w_pool), "pool_scale": row(p["pool_scale"][l]),
        "w_out": bf(p["w_out"][l]), "norm_xa": row(p["norm_xa"][l]), "w_xq": bf(p["w_xq"][l]),
        "g_xq": row(jnp.tile(p["g_xq"][l], XA_HEADS)), "w_xo": bf(p["w_xo"][l]),
        "norm_mlp": row(p["norm_mlp"][l]), "w_ff1": p["w_ff1_bf16"], "w_ff2": p["w_ff2_bf16"], "layer": l,
        "w_uk_t": bf(w_uk2.T), "g_k_nope": g_k_nope,
        "g_k_rope_col": jnp.broadcast_to(p["g_qk_k"][l][QK_NOPE:, None], (QK_ROPE, DEC_CHUNK)).astype(F32),
        "ss_select": bf(ss_select),
    }


def _prompt_layer(x2, layer, mk4, mv4, lw, ropes, b, t):
    lat, kr, q, k, v, gq, gk, gv, la, gr, u = _in_proj(x2, lw, ropes, 512)
    r3 = lambda a: a.reshape(b, t, a.shape[-1])
    mla = _mla_prompt(r3(q), r3(k), r3(v), 512).reshape(b * t, MLA_OUT)
    gla, st = _gla_prompt(r3(gq), r3(gk), r3(gv), r3(la), r3(gr), lw)
    pool = _pool_prompt(r3(u), lw, t).reshape(b * t, POOL_WIDTH)
    x2 = _post_prompt(x2, mla, gla.reshape(b * t, GLA_OUT), pool, mk4, mv4, layer, lw, t, 512)
    x2 = _mlp(x2, lw, 512)
    gla_state = jnp.stack([st[:, h * GLA_DV:(h + 1) * GLA_DV, h * GLA_DK:(h + 1) * GLA_DK]
                           for h in range(GLA_HEADS)], axis=1).transpose(0, 1, 3, 2)
    outs = (lat.reshape(b, t, KV_RANK), kr.reshape(b, t, QK_ROPE), gla_state,
            r3(u)[:, t - POOL_BUF:])
    return x2, outs


def _sample_layer(x2, layer, lw, ropes, tabs, page_table, cache_lat, cache_krt, state_gla4, state_pool,
                  cache_mk, cache_mv, past):
    n = x2.shape[0]
    lat, kr, q, k, v, gq, gk, gv, la, gr, u = _in_proj(x2, lw, ropes, n)
    pad16 = lambda a: jnp.pad(a.reshape(n, MLA_HEADS, HEAD_PAD), ((0, 0), (0, 16 - MLA_HEADS), (0, 0)))
    mla = _mla_decode(page_table, pad16(q), pad16(k), lat.reshape(n, 1, KV_RANK), cache_lat, cache_krt, layer,
                      lw, tabs).reshape(n, MLA_OUT)
    s_new, gla, pool, pb = _dec_mix(gq, gk, la, gv, gr, u, state_gla4, state_pool, layer, lw, past)
    x1, qx = _post_a(x2, mla, gla.reshape(n, GLA_OUT), pool.reshape(n, POOL_WIDTH), lw)
    ox = _xa_decode(qx.reshape(n, 1, D_MODEL), cache_mk, cache_mv, layer).reshape(n, D_MODEL)
    x2 = _mlp(_post_b(x1, ox, lw), lw, n)
    outs = (lat.reshape(n, 1, KV_RANK), kr.reshape(n, 1, QK_ROPE),
            s_new.reshape(n, GLA_HEADS, GLA_DK, GLA_DV), pb)
    return x2, outs


def kernel(x_prompt, x_sample, mem_prompt, cache_mla_latent, cache_mla_krope, state_gla, state_pool,
           cache_mem_k, cache_mem_v, page_table, norm_mix, w_in, g_q_lat, w_uq, g_kv_lat, w_uk, w_uv,
           g_qk_q, g_qk_k, w_gate_up, b_gate, g_gla_out, w_pool, pool_scale, w_out, norm_xa, norm_mem,
           w_xq, w_xk, w_xv, g_xq, g_xk, w_xo, norm_mlp, w_ff1, w_ff2):
    params = dict(norm_mix=norm_mix, w_in=w_in, g_q_lat=g_q_lat, w_uq=w_uq, g_kv_lat=g_kv_lat, w_uk=w_uk,
                  w_uv=w_uv, g_qk_q=g_qk_q, g_qk_k=g_qk_k, w_gate_up=w_gate_up, b_gate=b_gate,
                  g_gla_out=g_gla_out, w_pool=w_pool, pool_scale=pool_scale, w_out=w_out, norm_xa=norm_xa,
                  norm_mem=norm_mem, w_xq=w_xq, w_xk=w_xk, w_xv=w_xv, g_xq=g_xq, g_xk=g_xk, w_xo=w_xo,
                  norm_mlp=norm_mlp, w_ff1_bf16=w_ff1.astype(BF16), w_ff2_bf16=w_ff2.astype(BF16))
    b, t, _ = x_prompt.shape
    n, dec_seq, _ = x_sample.shape
    depth = norm_mix.shape[0]
    n_pages = page_table.shape[1]
    past = n_pages * PAGE_SIZE

    ropes_p = _rope_tables_rows(jnp.arange(t, dtype=jnp.int32))
    ropes_s = _rope_tables_rows(jnp.full((n,), past, jnp.int32))
    tabs = _rope_tables_cols(jnp.arange(past, dtype=jnp.int32))

    yp = x_prompt.reshape(b * t, D_MODEL)
    ys = x_sample.reshape(n * dec_seq, D_MODEL)
    mem2 = mem_prompt.reshape(b * N_MEM, D_MODEL)
    state_gla4 = state_gla.reshape(n, depth, GLA_QK, GLA_DV)
    cache_krt = cache_mla_krope.transpose(0, 1, 3, 2)

    mk, mv, mkb, mvb = _memory_kv(mem2, norm_mem.reshape(depth, 1, D_MODEL), w_xk.astype(BF16),
                                  w_xv.astype(BF16), jnp.tile(g_xk, (1, XA_HEADS)).reshape(depth, 1, D_MODEL))
    mk4 = mkb.reshape(depth, b, N_MEM, D_MODEL)
    mv4 = mvb.reshape(depth, b, N_MEM, D_MODEL)

    p_outs, s_outs = [], []
    for l in range(depth):
        lw = _layer_weights(l, params)
        yp, po = _prompt_layer(yp, l, mk4, mv4, lw, ropes_p, b, t)
        ys, so = _sample_layer(ys, l, lw, ropes_s, tabs, page_table, cache_mla_latent, cache_krt,
                               state_gla4, state_pool, cache_mem_k, cache_mem_v, past)
        p_outs.append(po)
        s_outs.append(so)

    stack = lambda outs, i: jnp.stack([o[i] for o in outs], axis=1)
    return (yp.reshape(b, t, D_MODEL), ys.reshape(n, dec_seq, D_MODEL),
            stack(p_outs, 0), stack(p_outs, 1), stack(p_outs, 2), stack(p_outs, 3),
            mk.reshape(b, depth, N_MEM, XA_HEADS, XA_HEAD), mv.reshape(b, depth, N_MEM, XA_HEADS, XA_HEAD),
            stack(s_outs, 0), stack(s_outs, 1), stack(s_outs, 2), stack(s_outs, 3))
```

```python
import functools

import jax
import jax.numpy as jnp
from jax import lax
from jax.experimental import pallas as pl
from jax.experimental.pallas import tpu as pltpu

F32 = jnp.float32
BF16 = jnp.bfloat16

D_MODEL = 1024
MLA_HEADS = 4
QK_NOPE = 64
QK_ROPE = 32
QK_HEAD = QK_NOPE + QK_ROPE
V_HEAD = 128
Q_RANK = 384
KV_RANK = 256
ROPE_THETA = 10000.0
GLA_HEADS = 4
GLA_DK = 32
GLA_DV = 64
GLA_GATE_RANK = 16
GLA_TAU = 16.0
GLA_CHUNK = 64
GLA_GROUP = 256
POOL_WINDOWS = (2, 4, 8, 16)
POOL_GROUP = 64
POOL_WIDTH = POOL_GROUP * len(POOL_WINDOWS)
POOL_BUF = max(POOL_WINDOWS) - 1
MLA_OUT = MLA_HEADS * V_HEAD
GLA_OUT = GLA_HEADS * GLA_DV
N_MEM = 256
XA_HEADS = 4
XA_HEAD = D_MODEL // XA_HEADS
D_FF = 4 * D_MODEL
PAGE_SIZE = 128
EPS = 1e-6

HEAD_PAD = 128
QK_W = MLA_HEADS * HEAD_PAD
GLA_QK = GLA_HEADS * GLA_DK
IN_PAD = 1792
MISC0 = 1664

MIB = 1024 * 1024


def _cparams(sem, vmem_mib):
    return pltpu.CompilerParams(dimension_semantics=sem, vmem_limit_bytes=vmem_mib * MIB)


def _rms(x, g):
    y = x * lax.rsqrt(jnp.mean(x * x, axis=-1, keepdims=True) + EPS)
    return y * g


def _head_rms(x, g, nh, width, count):
    outs = []
    for h in range(nh):
        p = x[:, h * width:(h + 1) * width]
        ms = jnp.sum(p * p, axis=-1, keepdims=True) * (1.0 / count)
        outs.append(p * lax.rsqrt(ms + EPS) * g[:, h * width:(h + 1) * width])
    return jnp.concatenate(outs, axis=1)


def _dot(a, b):
    return jnp.dot(a, b, preferred_element_type=F32)


def _dot_nt(a, b):
    return lax.dot_general(a, b, (((1,), (1,)), ((), ())), preferred_element_type=F32)


def _split_bf16(x):
    hi = x.astype(BF16)
    lo = (x - hi.astype(F32)).astype(BF16)
    return hi, lo


def _dot_split(x, w):
    hi, lo = _split_bf16(x)
    return _dot(hi, w) + _dot(lo, w)


def _norm_rope(x, xrot, g, grot, c, s):
    outs = []
    for h in range(MLA_HEADS):
        sl = slice(h * HEAD_PAD, (h + 1) * HEAD_PAD)
        p = x[:, sl]
        r = lax.rsqrt(jnp.sum(p * p, axis=-1, keepdims=True) * (1.0 / QK_HEAD) + EPS)
        outs.append((p * r * g[:, sl]) * c + (xrot[:, sl] * r * grot[:, sl]) * s)
    return jnp.concatenate(outs, axis=1)


def _log_sigmoid(x):
    return -(jnp.maximum(-x, 0.0) + jnp.log1p(jnp.exp(-jnp.abs(x))))


def _sigmoid(x):
    return 1.0 / (1.0 + jnp.exp(-x))


def _memkv_kernel(mem_ref, g_ref, wk_ref, wv_ref, gk_ref, k_ref, v_ref, kb_ref, vb_ref):
    m = _rms(mem_ref[...], g_ref[...]).astype(BF16)
    k = _head_rms(_dot(m, wk_ref[...]), gk_ref[...], XA_HEADS, XA_HEAD, XA_HEAD)
    v = _dot(m, wv_ref[...])
    k_ref[...] = k.reshape(k_ref.shape)
    v_ref[...] = v.reshape(v_ref.shape)
    kb_ref[...] = k.astype(BF16)
    vb_ref[...] = v.astype(BF16)


def _memory_kv(mem2, g, wk, wv, gk):
    m = mem2.shape[0]
    depth = wk.shape[0]
    tm = 512
    nb = tm // N_MEM
    vec = lambda l, i: (l, 0, 0)
    return pl.pallas_call(
        _memkv_kernel,
        grid=(depth, m // tm),
        in_specs=[pl.BlockSpec((tm, D_MODEL), lambda l, i: (i, 0)), pl.BlockSpec((None, 1, D_MODEL), vec),
                  pl.BlockSpec((None, D_MODEL, D_MODEL), vec), pl.BlockSpec((None, D_MODEL, D_MODEL), vec),
                  pl.BlockSpec((None, 1, D_MODEL), vec)],
        out_specs=[pl.BlockSpec((nb, None, N_MEM, D_MODEL), lambda l, i: (i, l, 0, 0))] * 2
        + [pl.BlockSpec((None, tm, D_MODEL), lambda l, i: (l, i, 0))] * 2,
        out_shape=[jax.ShapeDtypeStruct((m // N_MEM, depth, N_MEM, D_MODEL), F32)] * 2
        + [jax.ShapeDtypeStruct((depth, m, D_MODEL), BF16)] * 2,
        compiler_params=_cparams(("parallel", "parallel"), 40),
        name="memory_kv",
    )(mem2, g, wk, wv, gk)


def _inproj_kernel(x_ref, gmix_ref, win_ref, gql_ref, wuq_ref, gkv_ref, wukv_ref,
                   gq_ref, gk_ref, place_ref, wg_ref, bg_ref, gqr_ref, gkr_ref,
                   c_ref, s_ref, lat_ref, kr_ref, q_ref, k_ref, v_ref, gq_o, gk_o, gv_o, la_o, gr_o, u_o):
    h = _rms(x_ref[...], gmix_ref[...]).astype(BF16)
    z = _dot(h, win_ref[...])
    c_q = z[:, 0:384]
    c_kv = z[:, 384:640]
    misc = z[:, MISC0:MISC0 + 128]
    gq_o[...] = z[:, 640:768]
    gk_o[...] = z[:, 768:896]
    gv_o[...] = z[:, 896:1152]
    gr_o[...] = z[:, 1152:1408]
    u_o[...] = z[:, 1408:1664]
    kr_ref[...] = misc[:, 0:QK_ROPE]

    lat = _rms(c_kv, gkv_ref[...])
    lat_ref[...] = lat
    latb = lat.astype(BF16)
    c, s = c_ref[...], s_ref[...]

    q2 = _dot(_rms(c_q, gql_ref[...]).astype(BF16), wuq_ref[...])
    q = _norm_rope(q2[:, :QK_W], q2[:, QK_W:], gq_ref[...], gqr_ref[...], c, s)
    q_ref[...] = (q * (QK_HEAD ** -0.5)).astype(BF16)

    placed = _dot(jnp.concatenate(_split_bf16(misc), axis=1), place_ref[...])
    kv = _dot(latb, wukv_ref[...])
    k = kv[:, :QK_W] + placed[:, :QK_W]
    k_ref[...] = _norm_rope(k, placed[:, QK_W:], gk_ref[...], gkr_ref[...], c, s).astype(BF16)
    v_ref[...] = kv[:, QK_W:].astype(BF16)

    gate = _dot(misc.astype(BF16), wg_ref[...]) + bg_ref[...]
    la_o[...] = _log_sigmoid(gate) * (1.0 / GLA_TAU)


def _in_proj(x2, lw, ropes, tm):
    m = x2.shape[0]
    row = lambda i: (i, 0)
    fix = lambda i: (0, 0)
    c, s = ropes
    if c.shape[0] == m:
        rope_map = row
    else:
        nb = c.shape[0] // tm
        rope_map = lambda i: (i % nb, 0)
    weights = [lw["norm_mix"], lw["w_in"], lw["g_q_lat"], lw["w_uq2"], lw["g_kv_lat"], lw["w_ukv"],
               lw["g_qk_q"], lw["g_qk_k"], lw["place"], lw["w_gate"], lw["b_gate"],
               lw["g_q_rot"], lw["g_k_rot"]]
    in_specs = [pl.BlockSpec((tm, D_MODEL), row)]
    in_specs += [pl.BlockSpec(w.shape, fix) for w in weights]
    in_specs += [pl.BlockSpec((tm, HEAD_PAD), rope_map)] * 2
    widths = [(KV_RANK, F32), (QK_ROPE, F32), (QK_W, BF16), (QK_W, BF16), (MLA_OUT, BF16),
              (GLA_QK, F32), (GLA_QK, F32), (GLA_OUT, F32), (GLA_QK, F32), (GLA_OUT, F32),
              (POOL_WIDTH, F32)]
    return pl.pallas_call(
        _inproj_kernel,
        grid=(m // tm,),
        in_specs=in_specs,
        out_specs=[pl.BlockSpec((tm, w), row) for w, _ in widths],
        out_shape=[jax.ShapeDtypeStruct((m, w), dt) for w, dt in widths],
        compiler_params=_cparams(("parallel",), 48),
        name="in_proj",
    )(x2, *weights, c, s)


def _flash_kernel(q_ref, k_ref, v_ref, o_ref, *, tq):
    qi = pl.program_id(2)
    q = q_ref[...]

    def block(j, carry, masked):
        m, l, acc = carry
        start = pl.multiple_of(j * tq, tq)
        k = k_ref[pl.ds(start, tq), :]
        v = v_ref[pl.ds(start, tq), :]
        s = _dot_nt(q, k)
        if masked:
            rows = lax.broadcasted_iota(jnp.int32, (tq, tq), 0)
            cols = lax.broadcasted_iota(jnp.int32, (tq, tq), 1)
            s = jnp.where(cols <= rows, s, -jnp.inf)
        m_new = jnp.maximum(m, jnp.max(s, axis=-1, keepdims=True))
        alpha = jnp.exp(m - m_new)
        p = jnp.exp(s - m_new)
        l = alpha * l + jnp.sum(p, axis=-1, keepdims=True)
        acc = alpha * acc + _dot(p.astype(BF16), v)
        return m_new, l, acc

    init = (jnp.full((tq, 1), -jnp.inf, F32), jnp.zeros((tq, 1), F32), jnp.zeros((tq, V_HEAD), F32))
    carry = lax.fori_loop(0, qi, lambda j, cr: block(j, cr, False), init)
    _, l, acc = block(qi, carry, True)
    o_ref[...] = (acc / l).astype(o_ref.dtype)


def _mla_prompt(q3, k3, v3, tq):
    b, t, _ = q3.shape
    qmap = lambda bi, h, qi: (bi, qi, h)
    kmap = lambda bi, h, qi: (bi, 0, h)
    return pl.pallas_call(
        functools.partial(_flash_kernel, tq=tq),
        grid=(b, MLA_HEADS, t // tq),
        in_specs=[pl.BlockSpec((None, tq, HEAD_PAD), qmap), pl.BlockSpec((None, t, HEAD_PAD), kmap),
                  pl.BlockSpec((None, t, V_HEAD), kmap)],
        out_specs=pl.BlockSpec((None, tq, V_HEAD), qmap),
        out_shape=jax.ShapeDtypeStruct((b, t, MLA_OUT), BF16),
        compiler_params=_cparams(("parallel", "parallel", "arbitrary"), 40),
        name="mla_prompt",
    )(q3, k3, v3)


def _gla_kernel(q_ref, k_ref, v_ref, la_ref, gr_ref, tri_ref, ones_ref, e64_ref, gout_ref, o_ref, st_ref, *, t):
    c = GLA_CHUNK
    gt = GLA_GROUP
    tri = tri_ref[...]
    ones_bd = ones_ref[...]
    e64 = e64_ref[...]
    gout = gout_ref[...]
    lane_qk = lax.broadcasted_iota(jnp.int32, (gt, GLA_QK), 1) // GLA_DK
    lane_v = lax.broadcasted_iota(jnp.int32, (gt, GLA_OUT), 1) // GLA_DV
    row_chunk = lax.broadcasted_iota(jnp.int32, (gt, GLA_QK), 0) // c
    rr = lax.broadcasted_iota(jnp.int32, (gt, gt), 0)
    cc = lax.broadcasted_iota(jnp.int32, (gt, gt), 1)
    causal = (cc <= rr) & (rr // c == cc // c)
    bd = (lax.broadcasted_iota(jnp.int32, (GLA_OUT, GLA_QK), 0) // GLA_DV
          == lax.broadcasted_iota(jnp.int32, (GLA_OUT, GLA_QK), 1) // GLA_DK)

    def group(gi, st):
        r0 = pl.multiple_of(gi * gt, gt)
        q = q_ref[pl.ds(r0, gt), :] * (GLA_DK ** -0.5)
        k = k_ref[pl.ds(r0, gt), :]
        v = v_ref[pl.ds(r0, gt), :]
        a_pair = jnp.concatenate(_split_bf16(la_ref[pl.ds(r0, gt), :]), axis=1)
        b2 = _dot(tri, a_pair)
        t2 = _dot(ones_bd, a_pair)
        b = b2[:, :GLA_QK] + b2[:, GLA_QK:]
        tot = t2[:, :GLA_QK] + t2[:, GLA_QK:]
        qd = q * jnp.exp(b)
        qdb = qd.astype(BF16)
        kd = (k * jnp.exp(-b)).astype(BF16)
        k2 = k * jnp.exp(tot - b)
        decay = jnp.exp(tot)
        vb = v.astype(BF16)
        vt = v.T.astype(BF16)
        o = jnp.zeros((gt, GLA_OUT), F32)
        for h in range(GLA_HEADS):
            qh = jnp.where(lane_qk == h, qd, 0.0).astype(BF16)
            att = jnp.where(causal, _dot_nt(qh, kd), 0.0)
            o = o + jnp.where(lane_v == h, _dot(att.astype(BF16), vb), 0.0)
        k2_blocks = jnp.concatenate([jnp.where(row_chunk == ci, k2, 0.0) for ci in range(gt // c)], axis=1)
        incr = _dot(vt, k2_blocks.astype(BF16))
        inter = []
        for ci in range(gt // c):
            inter.append(_dot_nt(qdb[ci * c:(ci + 1) * c, :], st.astype(BF16)))
            st = st * decay[ci * c:ci * c + 1, :] + jnp.where(bd, incr[:, ci * GLA_QK:(ci + 1) * GLA_QK], 0.0)
        o = o + jnp.concatenate(inter, axis=0)
        ms = _dot_split(o * o, e64) * (1.0 / GLA_DV)
        on = o * lax.rsqrt(ms + EPS) * gout
        gr = gr_ref[pl.ds(r0, gt), :]
        o_ref[pl.ds(r0, gt), :] = (on * (gr * _sigmoid(gr))).astype(o_ref.dtype)
        return st

    st_ref[...] = lax.fori_loop(0, t // gt, group, jnp.zeros((GLA_OUT, GLA_QK), F32), unroll=2)


def _gla_prompt(gq3, gk3, gv3, la3, gr3, lw):
    b, t, _ = gq3.shape
    bmap = lambda i: (i, 0, 0)
    fix = lambda i: (0, 0)
    chunk_id = jnp.arange(GLA_GROUP) // GLA_CHUNK
    same = chunk_id[:, None] == chunk_id[None, :]
    tri = (same & (jnp.arange(GLA_GROUP)[None, :] <= jnp.arange(GLA_GROUP)[:, None])).astype(BF16)
    return pl.pallas_call(
        functools.partial(_gla_kernel, t=t),
        grid=(b,),
        in_specs=[pl.BlockSpec((None, t, GLA_QK), bmap), pl.BlockSpec((None, t, GLA_QK), bmap),
                  pl.BlockSpec((None, t, GLA_OUT), bmap), pl.BlockSpec((None, t, GLA_QK), bmap),
                  pl.BlockSpec((None, t, GLA_OUT), bmap), pl.BlockSpec((GLA_GROUP, GLA_GROUP), fix),
                  pl.BlockSpec((GLA_GROUP, GLA_GROUP), fix),
                  pl.BlockSpec((GLA_OUT, GLA_OUT), fix), pl.BlockSpec((1, GLA_OUT), fix)],
        out_specs=[pl.BlockSpec((None, t, GLA_OUT), bmap), pl.BlockSpec((None, GLA_OUT, GLA_QK), bmap)],
        out_shape=[jax.ShapeDtypeStruct((b, t, GLA_OUT), BF16),
                   jax.ShapeDtypeStruct((b, GLA_OUT, GLA_QK), F32)],
        compiler_params=_cparams(("parallel",), 40),
        name="gla_prompt",
    )(gq3, gk3, gv3, la3, gr3, tri, same.astype(BF16), lw["e64"], lw["g_gla_out"])


def _pool_select(parts):
    lane = lax.broadcasted_iota(jnp.int32, parts[0].shape, 1) // POOL_GROUP
    out = parts[-1]
    for g in range(len(parts) - 2, -1, -1):
        out = jnp.where(lane == g, parts[g], out)
    return out


def _pool_kernel(u_ref, halo_ref, w_ref, sc_ref, o_ref, *, tp):
    i = pl.program_id(1)
    u = u_ref[...]
    halo = jnp.where(i == 0, 0.0, halo_ref[...])
    ext = jnp.concatenate([halo, u], axis=0)
    hb = halo.shape[0]
    pos = i * tp + lax.broadcasted_iota(jnp.int32, (tp, 1), 0)
    sums = ext
    parts = []
    span = 1
    for w in POOL_WINDOWS:
        while span < w:
            sums = sums + pltpu.roll(sums, span, 0)
            span *= 2
        cnt = jnp.minimum(w, pos + 1).astype(F32)
        parts.append(sums[hb:, :] / cnt - u)
    pooled = _pool_select(parts)
    o_ref[...] = (_dot(pooled.astype(BF16), w_ref[...]) * sc_ref[...]).astype(o_ref.dtype)


def _pool_prompt(u3, lw, tp):
    b, t, _ = u3.shape
    hb = 16
    return pl.pallas_call(
        functools.partial(_pool_kernel, tp=tp),
        grid=(b, t // tp),
        in_specs=[pl.BlockSpec((None, tp, POOL_WIDTH), lambda bi, i: (bi, i, 0)),
                  pl.BlockSpec((None, hb, POOL_WIDTH),
                               lambda bi, i: (bi, jnp.maximum(i * (tp // hb) - 1, 0), 0)),
                  pl.BlockSpec((POOL_WIDTH, POOL_WIDTH), lambda bi, i: (0, 0)),
                  pl.BlockSpec((1, POOL_WIDTH), lambda bi, i: (0, 0))],
        out_specs=pl.BlockSpec((None, tp, POOL_WIDTH), lambda bi, i: (bi, i, 0)),
        out_shape=jax.ShapeDtypeStruct((b, t, POOL_WIDTH), BF16),
        compiler_params=_cparams(("parallel", "parallel"), 32),
        name="pool_prompt",
    )(u3, u3, lw["w_pool"], lw["pool_scale"])


def _mix_out(x, mla, gla, pool, wo_ref):
    mix = jnp.concatenate([mla.astype(BF16), gla.astype(BF16), pool.astype(BF16)], axis=1)
    return x + _dot(mix, wo_ref[...])


def _xa_query(x1, gxa, wxq, gxq):
    q = _dot(_rms(x1, gxa).astype(BF16), wxq)
    return _head_rms(q, gxq, XA_HEADS, XA_HEAD, XA_HEAD) * (XA_HEAD ** -0.5)


def _xa_attend(qb, kb, vb):
    outs = []
    for h in range(XA_HEADS):
        sl = slice(h * XA_HEAD, (h + 1) * XA_HEAD)
        s = _dot_nt(qb[:, sl], kb[:, sl])
        e = jnp.exp(s - jnp.max(s, axis=-1, keepdims=True))
        p = e / jnp.sum(e, axis=-1, keepdims=True)
        outs.append(_dot(p.astype(BF16), vb[:, sl]))
    return jnp.concatenate(outs, axis=1).astype(BF16)


def _post_kernel(x_ref, mla_ref, gla_ref, pool_ref, wo_ref, gxa_ref, wxq_ref, gxq_ref, mk_ref, mv_ref,
                 wxo_ref, o_ref):
    x1 = _mix_out(x_ref[...], mla_ref[...], gla_ref[...], pool_ref[...], wo_ref)
    qb = _xa_query(x1, gxa_ref[...], wxq_ref[...], gxq_ref[...]).astype(BF16)
    o = _xa_attend(qb, mk_ref[...], mv_ref[...])
    o_ref[...] = x1 + _dot(o, wxo_ref[...])


def _post_prompt(x2, mla, gla, pool, mk4, mv4, layer, lw, t, tm):
    m = x2.shape[0]
    per_seq = t // tm
    row = lambda i: (i, 0)
    fix = lambda i: (0, 0)
    mem = lambda i: (layer, i // per_seq, 0, 0)
    sq = (D_MODEL, D_MODEL)
    return pl.pallas_call(
        _post_kernel,
        grid=(m // tm,),
        in_specs=[pl.BlockSpec((tm, D_MODEL), row), pl.BlockSpec((tm, MLA_OUT), row),
                  pl.BlockSpec((tm, GLA_OUT), row), pl.BlockSpec((tm, POOL_WIDTH), row),
                  pl.BlockSpec(sq, fix), pl.BlockSpec((1, D_MODEL), fix), pl.BlockSpec(sq, fix),
                  pl.BlockSpec((1, D_MODEL), fix), pl.BlockSpec((None, None, N_MEM, D_MODEL), mem),
                  pl.BlockSpec((None, None, N_MEM, D_MODEL), mem), pl.BlockSpec(sq, fix)],
        out_specs=pl.BlockSpec((tm, D_MODEL), row),
        out_shape=jax.ShapeDtypeStruct((m, D_MODEL), F32),
        compiler_params=_cparams(("parallel",), 48),
        name="post_prompt",
    )(x2, mla, gla, pool, lw["w_out"], lw["norm_xa"], lw["w_xq"], lw["g_xq"], mk4, mv4, lw["w_xo"])


def _mlp_kernel(x_ref, g_ref, w1_ref, w2_ref, o_ref):
    x = x_ref[...]
    h = _rms(x, g_ref[...]).astype(BF16)
    a = jnp.maximum(_dot(h, w1_ref[...]), 0.0)
    o_ref[...] = x + _dot((a * a).astype(BF16), w2_ref[...])


def _mlp(x2, lw, tm):
    m = x2.shape[0]
    once = pl.Buffered(1)
    layer = lw["layer"]
    return pl.pallas_call(
        _mlp_kernel,
        grid=(m // tm,),
        in_specs=[pl.BlockSpec((tm, D_MODEL), lambda i: (i, 0)),
                  pl.BlockSpec((1, D_MODEL), lambda i: (0, 0)),
                  pl.BlockSpec((None, D_MODEL, D_FF), lambda i: (layer, 0, 0), pipeline_mode=once),
                  pl.BlockSpec((None, D_FF, D_MODEL), lambda i: (layer, 0, 0), pipeline_mode=once)],
        out_specs=pl.BlockSpec((tm, D_MODEL), lambda i: (i, 0)),
        out_shape=jax.ShapeDtypeStruct((m, D_MODEL), F32),
        compiler_params=_cparams(("parallel",), 56),
        name="mlp",
    )(x2, lw["norm_mlp"], lw["w_ff1"], lw["w_ff2"])


DEC_PAGES = 64
DEC_CHUNK = 8192
DEC_LHS_ROWS = MLA_HEADS * QK_NOPE + 16
DEC_SS_ROWS = 48


def _mla_decode_kernel(pt_ref, q_ref, k_ref, latn_ref, cos_ref, sin_ref, wukt_ref, gkn_ref, gcol_ref,
                       ssel_ref, wuv_ref, lat_hbm, krt_hbm, o_ref,
                       lat_buf, kr_buf, sem, m_sc, l_sc, acc_sc, lhs_sc, lb_sc, *, pp, nj, layer):
    t = pl.program_id(0)
    n_steps = pl.num_programs(0)
    j = t % nj
    slot = t % 2

    def page_copies(step, buf_slot):
        b = step // nj
        base = (step % nj) * pp
        copies = []
        for p in range(pp):
            page = pt_ref[b, base + p]
            copies.append(pltpu.make_async_copy(lat_hbm.at[page, layer], lat_buf.at[buf_slot, p],
                                                sem.at[buf_slot, 0]))
            copies.append(pltpu.make_async_copy(krt_hbm.at[page, layer],
                                                kr_buf.at[buf_slot, :, pl.ds(p * PAGE_SIZE, PAGE_SIZE)],
                                                sem.at[buf_slot, 1]))
        return copies

    def start_all(copies):
        for i, cp in enumerate(copies):
            cp.start(priority=(i // 2 + i) % 2)

    start_all(page_copies(jnp.minimum(t + 1, n_steps - 1), 1 - slot))

    @pl.when(t == 0)
    def _():
        start_all(page_copies(t, slot))
        lhs_sc[0:MLA_HEADS * QK_NOPE, :] = wukt_ref[...]

    q16 = q_ref[...].astype(F32)

    @pl.when(j == 0)
    def _():
        m_sc[...] = jnp.full(m_sc.shape, -jnp.inf, F32)
        l_sc[...] = jnp.zeros(l_sc.shape, F32)
        acc_sc[...] = jnp.zeros(acc_sc.shape, F32)
        qn = (q16 * gkn_ref[...])[:, 0:QK_NOPE]
        tiled = jnp.concatenate([qn] * MLA_HEADS, axis=1)
        row = lax.broadcasted_iota(jnp.int32, tiled.shape, 0)
        lane = lax.broadcasted_iota(jnp.int32, tiled.shape, 1) // QK_NOPE
        qbd = jnp.where(row == lane, tiled, 0.0).astype(BF16)
        lhs_sc[MLA_HEADS * QK_NOPE:, :] = _dot(qbd, wukt_ref[...]).astype(BF16)

    for cp in page_copies(t, slot):
        cp.wait()

    qr = q16[:, QK_NOPE:QK_HEAD].astype(BF16)
    lhs = lhs_sc[...]
    gcol = gcol_ref[...]
    ssel = ssel_ref[...]
    half = QK_ROPE // 2
    nf = MLA_HEADS * QK_NOPE
    for p in range(pp):
        lb_sc[p * PAGE_SIZE:(p + 1) * PAGE_SIZE, :] = lat_buf[slot, p].astype(BF16)

    ck = DEC_CHUNK
    scores = []
    for c0 in range(0, pp * PAGE_SIZE, ck):
        res = _dot_nt(lhs, lb_sc[c0:c0 + ck, :])
        sq = res[0:nf, :] * res[0:nf, :]
        sn = res[nf:nf + 8, :]
        krt = kr_buf[slot, :, c0:c0 + ck]
        parts = [sq[h * QK_NOPE:(h + 1) * QK_NOPE, :].reshape(8, 8, ck).sum(axis=0) for h in range(MLA_HEADS)]
        parts.append((krt * krt).reshape(4, 8, ck).sum(axis=0))
        parts.append(jnp.zeros((8, ck), F32))
        p_hi, p_lo = _split_bf16(jnp.concatenate(parts, axis=0))
        ss = _dot(ssel, p_hi) + _dot(ssel, p_lo)
        krg = krt * gcol
        cs = cos_ref[:, c0:c0 + ck]
        sn_ = sin_ref[:, c0:c0 + ck]
        x1, x2 = krg[0:half, :], krg[half:, :]
        roped = jnp.concatenate([x1 * cs - x2 * sn_, x2 * cs + x1 * sn_], axis=0)
        sr = _dot(qr, roped.astype(BF16))[0:8, :]
        scores.append((sn + sr) * lax.rsqrt(ss * (1.0 / QK_HEAD) + EPS))
    s_all = jnp.concatenate(scores, axis=1)

    m_old = m_sc[...]
    m_new = jnp.maximum(m_old, jnp.max(s_all, axis=1, keepdims=True))
    alpha = jnp.exp(m_old - m_new)
    p = jnp.exp(s_all - m_new[:, 0:1])
    l_sc[...] = alpha * l_sc[...] + jnp.sum(p, axis=1, keepdims=True)
    acc_sc[...] = alpha[:, 0:1] * acc_sc[...] + _dot(p.astype(BF16), lb_sc[...])
    m_sc[...] = m_new

    @pl.when(t == n_steps - 1)
    def _():
        for cp in page_copies(t, 1 - slot):
            cp.wait()

    @pl.when(j == nj - 1)
    def _():
        s_self = jnp.sum((q16 * k_ref[...].astype(F32))[0:8, :], axis=1, keepdims=True)
        m_o = m_sc[...]
        m_n = jnp.maximum(m_o, s_self)
        al = jnp.exp(m_o - m_n)
        ps = jnp.exp(s_self - m_n)
        l = al * l_sc[...] + ps
        acc = al[:, 0:1] * acc_sc[...] + ps[:, 0:1] * latn_ref[...]
        o_lat = acc / l[:, 0:1]
        res = _dot(o_lat.astype(BF16), wuv_ref[...])
        row = lax.broadcasted_iota(jnp.int32, res.shape, 0)
        lane = lax.broadcasted_iota(jnp.int32, res.shape, 1) // V_HEAD
        o_ref[...] = jnp.sum(jnp.where(row == lane, res, 0.0), axis=0, keepdims=True).astype(o_ref.dtype)


def _mla_decode(page_table, q16, k16, lat_new3, cache_lat, cache_krt, layer, lw, tabs):
    pp = DEC_PAGES
    n = q16.shape[0]
    n_pages = page_table.shape[1]
    nj = n_pages // pp
    cos_t, sin_t = tabs
    fix2 = lambda t, pt: (0, 0)
    seq3 = lambda t, pt: (t // nj, 0, 0)
    tab = lambda t, pt: (0, t % nj)
    weights = [lw["w_uk_t"], lw["g_k_nope"], lw["g_k_rope_col"], lw["ss_select"], lw["w_uv"]]
    in_specs = [pl.BlockSpec((None, 16, HEAD_PAD), seq3), pl.BlockSpec((None, 16, HEAD_PAD), seq3),
                pl.BlockSpec((None, 1, KV_RANK), seq3),
                pl.BlockSpec((QK_ROPE // 2, pp * PAGE_SIZE), tab),
                pl.BlockSpec((QK_ROPE // 2, pp * PAGE_SIZE), tab)]
    in_specs += [pl.BlockSpec(w.shape, fix2) for w in weights]
    in_specs += [pl.BlockSpec(memory_space=pl.ANY), pl.BlockSpec(memory_space=pl.ANY)]
    grid_spec = pltpu.PrefetchScalarGridSpec(
        num_scalar_prefetch=1,
        grid=(n * nj,),
        in_specs=in_specs,
        out_specs=pl.BlockSpec((None, 1, MLA_OUT), seq3),
        scratch_shapes=[pltpu.VMEM((2, pp, PAGE_SIZE, KV_RANK), F32),
                        pltpu.VMEM((2, QK_ROPE, pp * PAGE_SIZE), F32),
                        pltpu.SemaphoreType.DMA((2, 2)),
                        pltpu.VMEM((8, 128), F32), pltpu.VMEM((8, 128), F32), pltpu.VMEM((8, KV_RANK), F32),
                        pltpu.VMEM((DEC_LHS_ROWS, KV_RANK), BF16),
                        pltpu.VMEM((pp * PAGE_SIZE, KV_RANK), BF16)])
    return pl.pallas_call(
        functools.partial(_mla_decode_kernel, pp=pp, nj=nj, layer=layer),
        grid_spec=grid_spec,
        out_shape=jax.ShapeDtypeStruct((n, 1, MLA_OUT), F32),
        compiler_params=_cparams(("arbitrary",), 52),
        name="mla_decode",
    )(page_table, q16, k16, lat_new3, cos_t, sin_t, *weights, cache_lat, cache_krt)


def _dec_mix_kernel(q_ref, k_ref, la_ref, v_ref, gr_ref, u_ref, s_ref, pb_ref, gout_ref, wp_ref, sc_ref,
                    s_o, gla_o, pool_o, pb_o, *, past):
    a = jnp.exp(la_ref[...])
    v4 = v_ref[...]
    vexp = jnp.concatenate([jnp.broadcast_to(v4[h:h + 1, :], (GLA_DK, GLA_DV)) for h in range(GLA_HEADS)],
                           axis=0)
    s_new = a * s_ref[...] + k_ref[...] * vexp
    s_o[...] = s_new
    prod = (q_ref[...] * (GLA_DK ** -0.5)) * s_new
    gr = gr_ref[...]
    gout = gout_ref[...]
    for h in range(GLA_HEADS):
        o = jnp.sum(prod[h * GLA_DK:(h + 1) * GLA_DK, :], axis=0, keepdims=True)
        on = _rms(o, gout)
        g = gr[h:h + 1, :]
        gla_o[h:h + 1, :] = (on * (g * _sigmoid(g))).astype(gla_o.dtype)

    st = pb_ref[...]
    u = u_ref[...]
    ridx = lax.broadcasted_iota(jnp.int32, st.shape, 0)
    parts = []
    for w in POOL_WINDOWS:
        tot = u + jnp.sum(jnp.where(ridx >= POOL_BUF + 1 - w, st, 0.0), axis=0, keepdims=True)
        parts.append(tot / float(min(w, past + 1)) - u)
    pooled = jnp.broadcast_to(_pool_select(parts), (8, POOL_WIDTH)).astype(BF16)
    y = _dot(pooled, wp_ref[...])[0:1, :] * sc_ref[...]
    pool_o[...] = y.astype(pool_o.dtype)
    pb_o[0:POOL_BUF - 1, :] = st[1:POOL_BUF, :]
    pb_o[POOL_BUF - 1:POOL_BUF, :] = u


def _dec_mix(gq, gk, la, gv, gr, u, state_gla4, state_pool, layer, lw, past):
    n = gq.shape[0]
    col = lambda x: x.reshape(n, GLA_QK, 1)
    b3 = lambda b: (b, 0, 0)
    fix = lambda b: (0, 0)
    return pl.pallas_call(
        functools.partial(_dec_mix_kernel, past=past),
        grid=(n,),
        in_specs=[pl.BlockSpec((None, GLA_QK, 1), b3)] * 3
        + [pl.BlockSpec((None, GLA_HEADS, GLA_DV), b3)] * 2
        + [pl.BlockSpec((None, 1, POOL_WIDTH), b3),
           pl.BlockSpec((None, None, GLA_QK, GLA_DV), lambda b: (b, layer, 0, 0)),
           pl.BlockSpec((None, None, POOL_BUF, POOL_WIDTH), lambda b: (b, layer, 0, 0)),
           pl.BlockSpec((1, GLA_DV), fix), pl.BlockSpec((POOL_WIDTH, POOL_WIDTH), fix),
           pl.BlockSpec((1, POOL_WIDTH), fix)],
        out_specs=[pl.BlockSpec((None, GLA_QK, GLA_DV), b3), pl.BlockSpec((None, GLA_HEADS, GLA_DV), b3),
                   pl.BlockSpec((None, 1, POOL_WIDTH), b3), pl.BlockSpec((None, POOL_BUF, POOL_WIDTH), b3)],
        out_shape=[jax.ShapeDtypeStruct((n, GLA_QK, GLA_DV), F32),
                   jax.ShapeDtypeStruct((n, GLA_HEADS, GLA_DV), F32),
                   jax.ShapeDtypeStruct((n, 1, POOL_WIDTH), F32),
                   jax.ShapeDtypeStruct((n, POOL_BUF, POOL_WIDTH), F32)],
        compiler_params=_cparams(("parallel",), 32),
        name="dec_mix",
    )(col(gq), col(gk), col(la), gv.reshape(n, GLA_HEADS, GLA_DV), gr.reshape(n, GLA_HEADS, GLA_DV),
      u.reshape(n, 1, POOL_WIDTH), state_gla4, state_pool, lw["g_gla_out64"], lw["w_pool"], lw["pool_scale"])


def _post_a_kernel(x_ref, mla_ref, gla_ref, pool_ref, wo_ref, gxa_ref, wxq_ref, gxq_ref, x1_ref, q_ref):
    x1 = _mix_out(x_ref[...], mla_ref[...], gla_ref[...], pool_ref[...], wo_ref)
    x1_ref[...] = x1
    q_ref[...] = _xa_query(x1, gxa_ref[...], wxq_ref[...], gxq_ref[...])


def _post_a(x2, mla, gla, pool, lw):
    m = x2.shape[0]
    return pl.pallas_call(
        _post_a_kernel,
        out_shape=[jax.ShapeDtypeStruct((m, D_MODEL), F32), jax.ShapeDtypeStruct((m, D_MODEL), F32)],
        compiler_params=pltpu.CompilerParams(vmem_limit_bytes=32 * MIB),
        name="post_a",
    )(x2, mla, gla, pool, lw["w_out"], lw["norm_xa"], lw["w_xq"], lw["g_xq"])


XA_HALVES = XA_HEAD // 128
XA_ROWS = XA_HALVES * XA_HEADS


def _xa_decode_kernel(q_ref, mk_ref, mv_ref, o_ref):
    k3 = mk_ref[...].reshape(N_MEM, XA_ROWS, 128)
    v3 = mv_ref[...].reshape(N_MEM, XA_ROWS, 128)
    s = jnp.sum(k3 * q_ref[...][None], axis=-1, keepdims=True)
    s = s[:, 0:XA_HEADS] + s[:, XA_HEADS:XA_ROWS]
    s = jnp.concatenate([s, s], axis=1)
    e = jnp.exp(s - jnp.max(s, axis=0, keepdims=True))
    p = e / jnp.sum(e, axis=0, keepdims=True)
    o_ref[...] = jnp.sum(p * v3, axis=0)


def _xa_decode(q3, cache_k, cache_v, layer):
    n = q3.shape[0]
    depth = cache_k.shape[1]
    view = lambda c: c.reshape(n, depth, N_MEM, XA_HEADS, XA_HALVES, 128).transpose(0, 1, 2, 4, 3, 5).reshape(
        n, depth, N_MEM * XA_ROWS, 128)
    q8 = q3.reshape(n, XA_HEADS, XA_HALVES, 128).transpose(0, 2, 1, 3).reshape(n, XA_ROWS, 128)
    b3 = lambda b: (b, 0, 0)
    mem = lambda b: (b, layer, 0, 0)
    o = pl.pallas_call(
        _xa_decode_kernel,
        grid=(n,),
        in_specs=[pl.BlockSpec((None, XA_ROWS, 128), b3),
                  pl.BlockSpec((None, None, N_MEM * XA_ROWS, 128), mem),
                  pl.BlockSpec((None, None, N_MEM * XA_ROWS, 128), mem)],
        out_specs=pl.BlockSpec((None, XA_ROWS, 128), b3),
        out_shape=jax.ShapeDtypeStruct((n, XA_ROWS, 128), F32),
        compiler_params=_cparams(("parallel",), 32),
        name="xa_decode",
    )(q8, view(cache_k), view(cache_v))
    return o.reshape(n, XA_HALVES, XA_HEADS, 128).transpose(0, 2, 1, 3).reshape(n, 1, D_MODEL)


def _post_b_kernel(x1_ref, o_ref, wxo_ref, x2_ref):
    x2_ref[...] = x1_ref[...] + _dot(o_ref[...].astype(BF16), wxo_ref[...])


def _post_b(x1, o, lw):
    return pl.pallas_call(
        _post_b_kernel,
        out_shape=jax.ShapeDtypeStruct(x1.shape, F32),
        compiler_params=pltpu.CompilerParams(vmem_limit_bytes=32 * MIB),
        name="post_b",
    )(x1, o, lw["w_xo"])


def _rope_tables_rows(pos):
    half = QK_ROPE // 2
    inv = ROPE_THETA ** (-jnp.arange(half, dtype=F32) / half)
    ang = pos.astype(F32)[:, None] * inv[None, :]
    cos, sin = jnp.cos(ang), jnp.sin(ang)
    n = pos.shape[0]
    c = jnp.concatenate([jnp.ones((n, QK_NOPE), F32), cos, cos, jnp.ones((n, HEAD_PAD - QK_HEAD), F32)], axis=1)
    s = jnp.concatenate([jnp.zeros((n, QK_NOPE), F32), sin, sin, jnp.zeros((n, HEAD_PAD - QK_HEAD), F32)],
                        axis=1)
    return c, s


def _rope_tables_cols(pos):
    half = QK_ROPE // 2
    inv = ROPE_THETA ** (-jnp.arange(half, dtype=F32) / half)
    ang = pos.astype(F32)[:, None] * inv[None, :]
    return jnp.cos(ang).T, jnp.sin(ang).T


def _pad_heads(w, width):
    pad = [(0, 0)] * (w.ndim - 1) + [(0, HEAD_PAD - width)]
    w = jnp.pad(w, pad)
    return w.reshape(w.shape[:-2] + (w.shape[-2] * HEAD_PAD,))


def _layer_weights(l, p):
    w_in = p["w_in"][l]
    sizes = (Q_RANK, KV_RANK, QK_ROPE, GLA_QK, GLA_QK, GLA_OUT, GLA_GATE_RANK, GLA_OUT, POOL_WIDTH)
    offs = [0]
    for s in sizes:
        offs.append(offs[-1] + s)
    piece = lambda i: w_in[:, offs[i]:offs[i + 1]]
    w_in_p = jnp.concatenate([piece(0), piece(1), piece(3), piece(4), piece(5), piece(7), piece(8), piece(2),
                              piece(6), jnp.zeros((D_MODEL, IN_PAD - sum(sizes)), F32)], axis=1)
    g_q = _pad_heads(jnp.broadcast_to(p["g_qk_q"][l], (MLA_HEADS, QK_HEAD)), QK_HEAD)
    g_k = _pad_heads(jnp.broadcast_to(p["g_qk_k"][l], (MLA_HEADS, QK_HEAD)), QK_HEAD)
    place = jnp.zeros((128, QK_W), F32)
    j = jnp.arange(QK_ROPE)
    for h in range(MLA_HEADS):
        place = place.at[j, h * HEAD_PAD + QK_NOPE + j].set(1.0)
    w_gate = jnp.zeros((128, GLA_QK), F32).at[QK_ROPE:QK_ROPE + GLA_GATE_RANK].set(p["w_gate_up"][l])
    half = QK_ROPE // 2

    def partner(w):
        w4 = w.reshape(w.shape[:-1] + (MLA_HEADS, HEAD_PAD))
        x1 = w4[..., QK_NOPE:QK_NOPE + half]
        x2 = w4[..., QK_NOPE + half:QK_HEAD]
        out = jnp.zeros_like(w4).at[..., QK_NOPE:QK_NOPE + half].set(x2).at[..., QK_NOPE + half:QK_HEAD].set(x1)
        return out.reshape(w.shape)

    sign = jnp.tile(jnp.concatenate([jnp.zeros((QK_NOPE,), F32), -jnp.ones((half,), F32), jnp.ones((half,), F32),
                                     jnp.zeros((HEAD_PAD - QK_HEAD,), F32)]), MLA_HEADS)
    w_uq_pad = _pad_heads(p["w_uq"][l], QK_HEAD)
    w_uv2 = p["w_uv"][l].reshape(KV_RANK, MLA_OUT)
    place_both = jnp.concatenate([place, partner(place) * sign], axis=1)
    wp = p["w_pool"][l]
    w_pool = jnp.zeros((POOL_WIDTH, POOL_WIDTH), F32)
    for g in range(len(POOL_WINDOWS)):
        w_pool = w_pool.at[g * POOL_GROUP:(g + 1) * POOL_GROUP, g * POOL_GROUP:(g + 1) * POOL_GROUP].set(wp[g])
    blk = jnp.arange(GLA_OUT) // GLA_DV
    row = lambda v: v.reshape(1, -1).astype(F32)
    bf = lambda v: v.astype(BF16)
    w_uk2 = p["w_uk"][l].reshape(KV_RANK, MLA_HEADS * QK_NOPE)
    g_k_nope = jnp.zeros((1, HEAD_PAD), F32).at[0, :QK_NOPE].set(p["g_qk_k"][l][:QK_NOPE])
    col = jnp.arange(DEC_SS_ROWS)[None, :] // 8
    hrow = jnp.arange(8)[:, None]
    ss_select = ((hrow < MLA_HEADS) & ((col == hrow) | (col == MLA_HEADS))).astype(F32)
    return {
        "norm_mix": row(p["norm_mix"][l]), "w_in": bf(w_in_p), "g_q_lat": row(p["g_q_lat"][l]),
        "w_uq2": bf(jnp.concatenate([w_uq_pad, partner(w_uq_pad) * sign], axis=1)),
        "g_kv_lat": row(p["g_kv_lat"][l]),
        "w_ukv": bf(jnp.concatenate([_pad_heads(p["w_uk"][l], QK_NOPE), w_uv2], axis=1)), "w_uv": bf(w_uv2),
        "g_qk_q": row(g_q), "g_qk_k": row(g_k), "place": bf(jnp.tile(place_both, (2, 1))), "w_gate": bf(w_gate),
        "g_q_rot": row(partner(g_q)), "g_k_rot": row(partner(g_k)),
        "b_gate": row(p["b_gate"][l]),
        "e64": bf((blk[:, None] == blk[None, :]).astype(F32)),
        "g_gla_out": row(jnp.tile(p["g_gla_out"][l], GLA_HEADS)), "g_gla_out64": row(p["g_gla_out"][l]),
        "w_pool": bf(w_pool), "pool_scale": row(p["pool_scale"][l]),
        "w_out": bf(p["w_out"][l]), "norm_xa": row(p["norm_xa"][l]), "w_xq": bf(p["w_xq"][l]),
        "g_xq": row(jnp.tile(p["g_xq"][l], XA_HEADS)), "w_xo": bf(p["w_xo"][l]),
        "norm_mlp": row(p["norm_mlp"][l]), "w_ff1": p["w_ff1_bf16"], "w_ff2": p["w_ff2_bf16"], "layer": l,
        "w_uk_t": bf(w_uk2.T), "g_k_nope": g_k_nope,
        "g_k_rope_col": jnp.broadcast_to(p["g_qk_k"][l][QK_NOPE:, None], (QK_ROPE, DEC_CHUNK)).astype(F32),
        "ss_select": bf(ss_select),
    }


def _prompt_layer(x2, layer, mk4, mv4, lw, ropes, b, t):
    lat, kr, q, k, v, gq, gk, gv, la, gr, u = _in_proj(x2, lw, ropes, 512)
    r3 = lambda a: a.reshape(b, t, a.shape[-1])
    mla = _mla_prompt(r3(q), r3(k), r3(v), 512).reshape(b * t, MLA_OUT)
    gla, st = _gla_prompt(r3(gq), r3(gk), r3(gv), r3(la), r3(gr), lw)
    pool = _pool_prompt(r3(u), lw, t).reshape(b * t, POOL_WIDTH)
    x2 = _post_prompt(x2, mla, gla.reshape(b * t, GLA_OUT), pool, mk4, mv4, layer, lw, t, 512)
    x2 = _mlp(x2, lw, 512)
    gla_state = jnp.stack([st[:, h * GLA_DV:(h + 1) * GLA_DV, h * GLA_DK:(h + 1) * GLA_DK]
                           for h in range(GLA_HEADS)], axis=1).transpose(0, 1, 3, 2)
    outs = (lat.reshape(b, t, KV_RANK), kr.reshape(b, t, QK_ROPE), gla_state,
            r3(u)[:, t - POOL_BUF:])
    return x2, outs


def _sample_layer(x2, layer, lw, ropes, tabs, page_table, cache_lat, cache_krt, state_gla4, state_pool,
                  cache_mk, cache_mv, past):
    n = x2.shape[0]
    lat, kr, q, k, v, gq, gk, gv, la, gr, u = _in_proj(x2, lw, ropes, n)
    pad16 = lambda a: jnp.pad(a.reshape(n, MLA_HEADS, HEAD_PAD), ((0, 0), (0, 16 - MLA_HEADS), (0, 0)))
    mla = _mla_decode(page_table, pad16(q), pad16(k), lat.reshape(n, 1, KV_RANK), cache_lat, cache_krt, layer,
                      lw, tabs).reshape(n, MLA_OUT)
    s_new, gla, pool, pb = _dec_mix(gq, gk, la, gv, gr, u, state_gla4, state_pool, layer, lw, past)
    x1, qx = _post_a(x2, mla, gla.reshape(n, GLA_OUT), pool.reshape(n, POOL_WIDTH), lw)
    ox = _xa_decode(qx.reshape(n, 1, D_MODEL), cache_mk, cache_mv, layer).reshape(n, D_MODEL)
    x2 = _mlp(_post_b(x1, ox, lw), lw, n)
    outs = (lat.reshape(n, 1, KV_RANK), kr.reshape(n, 1, QK_ROPE),
            s_new.reshape(n, GLA_HEADS, GLA_DK, GLA_DV), pb)
    return x2, outs


def kernel(x_prompt, x_sample, mem_prompt, cache_mla_latent, cache_mla_krope, state_gla, state_pool,
           cache_mem_k, cache_mem_v, page_table, norm_mix, w_in, g_q_lat, w_uq, g_kv_lat, w_uk, w_uv,
           g_qk_q, g_qk_k, w_gate_up, b_gate, g_gla_out, w_pool, pool_scale, w_out, norm_xa, norm_mem,
           w_xq, w_xk, w_xv, g_xq, g_xk, w_xo, norm_mlp, w_ff1, w_ff2):
    params = dict(norm_mix=norm_mix, w_in=w_in, g_q_lat=g_q_lat, w_uq=w_uq, g_kv_lat=g_kv_lat, w_uk=w_uk,
                  w_uv=w_uv, g_qk_q=g_qk_q, g_qk_k=g_qk_k, w_gate_up=w_gate_up, b_gate=b_gate,
                  g_gla_out=g_gla_out, w_pool=w_pool, pool_scale=pool_scale, w_out=w_out, norm_xa=norm_xa,
                  norm_mem=norm_mem, w_xq=w_xq, w_xk=w_xk, w_xv=w_xv, g_xq=g_xq, g_xk=g_xk, w_xo=w_xo,
                  norm_mlp=norm_mlp, w_ff1_bf16=w_ff1.astype(BF16), w_ff2_bf16=w_ff2.astype(BF16))
    b, t, _ = x_prompt.shape
    n, dec_seq, _ = x_sample.shape
    depth = norm_mix.shape[0]
    n_pages = page_table.shape[1]
    past = n_pages * PAGE_SIZE

    ropes_p = _rope_tables_rows(jnp.arange(t, dtype=jnp.int32))
    ropes_s = _rope_tables_rows(jnp.full((n,), past, jnp.int32))
    tabs = _rope_tables_cols(jnp.arange(past, dtype=jnp.int32))

    yp = x_prompt.reshape(b * t, D_MODEL)
    ys = x_sample.reshape(n * dec_seq, D_MODEL)
    mem2 = mem_prompt.reshape(b * N_MEM, D_MODEL)
    state_gla4 = state_gla.reshape(n, depth, GLA_QK, GLA_DV)
    cache_krt = cache_mla_krope.transpose(0, 1, 3, 2)

    mk, mv, mkb, mvb = _memory_kv(mem2, norm_mem.reshape(depth, 1, D_MODEL), w_xk.astype(BF16),
                                  w_xv.astype(BF16), jnp.tile(g_xk, (1, XA_HEADS)).reshape(depth, 1, D_MODEL))
    mk4 = mkb.reshape(depth, b, N_MEM, D_MODEL)
    mv4 = mvb.reshape(depth, b, N_MEM, D_MODEL)

    p_outs, s_outs = [], []
    for l in range(depth):
        lw = _layer_weights(l, params)
        yp, po = _prompt_layer(yp, l, mk4, mv4, lw, ropes_p, b, t)
        ys, so = _sample_layer(ys, l, lw, ropes_s, tabs, page_table, cache_mla_latent, cache_krt,
                               state_gla4, state_pool, cache_mem_k, cache_mem_v, past)
        p_outs.append(po)
        s_outs.append(so)

    stack = lambda outs, i: jnp.stack([o[i] for o in outs], axis=1)
    return (yp.reshape(b, t, D_MODEL), ys.reshape(n, dec_seq, D_MODEL),
            stack(p_outs, 0), stack(p_outs, 1), stack(p_outs, 2), stack(p_outs, 3),
            mk.reshape(b, depth, N_MEM, XA_HEADS, XA_HEAD), mv.reshape(b, depth, N_MEM, XA_HEADS, XA_HEAD),
            stack(s_outs, 0), stack(s_outs, 1), stack(s_outs, 2), stack(s_outs, 3))
```

```python
import functools

import jax
import jax.numpy as jnp
from jax import lax
from jax.experimental import pallas as pl
from jax.experimental.pallas import tpu as pltpu

F32 = jnp.float32
BF16 = jnp.bfloat16

D_MODEL = 1024
MLA_HEADS = 4
QK_NOPE = 64
QK_ROPE = 32
QK_HEAD = QK_NOPE + QK_ROPE
V_HEAD = 128
Q_RANK = 384
KV_RANK = 256
ROPE_THETA = 10000.0
GLA_HEADS = 4
GLA_DK = 32
GLA_DV = 64
GLA_GATE_RANK = 16
GLA_TAU = 16.0
GLA_CHUNK = 64
GLA_GROUP = 256
POOL_WINDOWS = (2, 4, 8, 16)
POOL_GROUP = 64
POOL_WIDTH = POOL_GROUP * len(POOL_WINDOWS)
POOL_BUF = max(POOL_WINDOWS) - 1
MLA_OUT = MLA_HEADS * V_HEAD
GLA_OUT = GLA_HEADS * GLA_DV
N_MEM = 256
XA_HEADS = 4
XA_HEAD = D_MODEL // XA_HEADS
D_FF = 4 * D_MODEL
PAGE_SIZE = 128
EPS = 1e-6

HEAD_PAD = 128
QK_W = MLA_HEADS * HEAD_PAD
GLA_QK = GLA_HEADS * GLA_DK
IN_PAD = 1792
MISC0 = 1664

ROW_TILE = 512
MIB = 1024 * 1024


def _cparams(sem, vmem_mib):
    return pltpu.CompilerParams(dimension_semantics=sem, vmem_limit_bytes=vmem_mib * MIB)


def _rms(x, g):
    y = x * lax.rsqrt(jnp.mean(x * x, axis=-1, keepdims=True) + EPS)
    return y * g


def _head_rms(x, g, nh, width, count):
    outs = []
    for h in range(nh):
        p = x[:, h * width:(h + 1) * width]
        ms = jnp.sum(p * p, axis=-1, keepdims=True) * (1.0 / count)
        outs.append(p * lax.rsqrt(ms + EPS) * g[:, h * width:(h + 1) * width])
    return jnp.concatenate(outs, axis=1)


def _dot(a, b):
    return jnp.dot(a, b, preferred_element_type=F32)


def _dot_nt(a, b):
    return lax.dot_general(a, b, (((1,), (1,)), ((), ())), preferred_element_type=F32)


def _split_bf16(x):
    hi = x.astype(BF16)
    lo = (x - hi.astype(F32)).astype(BF16)
    return hi, lo


def _dot_split(x, w):
    hi, lo = _split_bf16(x)
    return _dot(hi, w) + _dot(lo, w)


def _norm_rope(x, xrot, g, grot, c, s):
    outs = []
    for h in range(MLA_HEADS):
        sl = slice(h * HEAD_PAD, (h + 1) * HEAD_PAD)
        p = x[:, sl]
        r = lax.rsqrt(jnp.sum(p * p, axis=-1, keepdims=True) * (1.0 / QK_HEAD) + EPS)
        outs.append((p * r * g[:, sl]) * c + (xrot[:, sl] * r * grot[:, sl]) * s)
    return jnp.concatenate(outs, axis=1)


def _log_sigmoid(x):
    return -(jnp.maximum(-x, 0.0) + jnp.log1p(jnp.exp(-jnp.abs(x))))


def _sigmoid(x):
    return 1.0 / (1.0 + jnp.exp(-x))


def _memkv_kernel(mem_ref, g_ref, wk_ref, wv_ref, gk_ref, k_ref, v_ref, kb_ref, vb_ref):
    m = _rms(mem_ref[...], g_ref[...]).astype(BF16)
    k = _head_rms(_dot(m, wk_ref[...]), gk_ref[...], XA_HEADS, XA_HEAD, XA_HEAD)
    v = _dot(m, wv_ref[...])
    k_ref[...] = k.reshape(k_ref.shape)
    v_ref[...] = v.reshape(v_ref.shape)
    kb_ref[...] = k.astype(BF16)
    vb_ref[...] = v.astype(BF16)


def _memory_kv(mem2, g, wk, wv, gk):
    m = mem2.shape[0]
    depth = wk.shape[0]
    tm = ROW_TILE
    nb = tm // N_MEM
    vec = lambda l, i: (l, 0, 0)
    return pl.pallas_call(
        _memkv_kernel,
        grid=(depth, m // tm),
        in_specs=[pl.BlockSpec((tm, D_MODEL), lambda l, i: (i, 0)), pl.BlockSpec((None, 1, D_MODEL), vec),
                  pl.BlockSpec((None, D_MODEL, D_MODEL), vec), pl.BlockSpec((None, D_MODEL, D_MODEL), vec),
                  pl.BlockSpec((None, 1, D_MODEL), vec)],
        out_specs=[pl.BlockSpec((nb, None, N_MEM, D_MODEL), lambda l, i: (i, l, 0, 0))] * 2
        + [pl.BlockSpec((None, tm, D_MODEL), lambda l, i: (l, i, 0))] * 2,
        out_shape=[jax.ShapeDtypeStruct((m // N_MEM, depth, N_MEM, D_MODEL), F32)] * 2
        + [jax.ShapeDtypeStruct((depth, m, D_MODEL), BF16)] * 2,
        compiler_params=_cparams(("parallel", "parallel"), 40),
        name="memory_kv",
    )(mem2, g, wk, wv, gk)


def _inproj_kernel(x_ref, gmix_ref, win_ref, gql_ref, wuq_ref, gkv_ref, wukv_ref,
                   gq_ref, gk_ref, place_ref, wg_ref, bg_ref, gqr_ref, gkr_ref,
                   c_ref, s_ref, lat_ref, kr_ref, q_ref, k_ref, v_ref, gq_o, gk_o, gv_o, la_o, gr_o, u_o):
    h = _rms(x_ref[...], gmix_ref[...]).astype(BF16)
    z = _dot(h, win_ref[...])
    c_q = z[:, 0:384]
    c_kv = z[:, 384:640]
    misc = z[:, MISC0:MISC0 + 128]
    gq_o[...] = z[:, 640:768]
    gk_o[...] = z[:, 768:896]
    gv_o[...] = z[:, 896:1152]
    gr_o[...] = z[:, 1152:1408]
    u_o[...] = z[:, 1408:1664]
    kr_ref[...] = misc[:, 0:QK_ROPE]

    lat = _rms(c_kv, gkv_ref[...])
    lat_ref[...] = lat
    latb = lat.astype(BF16)
    c, s = c_ref[...], s_ref[...]

    q2 = _dot(_rms(c_q, gql_ref[...]).astype(BF16), wuq_ref[...])
    q = _norm_rope(q2[:, :QK_W], q2[:, QK_W:], gq_ref[...], gqr_ref[...], c, s)
    q_ref[...] = (q * (QK_HEAD ** -0.5)).astype(BF16)

    placed = _dot(jnp.concatenate(_split_bf16(misc), axis=1), place_ref[...])
    kv = _dot(latb, wukv_ref[...])
    k = kv[:, :QK_W] + placed[:, :QK_W]
    k_ref[...] = _norm_rope(k, placed[:, QK_W:], gk_ref[...], gkr_ref[...], c, s).astype(BF16)
    v_ref[...] = kv[:, QK_W:].astype(BF16)

    gate = _dot(misc.astype(BF16), wg_ref[...]) + bg_ref[...]
    la_o[...] = _log_sigmoid(gate) * (1.0 / GLA_TAU)


def _in_proj(x2, lw, ropes, tm):
    m = x2.shape[0]
    row = lambda i: (i, 0)
    fix = lambda i: (0, 0)
    c, s = ropes
    if c.shape[0] == m:
        rope_map = row
    else:
        nb = c.shape[0] // tm
        rope_map = lambda i: (i % nb, 0)
    weights = [lw["norm_mix"], lw["w_in"], lw["g_q_lat"], lw["w_uq2"], lw["g_kv_lat"], lw["w_ukv"],
               lw["g_qk_q"], lw["g_qk_k"], lw["place"], lw["w_gate"], lw["b_gate"],
               lw["g_q_rot"], lw["g_k_rot"]]
    in_specs = [pl.BlockSpec((tm, D_MODEL), row)]
    in_specs += [pl.BlockSpec(w.shape, fix) for w in weights]
    in_specs += [pl.BlockSpec((tm, HEAD_PAD), rope_map)] * 2
    widths = [(KV_RANK, F32), (QK_ROPE, F32), (QK_W, BF16), (QK_W, BF16), (MLA_OUT, BF16),
              (GLA_QK, F32), (GLA_QK, F32), (GLA_OUT, F32), (GLA_QK, F32), (GLA_OUT, F32),
              (POOL_WIDTH, F32)]
    return pl.pallas_call(
        _inproj_kernel,
        grid=(m // tm,),
        in_specs=in_specs,
        out_specs=[pl.BlockSpec((tm, w), row) for w, _ in widths],
        out_shape=[jax.ShapeDtypeStruct((m, w), dt) for w, dt in widths],
        compiler_params=_cparams(("parallel",), 48),
        name="in_proj",
    )(x2, *weights, c, s)


def _flash_kernel(q_ref, k_ref, v_ref, o_ref, *, tq):
    qi = pl.program_id(2)
    q = q_ref[...]

    def block(j, carry, masked):
        m, l, acc = carry
        start = pl.multiple_of(j * tq, tq)
        k = k_ref[pl.ds(start, tq), :]
        v = v_ref[pl.ds(start, tq), :]
        s = _dot_nt(q, k)
        if masked:
            rows = lax.broadcasted_iota(jnp.int32, (tq, tq), 0)
            cols = lax.broadcasted_iota(jnp.int32, (tq, tq), 1)
            s = jnp.where(cols <= rows, s, -jnp.inf)
        m_new = jnp.maximum(m, jnp.max(s, axis=-1, keepdims=True))
        alpha = jnp.exp(m - m_new)
        p = jnp.exp(s - m_new)
        l = alpha * l + jnp.sum(p, axis=-1, keepdims=True)
        acc = alpha * acc + _dot(p.astype(BF16), v)
        return m_new, l, acc

    init = (jnp.full((tq, 1), -jnp.inf, F32), jnp.zeros((tq, 1), F32), jnp.zeros((tq, V_HEAD), F32))
    carry = lax.fori_loop(0, qi, lambda j, cr: block(j, cr, False), init)
    _, l, acc = block(qi, carry, True)
    o_ref[...] = (acc / l).astype(o_ref.dtype)


def _mla_prompt(q3, k3, v3, tq):
    b, t, _ = q3.shape
    qmap = lambda bi, h, qi: (bi, qi, h)
    kmap = lambda bi, h, qi: (bi, 0, h)
    return pl.pallas_call(
        functools.partial(_flash_kernel, tq=tq),
        grid=(b, MLA_HEADS, t // tq),
        in_specs=[pl.BlockSpec((None, tq, HEAD_PAD), qmap), pl.BlockSpec((None, t, HEAD_PAD), kmap),
                  pl.BlockSpec((None, t, V_HEAD), kmap)],
        out_specs=pl.BlockSpec((None, tq, V_HEAD), qmap),
        out_shape=jax.ShapeDtypeStruct((b, t, MLA_OUT), BF16),
        compiler_params=_cparams(("parallel", "parallel", "arbitrary"), 40),
        name="mla_prompt",
    )(q3, k3, v3)


def _gla_kernel(q_ref, k_ref, v_ref, la_ref, gr_ref, tri_ref, ones_ref, e64_ref, gout_ref, o_ref, st_ref, *, t):
    c = GLA_CHUNK
    gt = GLA_GROUP
    tri = tri_ref[...]
    ones_bd = ones_ref[...]
    e64 = e64_ref[...]
    gout = gout_ref[...]
    lane_qk = lax.broadcasted_iota(jnp.int32, (gt, GLA_QK), 1) // GLA_DK
    lane_v = lax.broadcasted_iota(jnp.int32, (gt, GLA_OUT), 1) // GLA_DV
    row_chunk = lax.broadcasted_iota(jnp.int32, (gt, GLA_QK), 0) // c
    rr = lax.broadcasted_iota(jnp.int32, (gt, gt), 0)
    cc = lax.broadcasted_iota(jnp.int32, (gt, gt), 1)
    causal = (cc <= rr) & (rr // c == cc // c)
    bd = (lax.broadcasted_iota(jnp.int32, (GLA_OUT, GLA_QK), 0) // GLA_DV
          == lax.broadcasted_iota(jnp.int32, (GLA_OUT, GLA_QK), 1) // GLA_DK)

    def group(gi, st):
        r0 = pl.multiple_of(gi * gt, gt)
        q = q_ref[pl.ds(r0, gt), :] * (GLA_DK ** -0.5)
        k = k_ref[pl.ds(r0, gt), :]
        v = v_ref[pl.ds(r0, gt), :]
        a_pair = jnp.concatenate(_split_bf16(la_ref[pl.ds(r0, gt), :]), axis=1)
        b2 = _dot(tri, a_pair)
        t2 = _dot(ones_bd, a_pair)
        b = b2[:, :GLA_QK] + b2[:, GLA_QK:]
        tot = t2[:, :GLA_QK] + t2[:, GLA_QK:]
        qd = q * jnp.exp(b)
        qdb = qd.astype(BF16)
        kd = (k * jnp.exp(-b)).astype(BF16)
        k2 = k * jnp.exp(tot - b)
        decay = jnp.exp(tot)
        vb = v.astype(BF16)
        vt = v.T.astype(BF16)
        o = jnp.zeros((gt, GLA_OUT), F32)
        for h in range(GLA_HEADS):
            qh = jnp.where(lane_qk == h, qd, 0.0).astype(BF16)
            att = jnp.where(causal, _dot_nt(qh, kd), 0.0)
            o = o + jnp.where(lane_v == h, _dot(att.astype(BF16), vb), 0.0)
        k2_blocks = jnp.concatenate([jnp.where(row_chunk == ci, k2, 0.0) for ci in range(gt // c)], axis=1)
        incr = _dot(vt, k2_blocks.astype(BF16))
        inter = []
        for ci in range(gt // c):
            inter.append(_dot_nt(qdb[ci * c:(ci + 1) * c, :], st.astype(BF16)))
            st = st * decay[ci * c:ci * c + 1, :] + jnp.where(bd, incr[:, ci * GLA_QK:(ci + 1) * GLA_QK], 0.0)
        o = o + jnp.concatenate(inter, axis=0)
        ms = _dot_split(o * o, e64) * (1.0 / GLA_DV)
        on = o * lax.rsqrt(ms + EPS) * gout
        gr = gr_ref[pl.ds(r0, gt), :]
        o_ref[pl.ds(r0, gt), :] = (on * (gr * _sigmoid(gr))).astype(o_ref.dtype)
        return st

    st_ref[...] = lax.fori_loop(0, t // gt, group, jnp.zeros((GLA_OUT, GLA_QK), F32), unroll=4)


def _gla_prompt(gq3, gk3, gv3, la3, gr3, lw):
    b, t, _ = gq3.shape
    bmap = lambda i: (i, 0, 0)
    fix = lambda i: (0, 0)
    chunk_id = jnp.arange(GLA_GROUP) // GLA_CHUNK
    same = chunk_id[:, None] == chunk_id[None, :]
    tri = (same & (jnp.arange(GLA_GROUP)[None, :] <= jnp.arange(GLA_GROUP)[:, None])).astype(BF16)
    return pl.pallas_call(
        functools.partial(_gla_kernel, t=t),
        grid=(b,),
        in_specs=[pl.BlockSpec((None, t, GLA_QK), bmap), pl.BlockSpec((None, t, GLA_QK), bmap),
                  pl.BlockSpec((None, t, GLA_OUT), bmap), pl.BlockSpec((None, t, GLA_QK), bmap),
                  pl.BlockSpec((None, t, GLA_OUT), bmap), pl.BlockSpec((GLA_GROUP, GLA_GROUP), fix),
                  pl.BlockSpec((GLA_GROUP, GLA_GROUP), fix),
                  pl.BlockSpec((GLA_OUT, GLA_OUT), fix), pl.BlockSpec((1, GLA_OUT), fix)],
        out_specs=[pl.BlockSpec((None, t, GLA_OUT), bmap), pl.BlockSpec((None, GLA_OUT, GLA_QK), bmap)],
        out_shape=[jax.ShapeDtypeStruct((b, t, GLA_OUT), BF16),
                   jax.ShapeDtypeStruct((b, GLA_OUT, GLA_QK), F32)],
        compiler_params=_cparams(("parallel",), 40),
        name="gla_prompt",
    )(gq3, gk3, gv3, la3, gr3, tri, same.astype(BF16), lw["e64"], lw["g_gla_out"])


def _pool_select(parts):
    lane = lax.broadcasted_iota(jnp.int32, parts[0].shape, 1) // POOL_GROUP
    out = parts[-1]
    for g in range(len(parts) - 2, -1, -1):
        out = jnp.where(lane == g, parts[g], out)
    return out


def _pool_kernel(u_ref, halo_ref, w_ref, sc_ref, o_ref, *, tp):
    i = pl.program_id(1)
    u = u_ref[...]
    halo = jnp.where(i == 0, 0.0, halo_ref[...])
    ext = jnp.concatenate([halo, u], axis=0)
    hb = halo.shape[0]
    pos = i * tp + lax.broadcasted_iota(jnp.int32, (tp, 1), 0)
    sums = ext
    parts = []
    span = 1
    for w in POOL_WINDOWS:
        while span < w:
            sums = sums + pltpu.roll(sums, span, 0)
            span *= 2
        cnt = jnp.minimum(w, pos + 1).astype(F32)
        parts.append(sums[hb:, :] / cnt - u)
    pooled = _pool_select(parts)
    o_ref[...] = (_dot(pooled.astype(BF16), w_ref[...]) * sc_ref[...]).astype(o_ref.dtype)


def _pool_prompt(u3, lw, tp):
    b, t, _ = u3.shape
    hb = 16
    return pl.pallas_call(
        functools.partial(_pool_kernel, tp=tp),
        grid=(b, t // tp),
        in_specs=[pl.BlockSpec((None, tp, POOL_WIDTH), lambda bi, i: (bi, i, 0)),
                  pl.BlockSpec((None, hb, POOL_WIDTH),
                               lambda bi, i: (bi, jnp.maximum(i * (tp // hb) - 1, 0), 0)),
                  pl.BlockSpec((POOL_WIDTH, POOL_WIDTH), lambda bi, i: (0, 0)),
                  pl.BlockSpec((1, POOL_WIDTH), lambda bi, i: (0, 0))],
        out_specs=pl.BlockSpec((None, tp, POOL_WIDTH), lambda bi, i: (bi, i, 0)),
        out_shape=jax.ShapeDtypeStruct((b, t, POOL_WIDTH), BF16),
        compiler_params=_cparams(("parallel", "parallel"), 32),
        name="pool_prompt",
    )(u3, u3, lw["w_pool"], lw["pool_scale"])


def _mix_out(x, mla, gla, pool, wo_ref):
    mix = jnp.concatenate([mla.astype(BF16), gla.astype(BF16), pool.astype(BF16)], axis=1)
    return x + _dot(mix, wo_ref[...])


def _xa_query(x1, gxa, wxq, gxq):
    q = _dot(_rms(x1, gxa).astype(BF16), wxq)
    return _head_rms(q, gxq, XA_HEADS, XA_HEAD, XA_HEAD) * (XA_HEAD ** -0.5)


def _xa_attend(qb, kb, vb):
    outs = []
    for h in range(XA_HEADS):
        sl = slice(h * XA_HEAD, (h + 1) * XA_HEAD)
        s = _dot_nt(qb[:, sl], kb[:, sl])
        e = jnp.exp(s - jnp.max(s, axis=-1, keepdims=True))
        p = e / jnp.sum(e, axis=-1, keepdims=True)
        outs.append(_dot(p.astype(BF16), vb[:, sl]))
    return jnp.concatenate(outs, axis=1).astype(BF16)


def _post_kernel(x_ref, mla_ref, gla_ref, pool_ref, wo_ref, gxa_ref, wxq_ref, gxq_ref, mk_ref, mv_ref,
                 wxo_ref, o_ref):
    x1 = _mix_out(x_ref[...], mla_ref[...], gla_ref[...], pool_ref[...], wo_ref)
    qb = _xa_query(x1, gxa_ref[...], wxq_ref[...], gxq_ref[...]).astype(BF16)
    o = _xa_attend(qb, mk_ref[...], mv_ref[...])
    o_ref[...] = x1 + _dot(o, wxo_ref[...])


def _post_prompt(x2, mla, gla, pool, mk4, mv4, layer, lw, t, tm):
    m = x2.shape[0]
    per_seq = t // tm
    row = lambda i: (i, 0)
    fix = lambda i: (0, 0)
    mem = lambda i: (layer, i // per_seq, 0, 0)
    sq = (D_MODEL, D_MODEL)
    return pl.pallas_call(
        _post_kernel,
        grid=(m // tm,),
        in_specs=[pl.BlockSpec((tm, D_MODEL), row), pl.BlockSpec((tm, MLA_OUT), row),
                  pl.BlockSpec((tm, GLA_OUT), row), pl.BlockSpec((tm, POOL_WIDTH), row),
                  pl.BlockSpec(sq, fix), pl.BlockSpec((1, D_MODEL), fix), pl.BlockSpec(sq, fix),
                  pl.BlockSpec((1, D_MODEL), fix), pl.BlockSpec((None, None, N_MEM, D_MODEL), mem),
                  pl.BlockSpec((None, None, N_MEM, D_MODEL), mem), pl.BlockSpec(sq, fix)],
        out_specs=pl.BlockSpec((tm, D_MODEL), row),
        out_shape=jax.ShapeDtypeStruct((m, D_MODEL), F32),
        compiler_params=_cparams(("parallel",), 48),
        name="post_prompt",
    )(x2, mla, gla, pool, lw["w_out"], lw["norm_xa"], lw["w_xq"], lw["g_xq"], mk4, mv4, lw["w_xo"])


def _mlp_kernel(x_ref, g_ref, w1_ref, w2_ref, o_ref):
    x = x_ref[...]
    h = _rms(x, g_ref[...]).astype(BF16)
    a = jnp.maximum(_dot(h, w1_ref[...]), 0.0)
    o_ref[...] = x + _dot((a * a).astype(BF16), w2_ref[...])


def _mlp(x2, lw, tm):
    m = x2.shape[0]
    once = pl.Buffered(1)
    layer = lw["layer"]
    return pl.pallas_call(
        _mlp_kernel,
        grid=(m // tm,),
        in_specs=[pl.BlockSpec((tm, D_MODEL), lambda i: (i, 0)),
                  pl.BlockSpec((1, D_MODEL), lambda i: (0, 0)),
                  pl.BlockSpec((None, D_MODEL, D_FF), lambda i: (layer, 0, 0), pipeline_mode=once),
                  pl.BlockSpec((None, D_FF, D_MODEL), lambda i: (layer, 0, 0), pipeline_mode=once)],
        out_specs=pl.BlockSpec((tm, D_MODEL), lambda i: (i, 0)),
        out_shape=jax.ShapeDtypeStruct((m, D_MODEL), F32),
        compiler_params=_cparams(("parallel",), 56),
        name="mlp",
    )(x2, lw["norm_mlp"], lw["w_ff1"], lw["w_ff2"])


DEC_PAGES = 64
DEC_CHUNK = 8192
DEC_LHS_ROWS = MLA_HEADS * QK_NOPE + 16
DEC_SS_ROWS = 48


def _mla_decode_kernel(pt_ref, q_ref, k_ref, latn_ref, cos_ref, sin_ref, wukt_ref, gkn_ref, gcol_ref,
                       ssel_ref, wuv_ref, lat_hbm, krt_hbm, o_ref,
                       lat_buf, kr_buf, sem, m_sc, l_sc, acc_sc, lhs_sc, lb_sc, *, pp, nj, layer):
    t = pl.program_id(0)
    n_steps = pl.num_programs(0)
    j = t % nj
    slot = t % 2

    def page_copies(step, buf_slot):
        b = step // nj
        base = (step % nj) * pp
        copies = []
        for p in range(pp):
            page = pt_ref[b, base + p]
            copies.append(pltpu.make_async_copy(lat_hbm.at[page, layer], lat_buf.at[buf_slot, p],
                                                sem.at[buf_slot, 0]))
            copies.append(pltpu.make_async_copy(krt_hbm.at[page, layer],
                                                kr_buf.at[buf_slot, :, pl.ds(p * PAGE_SIZE, PAGE_SIZE)],
                                                sem.at[buf_slot, 1]))
        return copies

    def start_all(copies):
        for i, cp in enumerate(copies):
            cp.start(priority=(i // 2 + i) % 2)

    start_all(page_copies(jnp.minimum(t + 1, n_steps - 1), 1 - slot))

    @pl.when(t == 0)
    def _():
        start_all(page_copies(t, slot))
        lhs_sc[0:MLA_HEADS * QK_NOPE, :] = wukt_ref[...]

    q16 = q_ref[...].astype(F32)

    @pl.when(j == 0)
    def _():
        m_sc[...] = jnp.full(m_sc.shape, -jnp.inf, F32)
        l_sc[...] = jnp.zeros(l_sc.shape, F32)
        acc_sc[...] = jnp.zeros(acc_sc.shape, F32)
        qn = (q16 * gkn_ref[...])[:, 0:QK_NOPE]
        tiled = jnp.concatenate([qn] * MLA_HEADS, axis=1)
        row = lax.broadcasted_iota(jnp.int32, tiled.shape, 0)
        lane = lax.broadcasted_iota(jnp.int32, tiled.shape, 1) // QK_NOPE
        qbd = jnp.where(row == lane, tiled, 0.0).astype(BF16)
        lhs_sc[MLA_HEADS * QK_NOPE:, :] = _dot(qbd, wukt_ref[...]).astype(BF16)

    for cp in page_copies(t, slot):
        cp.wait()

    qr = q16[:, QK_NOPE:QK_HEAD].astype(BF16)
    lhs = lhs_sc[...]
    gcol = gcol_ref[...]
    ssel = ssel_ref[...]
    half = QK_ROPE // 2
    nf = MLA_HEADS * QK_NOPE
    for p in range(pp):
        lb_sc[p * PAGE_SIZE:(p + 1) * PAGE_SIZE, :] = lat_buf[slot, p].astype(BF16)

    ck = DEC_CHUNK
    scores = []
    for c0 in range(0, pp * PAGE_SIZE, ck):
        res = _dot_nt(lhs, lb_sc[c0:c0 + ck, :])
        sq = res[0:nf, :] * res[0:nf, :]
        sn = res[nf:nf + 8, :]
        krt = kr_buf[slot, :, c0:c0 + ck]
        parts = [sq[h * QK_NOPE:(h + 1) * QK_NOPE, :].reshape(8, 8, ck).sum(axis=0) for h in range(MLA_HEADS)]
        parts.append((krt * krt).reshape(4, 8, ck).sum(axis=0))
        parts.append(jnp.zeros((8, ck), F32))
        p_hi, p_lo = _split_bf16(jnp.concatenate(parts, axis=0))
        ss = _dot(ssel, p_hi) + _dot(ssel, p_lo)
        krg = krt * gcol
        cs = cos_ref[:, c0:c0 + ck]
        sn_ = sin_ref[:, c0:c0 + ck]
        x1, x2 = krg[0:half, :], krg[half:, :]
        roped = jnp.concatenate([x1 * cs - x2 * sn_, x2 * cs + x1 * sn_], axis=0)
        sr = _dot(qr, roped.astype(BF16))[0:8, :]
        scores.append((sn + sr) * lax.rsqrt(ss * (1.0 / QK_HEAD) + EPS))
    s_all = jnp.concatenate(scores, axis=1)

    m_old = m_sc[...]
    m_new = jnp.maximum(m_old, jnp.max(s_all, axis=1, keepdims=True))
    alpha = jnp.exp(m_old - m_new)
    p = jnp.exp(s_all - m_new[:, 0:1])
    l_sc[...] = alpha * l_sc[...] + jnp.sum(p, axis=1, keepdims=True)
    acc_sc[...] = alpha[:, 0:1] * acc_sc[...] + _dot(p.astype(BF16), lb_sc[...])
    m_sc[...] = m_new

    @pl.when(t == n_steps - 1)
    def _():
        for cp in page_copies(t, 1 - slot):
            cp.wait()

    @pl.when(j == nj - 1)
    def _():
        s_self = jnp.sum((q16 * k_ref[...].astype(F32))[0:8, :], axis=1, keepdims=True)
        m_o = m_sc[...]
        m_n = jnp.maximum(m_o, s_self)
        al = jnp.exp(m_o - m_n)
        ps = jnp.exp(s_self - m_n)
        l = al * l_sc[...] + ps
        acc = al[:, 0:1] * acc_sc[...] + ps[:, 0:1] * latn_ref[...]
        o_lat = acc / l[:, 0:1]
        res = _dot(o_lat.astype(BF16), wuv_ref[...])
        row = lax.broadcasted_iota(jnp.int32, res.shape, 0)
        lane = lax.broadcasted_iota(jnp.int32, res.shape, 1) // V_HEAD
        o_ref[...] = jnp.sum(jnp.where(row == lane, res, 0.0), axis=0, keepdims=True).astype(o_ref.dtype)


def _mla_decode(page_table, q16, k16, lat_new3, cache_lat, cache_krt, layer, lw, tabs):
    pp = DEC_PAGES
    n = q16.shape[0]
    n_pages = page_table.shape[1]
    nj = n_pages // pp
    cos_t, sin_t = tabs
    fix2 = lambda t, pt: (0, 0)
    seq3 = lambda t, pt: (t // nj, 0, 0)
    tab = lambda t, pt: (0, t % nj)
    weights = [lw["w_uk_t"], lw["g_k_nope"], lw["g_k_rope_col"], lw["ss_select"], lw["w_uv"]]
    in_specs = [pl.BlockSpec((None, 16, HEAD_PAD), seq3), pl.BlockSpec((None, 16, HEAD_PAD), seq3),
                pl.BlockSpec((None, 1, KV_RANK), seq3),
                pl.BlockSpec((QK_ROPE // 2, pp * PAGE_SIZE), tab),
                pl.BlockSpec((QK_ROPE // 2, pp * PAGE_SIZE), tab)]
    in_specs += [pl.BlockSpec(w.shape, fix2) for w in weights]
    in_specs += [pl.BlockSpec(memory_space=pl.ANY), pl.BlockSpec(memory_space=pl.ANY)]
    grid_spec = pltpu.PrefetchScalarGridSpec(
        num_scalar_prefetch=1,
        grid=(n * nj,),
        in_specs=in_specs,
        out_specs=pl.BlockSpec((None, 1, MLA_OUT), seq3),
        scratch_shapes=[pltpu.VMEM((2, pp, PAGE_SIZE, KV_RANK), F32),
                        pltpu.VMEM((2, QK_ROPE, pp * PAGE_SIZE), F32),
                        pltpu.SemaphoreType.DMA((2, 2)),
                        pltpu.VMEM((8, 128), F32), pltpu.VMEM((8, 128), F32), pltpu.VMEM((8, KV_RANK), F32),
                        pltpu.VMEM((DEC_LHS_ROWS, KV_RANK), BF16),
                        pltpu.VMEM((pp * PAGE_SIZE, KV_RANK), BF16)])
    return pl.pallas_call(
        functools.partial(_mla_decode_kernel, pp=pp, nj=nj, layer=layer),
        grid_spec=grid_spec,
        out_shape=jax.ShapeDtypeStruct((n, 1, MLA_OUT), F32),
        compiler_params=_cparams(("arbitrary",), 52),
        name="mla_decode",
    )(page_table, q16, k16, lat_new3, cos_t, sin_t, *weights, cache_lat, cache_krt)


def _dec_mix_kernel(q_ref, k_ref, la_ref, v_ref, gr_ref, u_ref, s_ref, pb_ref, gout_ref, wp_ref, sc_ref,
                    s_o, gla_o, pool_o, pb_o, *, past):
    a = jnp.exp(la_ref[...])
    v4 = v_ref[...]
    vexp = jnp.concatenate([jnp.broadcast_to(v4[h:h + 1, :], (GLA_DK, GLA_DV)) for h in range(GLA_HEADS)],
                           axis=0)
    s_new = a * s_ref[...] + k_ref[...] * vexp
    s_o[...] = s_new
    prod = (q_ref[...] * (GLA_DK ** -0.5)) * s_new
    gr = gr_ref[...]
    gout = gout_ref[...]
    for h in range(GLA_HEADS):
        o = jnp.sum(prod[h * GLA_DK:(h + 1) * GLA_DK, :], axis=0, keepdims=True)
        on = _rms(o, gout)
        g = gr[h:h + 1, :]
        gla_o[h:h + 1, :] = (on * (g * _sigmoid(g))).astype(gla_o.dtype)

    st = pb_ref[...]
    u = u_ref[...]
    ridx = lax.broadcasted_iota(jnp.int32, st.shape, 0)
    parts = []
    for w in POOL_WINDOWS:
        tot = u + jnp.sum(jnp.where(ridx >= POOL_BUF + 1 - w, st, 0.0), axis=0, keepdims=True)
        parts.append(tot / float(min(w, past + 1)) - u)
    pooled = jnp.broadcast_to(_pool_select(parts), (8, POOL_WIDTH)).astype(BF16)
    y = _dot(pooled, wp_ref[...])[0:1, :] * sc_ref[...]
    pool_o[...] = y.astype(pool_o.dtype)
    pb_o[0:POOL_BUF - 1, :] = st[1:POOL_BUF, :]
    pb_o[POOL_BUF - 1:POOL_BUF, :] = u


def _dec_mix(gq, gk, la, gv, gr, u, state_gla4, state_pool, layer, lw, past):
    n = gq.shape[0]
    col = lambda x: x.reshape(n, GLA_QK, 1)
    b3 = lambda b: (b, 0, 0)
    fix = lambda b: (0, 0)
    return pl.pallas_call(
        functools.partial(_dec_mix_kernel, past=past),
        grid=(n,),
        in_specs=[pl.BlockSpec((None, GLA_QK, 1), b3)] * 3
        + [pl.BlockSpec((None, GLA_HEADS, GLA_DV), b3)] * 2
        + [pl.BlockSpec((None, 1, POOL_WIDTH), b3),
           pl.BlockSpec((None, None, GLA_QK, GLA_DV), lambda b: (b, layer, 0, 0)),
           pl.BlockSpec((None, None, POOL_BUF, POOL_WIDTH), lambda b: (b, layer, 0, 0)),
           pl.BlockSpec((1, GLA_DV), fix), pl.BlockSpec((POOL_WIDTH, POOL_WIDTH), fix),
           pl.BlockSpec((1, POOL_WIDTH), fix)],
        out_specs=[pl.BlockSpec((None, GLA_QK, GLA_DV), b3), pl.BlockSpec((None, GLA_HEADS, GLA_DV), b3),
                   pl.BlockSpec((None, 1, POOL_WIDTH), b3), pl.BlockSpec((None, POOL_BUF, POOL_WIDTH), b3)],
        out_shape=[jax.ShapeDtypeStruct((n, GLA_QK, GLA_DV), F32),
                   jax.ShapeDtypeStruct((n, GLA_HEADS, GLA_DV), F32),
                   jax.ShapeDtypeStruct((n, 1, POOL_WIDTH), F32),
                   jax.ShapeDtypeStruct((n, POOL_BUF, POOL_WIDTH), F32)],
        compiler_params=_cparams(("parallel",), 32),
        name="dec_mix",
    )(col(gq), col(gk), col(la), gv.reshape(n, GLA_HEADS, GLA_DV), gr.reshape(n, GLA_HEADS, GLA_DV),
      u.reshape(n, 1, POOL_WIDTH), state_gla4, state_pool, lw["g_gla_out64"], lw["w_pool"], lw["pool_scale"])


def _post_a_kernel(x_ref, mla_ref, gla_ref, pool_ref, wo_ref, gxa_ref, wxq_ref, gxq_ref, x1_ref, q_ref):
    x1 = _mix_out(x_ref[...], mla_ref[...], gla_ref[...], pool_ref[...], wo_ref)
    x1_ref[...] = x1
    q_ref[...] = _xa_query(x1, gxa_ref[...], wxq_ref[...], gxq_ref[...])


def _post_a(x2, mla, gla, pool, lw):
    m = x2.shape[0]
    return pl.pallas_call(
        _post_a_kernel,
        out_shape=[jax.ShapeDtypeStruct((m, D_MODEL), F32), jax.ShapeDtypeStruct((m, D_MODEL), F32)],
        compiler_params=pltpu.CompilerParams(vmem_limit_bytes=32 * MIB),
        name="post_a",
    )(x2, mla, gla, pool, lw["w_out"], lw["norm_xa"], lw["w_xq"], lw["g_xq"])


XA_HALVES = XA_HEAD // 128
XA_ROWS = XA_HALVES * XA_HEADS


def _xa_decode_kernel(q_ref, mk_ref, mv_ref, o_ref):
    k3 = mk_ref[...].reshape(N_MEM, XA_ROWS, 128)
    v3 = mv_ref[...].reshape(N_MEM, XA_ROWS, 128)
    s = jnp.sum(k3 * q_ref[...][None], axis=-1, keepdims=True)
    s = s[:, 0:XA_HEADS] + s[:, XA_HEADS:XA_ROWS]
    s = jnp.concatenate([s, s], axis=1)
    e = jnp.exp(s - jnp.max(s, axis=0, keepdims=True))
    p = e / jnp.sum(e, axis=0, keepdims=True)
    o_ref[...] = jnp.sum(p * v3, axis=0)


def _xa_decode(q3, cache_k, cache_v, layer):
    n = q3.shape[0]
    depth = cache_k.shape[1]
    view = lambda c: c.reshape(n, depth, N_MEM, XA_HEADS, XA_HALVES, 128).transpose(0, 1, 2, 4, 3, 5).reshape(
        n, depth, N_MEM * XA_ROWS, 128)
    q8 = q3.reshape(n, XA_HEADS, XA_HALVES, 128).transpose(0, 2, 1, 3).reshape(n, XA_ROWS, 128)
    b3 = lambda b: (b, 0, 0)
    mem = lambda b: (b, layer, 0, 0)
    o = pl.pallas_call(
        _xa_decode_kernel,
        grid=(n,),
        in_specs=[pl.BlockSpec((None, XA_ROWS, 128), b3),
                  pl.BlockSpec((None, None, N_MEM * XA_ROWS, 128), mem),
                  pl.BlockSpec((None, None, N_MEM * XA_ROWS, 128), mem)],
        out_specs=pl.BlockSpec((None, XA_ROWS, 128), b3),
        out_shape=jax.ShapeDtypeStruct((n, XA_ROWS, 128), F32),
        compiler_params=_cparams(("parallel",), 32),
        name="xa_decode",
    )(q8, view(cache_k), view(cache_v))
    return o.reshape(n, XA_HALVES, XA_HEADS, 128).transpose(0, 2, 1, 3).reshape(n, 1, D_MODEL)


def _post_b_kernel(x1_ref, o_ref, wxo_ref, x2_ref):
    x2_ref[...] = x1_ref[...] + _dot(o_ref[...].astype(BF16), wxo_ref[...])


def _post_b(x1, o, lw):
    return pl.pallas_call(
        _post_b_kernel,
        out_shape=jax.ShapeDtypeStruct(x1.shape, F32),
        compiler_params=pltpu.CompilerParams(vmem_limit_bytes=32 * MIB),
        name="post_b",
    )(x1, o, lw["w_xo"])


def _rope_tables_rows(pos):
    half = QK_ROPE // 2
    inv = ROPE_THETA ** (-jnp.arange(half, dtype=F32) / half)
    ang = pos.astype(F32)[:, None] * inv[None, :]
    cos, sin = jnp.cos(ang), jnp.sin(ang)
    n = pos.shape[0]
    c = jnp.concatenate([jnp.ones((n, QK_NOPE), F32), cos, cos, jnp.ones((n, HEAD_PAD - QK_HEAD), F32)], axis=1)
    s = jnp.concatenate([jnp.zeros((n, QK_NOPE), F32), sin, sin, jnp.zeros((n, HEAD_PAD - QK_HEAD), F32)],
                        axis=1)
    return c, s


def _rope_tables_cols(pos):
    half = QK_ROPE // 2
    inv = ROPE_THETA ** (-jnp.arange(half, dtype=F32) / half)
    ang = pos.astype(F32)[:, None] * inv[None, :]
    return jnp.cos(ang).T, jnp.sin(ang).T


def _pad_heads(w, width):
    pad = [(0, 0)] * (w.ndim - 1) + [(0, HEAD_PAD - width)]
    w = jnp.pad(w, pad)
    return w.reshape(w.shape[:-2] + (w.shape[-2] * HEAD_PAD,))


def _layer_weights(l, p):
    w_in = p["w_in"][l]
    sizes = (Q_RANK, KV_RANK, QK_ROPE, GLA_QK, GLA_QK, GLA_OUT, GLA_GATE_RANK, GLA_OUT, POOL_WIDTH)
    offs = [0]
    for s in sizes:
        offs.append(offs[-1] + s)
    piece = lambda i: w_in[:, offs[i]:offs[i + 1]]
    w_in_p = jnp.concatenate([piece(0), piece(1), piece(3), piece(4), piece(5), piece(7), piece(8), piece(2),
                              piece(6), jnp.zeros((D_MODEL, IN_PAD - sum(sizes)), F32)], axis=1)
    g_q = _pad_heads(jnp.broadcast_to(p["g_qk_q"][l], (MLA_HEADS, QK_HEAD)), QK_HEAD)
    g_k = _pad_heads(jnp.broadcast_to(p["g_qk_k"][l], (MLA_HEADS, QK_HEAD)), QK_HEAD)
    place = jnp.zeros((128, QK_W), F32)
    j = jnp.arange(QK_ROPE)
    for h in range(MLA_HEADS):
        place = place.at[j, h * HEAD_PAD + QK_NOPE + j].set(1.0)
    w_gate = jnp.zeros((128, GLA_QK), F32).at[QK_ROPE:QK_ROPE + GLA_GATE_RANK].set(p["w_gate_up"][l])
    half = QK_ROPE // 2

    def partner(w):
        w4 = w.reshape(w.shape[:-1] + (MLA_HEADS, HEAD_PAD))
        x1 = w4[..., QK_NOPE:QK_NOPE + half]
        x2 = w4[..., QK_NOPE + half:QK_HEAD]
        out = jnp.zeros_like(w4).at[..., QK_NOPE:QK_NOPE + half].set(x2).at[..., QK_NOPE + half:QK_HEAD].set(x1)
        return out.reshape(w.shape)

    sign = jnp.tile(jnp.concatenate([jnp.zeros((QK_NOPE,), F32), -jnp.ones((half,), F32), jnp.ones((half,), F32),
                                     jnp.zeros((HEAD_PAD - QK_HEAD,), F32)]), MLA_HEADS)
    w_uq_pad = _pad_heads(p["w_uq"][l], QK_HEAD)
    w_uv2 = p["w_uv"][l].reshape(KV_RANK, MLA_OUT)
    place_both = jnp.concatenate([place, partner(place) * sign], axis=1)
    wp = p["w_pool"][l]
    w_pool = jnp.zeros((POOL_WIDTH, POOL_WIDTH), F32)
    for g in range(len(POOL_WINDOWS)):
        w_pool = w_pool.at[g * POOL_GROUP:(g + 1) * POOL_GROUP, g * POOL_GROUP:(g + 1) * POOL_GROUP].set(wp[g])
    blk = jnp.arange(GLA_OUT) // GLA_DV
    row = lambda v: v.reshape(1, -1).astype(F32)
    bf = lambda v: v.astype(BF16)
    w_uk2 = p["w_uk"][l].reshape(KV_RANK, MLA_HEADS * QK_NOPE)
    g_k_nope = jnp.zeros((1, HEAD_PAD), F32).at[0, :QK_NOPE].set(p["g_qk_k"][l][:QK_NOPE])
    col = jnp.arange(DEC_SS_ROWS)[None, :] // 8
    hrow = jnp.arange(8)[:, None]
    ss_select = ((hrow < MLA_HEADS) & ((col == hrow) | (col == MLA_HEADS))).astype(F32)
    return {
        "norm_mix": row(p["norm_mix"][l]), "w_in": bf(w_in_p), "g_q_lat": row(p["g_q_lat"][l]),
        "w_uq2": bf(jnp.concatenate([w_uq_pad, partner(w_uq_pad) * sign], axis=1)),
        "g_kv_lat": row(p["g_kv_lat"][l]),
        "w_ukv": bf(jnp.concatenate([_pad_heads(p["w_uk"][l], QK_NOPE), w_uv2], axis=1)), "w_uv": bf(w_uv2),
        "g_qk_q": row(g_q), "g_qk_k": row(g_k), "place": bf(jnp.tile(place_both, (2, 1))), "w_gate": bf(w_gate),
        "g_q_rot": row(partner(g_q)), "g_k_rot": row(partner(g_k)),
        "b_gate": row(p["b_gate"][l]),
        "e64": bf((blk[:, None] == blk[None, :]).astype(F32)),
        "g_gla_out": row(jnp.tile(p["g_gla_out"][l], GLA_HEADS)), "g_gla_out64": row(p["g_gla_out"][l]),
        "w_pool": bf(w_pool), "pool_scale": row(p["pool_scale"][l]),
        "w_out": bf(p["w_out"][l]), "norm_xa": row(p["norm_xa"][l]), "w_xq": bf(p["w_xq"][l]),
        "g_xq": row(jnp.tile(p["g_xq"][l], XA_HEADS)), "w_xo": bf(p["w_xo"][l]),
        "norm_mlp": row(p["norm_mlp"][l]), "w_ff1": p["w_ff1_bf16"], "w_ff2": p["w_ff2_bf16"], "layer": l,
        "w_uk_t": bf(w_uk2.T), "g_k_nope": g_k_nope,
        "g_k_rope_col": jnp.broadcast_to(p["g_qk_k"][l][QK_NOPE:, None], (QK_ROPE, DEC_CHUNK)).astype(F32),
        "ss_select": bf(ss_select),
    }


def _prompt_layer(x2, layer, mk4, mv4, lw, ropes, b, t):
    lat, kr, q, k, v, gq, gk, gv, la, gr, u = _in_proj(x2, lw, ropes, ROW_TILE)
    r3 = lambda a: a.reshape(b, t, a.shape[-1])
    mla = _mla_prompt(r3(q), r3(k), r3(v), ROW_TILE).reshape(b * t, MLA_OUT)
    gla, st = _gla_prompt(r3(gq), r3(gk), r3(gv), r3(la), r3(gr), lw)
    pool = _pool_prompt(r3(u), lw, t).reshape(b * t, POOL_WIDTH)
    x2 = _post_prompt(x2, mla, gla.reshape(b * t, GLA_OUT), pool, mk4, mv4, layer, lw, t, ROW_TILE)
    x2 = _mlp(x2, lw, ROW_TILE)
    gla_state = jnp.stack([st[:, h * GLA_DV:(h + 1) * GLA_DV, h * GLA_DK:(h + 1) * GLA_DK]
                           for h in range(GLA_HEADS)], axis=1).transpose(0, 1, 3, 2)
    outs = (lat.reshape(b, t, KV_RANK), kr.reshape(b, t, QK_ROPE), gla_state,
            r3(u)[:, t - POOL_BUF:])
    return x2, outs


def _sample_layer(x2, layer, lw, ropes, tabs, page_table, cache_lat, cache_krt, state_gla4, state_pool,
                  cache_mk, cache_mv, past):
    n = x2.shape[0]
    lat, kr, q, k, v, gq, gk, gv, la, gr, u = _in_proj(x2, lw, ropes, n)
    pad16 = lambda a: jnp.pad(a.reshape(n, MLA_HEADS, HEAD_PAD), ((0, 0), (0, 16 - MLA_HEADS), (0, 0)))
    mla = _mla_decode(page_table, pad16(q), pad16(k), lat.reshape(n, 1, KV_RANK), cache_lat, cache_krt, layer,
                      lw, tabs).reshape(n, MLA_OUT)
    s_new, gla, pool, pb = _dec_mix(gq, gk, la, gv, gr, u, state_gla4, state_pool, layer, lw, past)
    x1, qx = _post_a(x2, mla, gla.reshape(n, GLA_OUT), pool.reshape(n, POOL_WIDTH), lw)
    ox = _xa_decode(qx.reshape(n, 1, D_MODEL), cache_mk, cache_mv, layer).reshape(n, D_MODEL)
    x2 = _mlp(_post_b(x1, ox, lw), lw, n)
    outs = (lat.reshape(n, 1, KV_RANK), kr.reshape(n, 1, QK_ROPE),
            s_new.reshape(n, GLA_HEADS, GLA_DK, GLA_DV), pb)
    return x2, outs


def kernel(x_prompt, x_sample, mem_prompt, cache_mla_latent, cache_mla_krope, state_gla, state_pool,
           cache_mem_k, cache_mem_v, page_table, norm_mix, w_in, g_q_lat, w_uq, g_kv_lat, w_uk, w_uv,
           g_qk_q, g_qk_k, w_gate_up, b_gate, g_gla_out, w_pool, pool_scale, w_out, norm_xa, norm_mem,
           w_xq, w_xk, w_xv, g_xq, g_xk, w_xo, norm_mlp, w_ff1, w_ff2):
    params = dict(norm_mix=norm_mix, w_in=w_in, g_q_lat=g_q_lat, w_uq=w_uq, g_kv_lat=g_kv_lat, w_uk=w_uk,
                  w_uv=w_uv, g_qk_q=g_qk_q, g_qk_k=g_qk_k, w_gate_up=w_gate_up, b_gate=b_gate,
                  g_gla_out=g_gla_out, w_pool=w_pool, pool_scale=pool_scale, w_out=w_out, norm_xa=norm_xa,
                  norm_mem=norm_mem, w_xq=w_xq, w_xk=w_xk, w_xv=w_xv, g_xq=g_xq, g_xk=g_xk, w_xo=w_xo,
                  norm_mlp=norm_mlp, w_ff1_bf16=w_ff1.astype(BF16), w_ff2_bf16=w_ff2.astype(BF16))
    b, t, _ = x_prompt.shape
    n, dec_seq, _ = x_sample.shape
    depth = norm_mix.shape[0]
    n_pages = page_table.shape[1]
    past = n_pages * PAGE_SIZE

    ropes_p = _rope_tables_rows(jnp.arange(t, dtype=jnp.int32))
    ropes_s = _rope_tables_rows(jnp.full((n,), past, jnp.int32))
    tabs = _rope_tables_cols(jnp.arange(past, dtype=jnp.int32))

    yp = x_prompt.reshape(b * t, D_MODEL)
    ys = x_sample.reshape(n * dec_seq, D_MODEL)
    mem2 = mem_prompt.reshape(b * N_MEM, D_MODEL)
    state_gla4 = state_gla.reshape(n, depth, GLA_QK, GLA_DV)
    cache_krt = cache_mla_krope.transpose(0, 1, 3, 2)

    mk, mv, mkb, mvb = _memory_kv(mem2, norm_mem.reshape(depth, 1, D_MODEL), w_xk.astype(BF16),
                                  w_xv.astype(BF16), jnp.tile(g_xk, (1, XA_HEADS)).reshape(depth, 1, D_MODEL))
    mk4 = mkb.reshape(depth, b, N_MEM, D_MODEL)
    mv4 = mvb.reshape(depth, b, N_MEM, D_MODEL)

    p_outs, s_outs = [], []
    for l in range(depth):
        lw = _layer_weights(l, params)
        yp, po = _prompt_layer(yp, l, mk4, mv4, lw, ropes_p, b, t)
        ys, so = _sample_layer(ys, l, lw, ropes_s, tabs, page_table, cache_mla_latent, cache_krt,
                               state_gla4, state_pool, cache_mem_k, cache_mem_v, past)
        p_outs.append(po)
        s_outs.append(so)

    stack = lambda outs, i: jnp.stack([o[i] for o in outs], axis=1)
    return (yp.reshape(b, t, D_MODEL), ys.reshape(n, dec_seq, D_MODEL),
            stack(p_outs, 0), stack(p_outs, 1), stack(p_outs, 2), stack(p_outs, 3),
            mk.reshape(b, depth, N_MEM, XA_HEADS, XA_HEAD), mv.reshape(b, depth, N_MEM, XA_HEADS, XA_HEAD),
            stack(s_outs, 0), stack(s_outs, 1), stack(s_outs, 2), stack(s_outs, 3))
```

```python
import functools

import jax
import jax.numpy as jnp
from jax import lax
from jax.experimental import pallas as pl
from jax.experimental.pallas import tpu as pltpu

F32 = jnp.float32
BF16 = jnp.bfloat16

D_MODEL = 1024
MLA_HEADS = 4
QK_NOPE = 64
QK_ROPE = 32
QK_HEAD = QK_NOPE + QK_ROPE
V_HEAD = 128
Q_RANK = 384
KV_RANK = 256
ROPE_THETA = 10000.0
GLA_HEADS = 4
GLA_DK = 32
GLA_DV = 64
GLA_GATE_RANK = 16
GLA_TAU = 16.0
GLA_CHUNK = 64
GLA_GROUP = 256
POOL_WINDOWS = (2, 4, 8, 16)
POOL_GROUP = 64
POOL_WIDTH = POOL_GROUP * len(POOL_WINDOWS)
POOL_BUF = max(POOL_WINDOWS) - 1
MLA_OUT = MLA_HEADS * V_HEAD
GLA_OUT = GLA_HEADS * GLA_DV
N_MEM = 256
XA_HEADS = 4
XA_HEAD = D_MODEL // XA_HEADS
D_FF = 4 * D_MODEL
PAGE_SIZE = 128
EPS = 1e-6

HEAD_PAD = 128
QK_W = MLA_HEADS * HEAD_PAD
GLA_QK = GLA_HEADS * GLA_DK
IN_PAD = 1792
MISC0 = 1664

ROW_TILE = 512
MIB = 1024 * 1024


def _cparams(sem, vmem_mib):
    return pltpu.CompilerParams(dimension_semantics=sem, vmem_limit_bytes=vmem_mib * MIB)


def _rms(x, g):
    y = x * lax.rsqrt(jnp.mean(x * x, axis=-1, keepdims=True) + EPS)
    return y * g


def _head_rms(x, g, nh, width, count):
    outs = []
    for h in range(nh):
        p = x[:, h * width:(h + 1) * width]
        ms = jnp.sum(p * p, axis=-1, keepdims=True) * (1.0 / count)
        outs.append(p * lax.rsqrt(ms + EPS) * g[:, h * width:(h + 1) * width])
    return jnp.concatenate(outs, axis=1)


def _dot(a, b):
    return jnp.dot(a, b, preferred_element_type=F32)


def _dot_nt(a, b):
    return lax.dot_general(a, b, (((1,), (1,)), ((), ())), preferred_element_type=F32)


def _split_bf16(x):
    hi = x.astype(BF16)
    lo = (x - hi.astype(F32)).astype(BF16)
    return hi, lo


def _dot_split(x, w):
    hi, lo = _split_bf16(x)
    return _dot(hi, w) + _dot(lo, w)


def _norm_rope(x, xrot, g, grot, c, s):
    outs = []
    for h in range(MLA_HEADS):
        sl = slice(h * HEAD_PAD, (h + 1) * HEAD_PAD)
        p = x[:, sl]
        r = lax.rsqrt(jnp.sum(p * p, axis=-1, keepdims=True) * (1.0 / QK_HEAD) + EPS)
        outs.append((p * r * g[:, sl]) * c + (xrot[:, sl] * r * grot[:, sl]) * s)
    return jnp.concatenate(outs, axis=1)


def _log_sigmoid(x):
    return -(jnp.maximum(-x, 0.0) + jnp.log1p(jnp.exp(-jnp.abs(x))))


def _sigmoid(x):
    return 1.0 / (1.0 + jnp.exp(-x))


def _memkv_kernel(mem_ref, g_ref, wk_ref, wv_ref, gk_ref, k_ref, v_ref, kb_ref, vb_ref):
    m = _rms(mem_ref[...], g_ref[...]).astype(BF16)
    k = _head_rms(_dot(m, wk_ref[...]), gk_ref[...], XA_HEADS, XA_HEAD, XA_HEAD)
    v = _dot(m, wv_ref[...])
    k_ref[...] = k.reshape(k_ref.shape)
    v_ref[...] = v.reshape(v_ref.shape)
    kb_ref[...] = k.astype(BF16)
    vb_ref[...] = v.astype(BF16)


def _memory_kv(mem2, g, wk, wv, gk):
    m = mem2.shape[0]
    depth = wk.shape[0]
    tm = ROW_TILE
    nb = tm // N_MEM
    vec = lambda l, i: (l, 0, 0)
    return pl.pallas_call(
        _memkv_kernel,
        grid=(depth, m // tm),
        in_specs=[pl.BlockSpec((tm, D_MODEL), lambda l, i: (i, 0)), pl.BlockSpec((None, 1, D_MODEL), vec),
                  pl.BlockSpec((None, D_MODEL, D_MODEL), vec), pl.BlockSpec((None, D_MODEL, D_MODEL), vec),
                  pl.BlockSpec((None, 1, D_MODEL), vec)],
        out_specs=[pl.BlockSpec((nb, None, N_MEM, D_MODEL), lambda l, i: (i, l, 0, 0))] * 2
        + [pl.BlockSpec((None, tm, D_MODEL), lambda l, i: (l, i, 0))] * 2,
        out_shape=[jax.ShapeDtypeStruct((m // N_MEM, depth, N_MEM, D_MODEL), F32)] * 2
        + [jax.ShapeDtypeStruct((depth, m, D_MODEL), BF16)] * 2,
        compiler_params=_cparams(("parallel", "parallel"), 40),
        name="memory_kv",
    )(mem2, g, wk, wv, gk)


def _inproj_kernel(x_ref, gmix_ref, win_ref, gql_ref, wuq_ref, gkv_ref, wukv_ref,
                   gq_ref, gk_ref, place_ref, wg_ref, bg_ref, gqr_ref, gkr_ref,
                   c_ref, s_ref, lat_ref, kr_ref, q_ref, k_ref, v_ref, gq_o, gk_o, gv_o, la_o, gr_o, u_o):
    h = _rms(x_ref[...], gmix_ref[...]).astype(BF16)
    z = _dot(h, win_ref[...])
    c_q = z[:, 0:384]
    c_kv = z[:, 384:640]
    misc = z[:, MISC0:MISC0 + 128]
    gq_o[...] = z[:, 640:768]
    gk_o[...] = z[:, 768:896]
    gv_o[...] = z[:, 896:1152]
    gr_o[...] = z[:, 1152:1408]
    u_o[...] = z[:, 1408:1664]
    kr_ref[...] = misc[:, 0:QK_ROPE]

    lat = _rms(c_kv, gkv_ref[...])
    lat_ref[...] = lat
    latb = lat.astype(BF16)
    c, s = c_ref[...], s_ref[...]

    q2 = _dot(_rms(c_q, gql_ref[...]).astype(BF16), wuq_ref[...])
    q = _norm_rope(q2[:, :QK_W], q2[:, QK_W:], gq_ref[...], gqr_ref[...], c, s)
    q_ref[...] = (q * (QK_HEAD ** -0.5)).astype(BF16)

    placed = _dot(jnp.concatenate(_split_bf16(misc), axis=1), place_ref[...])
    kv = _dot(latb, wukv_ref[...])
    k = kv[:, :QK_W] + placed[:, :QK_W]
    k_ref[...] = _norm_rope(k, placed[:, QK_W:], gk_ref[...], gkr_ref[...], c, s).astype(BF16)
    v_ref[...] = kv[:, QK_W:].astype(BF16)

    gate = _dot(misc.astype(BF16), wg_ref[...]) + bg_ref[...]
    la_o[...] = _log_sigmoid(gate) * (1.0 / GLA_TAU)


def _in_proj(x2, lw, ropes, tm):
    m = x2.shape[0]
    row = lambda i: (i, 0)
    fix = lambda i: (0, 0)
    c, s = ropes
    if c.shape[0] == m:
        rope_map = row
    else:
        nb = c.shape[0] // tm
        rope_map = lambda i: (i % nb, 0)
    weights = [lw["norm_mix"], lw["w_in"], lw["g_q_lat"], lw["w_uq2"], lw["g_kv_lat"], lw["w_ukv"],
               lw["g_qk_q"], lw["g_qk_k"], lw["place"], lw["w_gate"], lw["b_gate"],
               lw["g_q_rot"], lw["g_k_rot"]]
    in_specs = [pl.BlockSpec((tm, D_MODEL), row)]
    in_specs += [pl.BlockSpec(w.shape, fix) for w in weights]
    in_specs += [pl.BlockSpec((tm, HEAD_PAD), rope_map)] * 2
    widths = [(KV_RANK, F32), (QK_ROPE, F32), (QK_W, BF16), (QK_W, BF16), (MLA_OUT, BF16),
              (GLA_QK, F32), (GLA_QK, F32), (GLA_OUT, F32), (GLA_QK, F32), (GLA_OUT, F32),
              (POOL_WIDTH, F32)]
    return pl.pallas_call(
        _inproj_kernel,
        grid=(m // tm,),
        in_specs=in_specs,
        out_specs=[pl.BlockSpec((tm, w), row) for w, _ in widths],
        out_shape=[jax.ShapeDtypeStruct((m, w), dt) for w, dt in widths],
        compiler_params=_cparams(("parallel",), 48),
        name="in_proj",
    )(x2, *weights, c, s)


FLASH_HEADS = 2


def _flash_kernel(q_ref, k_ref, v_ref, o_ref, *, tq):
    qi = pl.program_id(2)
    nh = q_ref.shape[1] // HEAD_PAD
    qs = [q_ref[:, h * HEAD_PAD:(h + 1) * HEAD_PAD] for h in range(nh)]

    def block(j, carry, masked):
        start = pl.multiple_of(j * tq, tq)
        out = []
        for h in range(nh):
            m, l, acc = carry[h]
            k = k_ref[pl.ds(start, tq), h * HEAD_PAD:(h + 1) * HEAD_PAD]
            v = v_ref[pl.ds(start, tq), h * V_HEAD:(h + 1) * V_HEAD]
            s = _dot_nt(qs[h], k)
            if masked:
                rows = lax.broadcasted_iota(jnp.int32, (tq, tq), 0)
                cols = lax.broadcasted_iota(jnp.int32, (tq, tq), 1)
                s = jnp.where(cols <= rows, s, -jnp.inf)
            m_new = jnp.maximum(m, jnp.max(s, axis=-1, keepdims=True))
            alpha = jnp.exp(m - m_new)
            p = jnp.exp(s - m_new)
            l = alpha * l + jnp.sum(p, axis=-1, keepdims=True)
            acc = alpha * acc + _dot(p.astype(BF16), v)
            out.append((m_new, l, acc))
        return tuple(out)

    init = tuple((jnp.full((tq, 1), -jnp.inf, F32), jnp.zeros((tq, 1), F32), jnp.zeros((tq, V_HEAD), F32))
                 for _ in range(nh))
    carry = lax.fori_loop(0, qi, lambda j, cr: block(j, cr, False), init)
    res = block(qi, carry, True)
    o_ref[...] = jnp.concatenate([acc / l for _, l, acc in res], axis=1).astype(o_ref.dtype)


def _mla_prompt(q3, k3, v3, tq):
    b, t, _ = q3.shape
    hp = FLASH_HEADS
    qmap = lambda bi, h, qi: (bi, qi, h)
    kmap = lambda bi, h, qi: (bi, 0, h)
    return pl.pallas_call(
        functools.partial(_flash_kernel, tq=tq),
        grid=(b, MLA_HEADS // hp, t // tq),
        in_specs=[pl.BlockSpec((None, tq, hp * HEAD_PAD), qmap), pl.BlockSpec((None, t, hp * HEAD_PAD), kmap),
                  pl.BlockSpec((None, t, hp * V_HEAD), kmap)],
        out_specs=pl.BlockSpec((None, tq, hp * V_HEAD), qmap),
        out_shape=jax.ShapeDtypeStruct((b, t, MLA_OUT), BF16),
        compiler_params=_cparams(("parallel", "parallel", "arbitrary"), 48),
        name="mla_prompt",
    )(q3, k3, v3)


def _gla_kernel(q_ref, k_ref, v_ref, la_ref, gr_ref, tri_ref, ones_ref, e64_ref, gout_ref, o_ref, st_ref, *, t):
    c = GLA_CHUNK
    gt = GLA_GROUP
    tri = tri_ref[...]
    ones_bd = ones_ref[...]
    e64 = e64_ref[...]
    gout = gout_ref[...]
    lane_qk = lax.broadcasted_iota(jnp.int32, (gt, GLA_QK), 1) // GLA_DK
    lane_v = lax.broadcasted_iota(jnp.int32, (gt, GLA_OUT), 1) // GLA_DV
    row_chunk = lax.broadcasted_iota(jnp.int32, (gt, GLA_QK), 0) // c
    rr = lax.broadcasted_iota(jnp.int32, (gt, gt), 0)
    cc = lax.broadcasted_iota(jnp.int32, (gt, gt), 1)
    causal = (cc <= rr) & (rr // c == cc // c)
    bd = (lax.broadcasted_iota(jnp.int32, (GLA_OUT, GLA_QK), 0) // GLA_DV
          == lax.broadcasted_iota(jnp.int32, (GLA_OUT, GLA_QK), 1) // GLA_DK)

    def group(gi, st):
        r0 = pl.multiple_of(gi * gt, gt)
        q = q_ref[pl.ds(r0, gt), :] * (GLA_DK ** -0.5)
        k = k_ref[pl.ds(r0, gt), :]
        v = v_ref[pl.ds(r0, gt), :]
        a_pair = jnp.concatenate(_split_bf16(la_ref[pl.ds(r0, gt), :]), axis=1)
        b2 = _dot(tri, a_pair)
        t2 = _dot(ones_bd, a_pair)
        b = b2[:, :GLA_QK] + b2[:, GLA_QK:]
        tot = t2[:, :GLA_QK] + t2[:, GLA_QK:]
        qd = q * jnp.exp(b)
        qdb = qd.astype(BF16)
        kd = (k * jnp.exp(-b)).astype(BF16)
        k2 = k * jnp.exp(tot - b)
        decay = jnp.exp(tot)
        vb = v.astype(BF16)
        vt = v.T.astype(BF16)
        o = jnp.zeros((gt, GLA_OUT), F32)
        for h in range(GLA_HEADS):
            qh = jnp.where(lane_qk == h, qd, 0.0).astype(BF16)
            att = jnp.where(causal, _dot_nt(qh, kd), 0.0)
            o = o + jnp.where(lane_v == h, _dot(att.astype(BF16), vb), 0.0)
        k2_blocks = jnp.concatenate([jnp.where(row_chunk == ci, k2, 0.0) for ci in range(gt // c)], axis=1)
        incr = _dot(vt, k2_blocks.astype(BF16))
        inter = []
        for ci in range(gt // c):
            inter.append(_dot_nt(qdb[ci * c:(ci + 1) * c, :], st.astype(BF16)))
            st = st * decay[ci * c:ci * c + 1, :] + jnp.where(bd, incr[:, ci * GLA_QK:(ci + 1) * GLA_QK], 0.0)
        o = o + jnp.concatenate(inter, axis=0)
        ms = _dot_split(o * o, e64) * (1.0 / GLA_DV)
        on = o * lax.rsqrt(ms + EPS) * gout
        gr = gr_ref[pl.ds(r0, gt), :]
        o_ref[pl.ds(r0, gt), :] = (on * (gr * _sigmoid(gr))).astype(o_ref.dtype)
        return st

    st_ref[...] = lax.fori_loop(0, t // gt, group, jnp.zeros((GLA_OUT, GLA_QK), F32), unroll=4)


def _gla_prompt(gq3, gk3, gv3, la3, gr3, lw):
    b, t, _ = gq3.shape
    bmap = lambda i: (i, 0, 0)
    fix = lambda i: (0, 0)
    chunk_id = jnp.arange(GLA_GROUP) // GLA_CHUNK
    same = chunk_id[:, None] == chunk_id[None, :]
    tri = (same & (jnp.arange(GLA_GROUP)[None, :] <= jnp.arange(GLA_GROUP)[:, None])).astype(BF16)
    return pl.pallas_call(
        functools.partial(_gla_kernel, t=t),
        grid=(b,),
        in_specs=[pl.BlockSpec((None, t, GLA_QK), bmap), pl.BlockSpec((None, t, GLA_QK), bmap),
                  pl.BlockSpec((None, t, GLA_OUT), bmap), pl.BlockSpec((None, t, GLA_QK), bmap),
                  pl.BlockSpec((None, t, GLA_OUT), bmap), pl.BlockSpec((GLA_GROUP, GLA_GROUP), fix),
                  pl.BlockSpec((GLA_GROUP, GLA_GROUP), fix),
                  pl.BlockSpec((GLA_OUT, GLA_OUT), fix), pl.BlockSpec((1, GLA_OUT), fix)],
        out_specs=[pl.BlockSpec((None, t, GLA_OUT), bmap), pl.BlockSpec((None, GLA_OUT, GLA_QK), bmap)],
        out_shape=[jax.ShapeDtypeStruct((b, t, GLA_OUT), BF16),
                   jax.ShapeDtypeStruct((b, GLA_OUT, GLA_QK), F32)],
        compiler_params=_cparams(("parallel",), 40),
        name="gla_prompt",
    )(gq3, gk3, gv3, la3, gr3, tri, same.astype(BF16), lw["e64"], lw["g_gla_out"])


def _pool_select(parts):
    lane = lax.broadcasted_iota(jnp.int32, parts[0].shape, 1) // POOL_GROUP
    out = parts[-1]
    for g in range(len(parts) - 2, -1, -1):
        out = jnp.where(lane == g, parts[g], out)
    return out


def _pool_kernel(u_ref, halo_ref, w_ref, sc_ref, o_ref, *, tp):
    i = pl.program_id(1)
    u = u_ref[...]
    halo = jnp.where(i == 0, 0.0, halo_ref[...])
    ext = jnp.concatenate([halo, u], axis=0)
    hb = halo.shape[0]
    pos = i * tp + lax.broadcasted_iota(jnp.int32, (tp, 1), 0)
    sums = ext
    parts = []
    span = 1
    for w in POOL_WINDOWS:
        while span < w:
            sums = sums + pltpu.roll(sums, span, 0)
            span *= 2
        cnt = jnp.minimum(w, pos + 1).astype(F32)
        parts.append(sums[hb:, :] / cnt - u)
    pooled = _pool_select(parts)
    o_ref[...] = (_dot(pooled.astype(BF16), w_ref[...]) * sc_ref[...]).astype(o_ref.dtype)


def _pool_prompt(u3, lw, tp):
    b, t, _ = u3.shape
    hb = 16
    return pl.pallas_call(
        functools.partial(_pool_kernel, tp=tp),
        grid=(b, t // tp),
        in_specs=[pl.BlockSpec((None, tp, POOL_WIDTH), lambda bi, i: (bi, i, 0)),
                  pl.BlockSpec((None, hb, POOL_WIDTH),
                               lambda bi, i: (bi, jnp.maximum(i * (tp // hb) - 1, 0), 0)),
                  pl.BlockSpec((POOL_WIDTH, POOL_WIDTH), lambda bi, i: (0, 0)),
                  pl.BlockSpec((1, POOL_WIDTH), lambda bi, i: (0, 0))],
        out_specs=pl.BlockSpec((None, tp, POOL_WIDTH), lambda bi, i: (bi, i, 0)),
        out_shape=jax.ShapeDtypeStruct((b, t, POOL_WIDTH), BF16),
        compiler_params=_cparams(("parallel", "parallel"), 32),
        name="pool_prompt",
    )(u3, u3, lw["w_pool"], lw["pool_scale"])


def _mix_out(x, mla, gla, pool, wo_ref):
    mix = jnp.concatenate([mla.astype(BF16), gla.astype(BF16), pool.astype(BF16)], axis=1)
    return x + _dot(mix, wo_ref[...])


def _xa_query(x1, gxa, wxq, gxq):
    q = _dot(_rms(x1, gxa).astype(BF16), wxq)
    return _head_rms(q, gxq, XA_HEADS, XA_HEAD, XA_HEAD) * (XA_HEAD ** -0.5)


def _xa_attend(qb, kb, vb):
    outs = []
    for h in range(XA_HEADS):
        sl = slice(h * XA_HEAD, (h + 1) * XA_HEAD)
        s = _dot_nt(qb[:, sl], kb[:, sl])
        e = jnp.exp(s - jnp.max(s, axis=-1, keepdims=True))
        p = e / jnp.sum(e, axis=-1, keepdims=True)
        outs.append(_dot(p.astype(BF16), vb[:, sl]))
    return jnp.concatenate(outs, axis=1).astype(BF16)


def _post_kernel(x_ref, mla_ref, gla_ref, pool_ref, wo_ref, gxa_ref, wxq_ref, gxq_ref, mk_ref, mv_ref,
                 wxo_ref, o_ref):
    x1 = _mix_out(x_ref[...], mla_ref[...], gla_ref[...], pool_ref[...], wo_ref)
    qb = _xa_query(x1, gxa_ref[...], wxq_ref[...], gxq_ref[...]).astype(BF16)
    o = _xa_attend(qb, mk_ref[...], mv_ref[...])
    o_ref[...] = x1 + _dot(o, wxo_ref[...])


def _post_prompt(x2, mla, gla, pool, mk4, mv4, layer, lw, t, tm):
    m = x2.shape[0]
    per_seq = t // tm
    row = lambda i: (i, 0)
    fix = lambda i: (0, 0)
    mem = lambda i: (layer, i // per_seq, 0, 0)
    sq = (D_MODEL, D_MODEL)
    return pl.pallas_call(
        _post_kernel,
        grid=(m // tm,),
        in_specs=[pl.BlockSpec((tm, D_MODEL), row), pl.BlockSpec((tm, MLA_OUT), row),
                  pl.BlockSpec((tm, GLA_OUT), row), pl.BlockSpec((tm, POOL_WIDTH), row),
                  pl.BlockSpec(sq, fix), pl.BlockSpec((1, D_MODEL), fix), pl.BlockSpec(sq, fix),
                  pl.BlockSpec((1, D_MODEL), fix), pl.BlockSpec((None, None, N_MEM, D_MODEL), mem),
                  pl.BlockSpec((None, None, N_MEM, D_MODEL), mem), pl.BlockSpec(sq, fix)],
        out_specs=pl.BlockSpec((tm, D_MODEL), row),
        out_shape=jax.ShapeDtypeStruct((m, D_MODEL), F32),
        compiler_params=_cparams(("parallel",), 48),
        name="post_prompt",
    )(x2, mla, gla, pool, lw["w_out"], lw["norm_xa"], lw["w_xq"], lw["g_xq"], mk4, mv4, lw["w_xo"])


def _mlp_kernel(x_ref, g_ref, w1_ref, w2_ref, o_ref):
    x = x_ref[...]
    h = _rms(x, g_ref[...]).astype(BF16)
    a = jnp.maximum(_dot(h, w1_ref[...]), 0.0)
    o_ref[...] = x + _dot((a * a).astype(BF16), w2_ref[...])


def _mlp(x2, lw, tm):
    m = x2.shape[0]
    once = pl.Buffered(1)
    layer = lw["layer"]
    return pl.pallas_call(
        _mlp_kernel,
        grid=(m // tm,),
        in_specs=[pl.BlockSpec((tm, D_MODEL), lambda i: (i, 0)),
                  pl.BlockSpec((1, D_MODEL), lambda i: (0, 0)),
                  pl.BlockSpec((None, D_MODEL, D_FF), lambda i: (layer, 0, 0), pipeline_mode=once),
                  pl.BlockSpec((None, D_FF, D_MODEL), lambda i: (layer, 0, 0), pipeline_mode=once)],
        out_specs=pl.BlockSpec((tm, D_MODEL), lambda i: (i, 0)),
        out_shape=jax.ShapeDtypeStruct((m, D_MODEL), F32),
        compiler_params=_cparams(("parallel",), 56),
        name="mlp",
    )(x2, lw["norm_mlp"], lw["w_ff1"], lw["w_ff2"])


DEC_PAGES = 64
DEC_CHUNK = 8192
DEC_LHS_ROWS = MLA_HEADS * QK_NOPE + 16
DEC_SS_ROWS = 48


def _mla_decode_kernel(pt_ref, q_ref, k_ref, latn_ref, cos_ref, sin_ref, wukt_ref, gkn_ref, gcol_ref,
                       ssel_ref, wuv_ref, lat_hbm, krt_hbm, o_ref,
                       lat_buf, kr_buf, sem, m_sc, l_sc, acc_sc, lhs_sc, lb_sc, *, pp, nj, layer):
    t = pl.program_id(0)
    n_steps = pl.num_programs(0)
    j = t % nj
    slot = t % 2

    def page_copies(step, buf_slot):
        b = step // nj
        base = (step % nj) * pp
        copies = []
        for p in range(pp):
            page = pt_ref[b, base + p]
            copies.append(pltpu.make_async_copy(lat_hbm.at[page, layer], lat_buf.at[buf_slot, p],
                                                sem.at[buf_slot, 0]))
            copies.append(pltpu.make_async_copy(krt_hbm.at[page, layer],
                                                kr_buf.at[buf_slot, :, pl.ds(p * PAGE_SIZE, PAGE_SIZE)],
                                                sem.at[buf_slot, 1]))
        return copies

    def start_all(copies):
        for i, cp in enumerate(copies):
            cp.start(priority=(i // 2 + i) % 2)

    start_all(page_copies(jnp.minimum(t + 1, n_steps - 1), 1 - slot))

    @pl.when(t == 0)
    def _():
        start_all(page_copies(t, slot))
        lhs_sc[0:MLA_HEADS * QK_NOPE, :] = wukt_ref[...]

    q16 = q_ref[...].astype(F32)

    @pl.when(j == 0)
    def _():
        m_sc[...] = jnp.full(m_sc.shape, -jnp.inf, F32)
        l_sc[...] = jnp.zeros(l_sc.shape, F32)
        acc_sc[...] = jnp.zeros(acc_sc.shape, F32)
        qn = (q16 * gkn_ref[...])[:, 0:QK_NOPE]
        tiled = jnp.concatenate([qn] * MLA_HEADS, axis=1)
        row = lax.broadcasted_iota(jnp.int32, tiled.shape, 0)
        lane = lax.broadcasted_iota(jnp.int32, tiled.shape, 1) // QK_NOPE
        qbd = jnp.where(row == lane, tiled, 0.0).astype(BF16)
        lhs_sc[MLA_HEADS * QK_NOPE:, :] = _dot(qbd, wukt_ref[...]).astype(BF16)

    for cp in page_copies(t, slot):
        cp.wait()

    qr = q16[:, QK_NOPE:QK_HEAD].astype(BF16)
    lhs = lhs_sc[...]
    gcol = gcol_ref[...]
    ssel = ssel_ref[...]
    half = QK_ROPE // 2
    nf = MLA_HEADS * QK_NOPE
    for p in range(pp):
        lb_sc[p * PAGE_SIZE:(p + 1) * PAGE_SIZE, :] = lat_buf[slot, p].astype(BF16)

    ck = DEC_CHUNK
    scores = []
    for c0 in range(0, pp * PAGE_SIZE, ck):
        res = _dot_nt(lhs, lb_sc[c0:c0 + ck, :])
        sq = res[0:nf, :] * res[0:nf, :]
        sn = res[nf:nf + 8, :]
        krt = kr_buf[slot, :, c0:c0 + ck]
        parts = [sq[h * QK_NOPE:(h + 1) * QK_NOPE, :].reshape(8, 8, ck).sum(axis=0) for h in range(MLA_HEADS)]
        parts.append((krt * krt).reshape(4, 8, ck).sum(axis=0))
        parts.append(jnp.zeros((8, ck), F32))
        p_hi, p_lo = _split_bf16(jnp.concatenate(parts, axis=0))
        ss = _dot(ssel, p_hi) + _dot(ssel, p_lo)
        krg = krt * gcol
        cs = cos_ref[:, c0:c0 + ck]
        sn_ = sin_ref[:, c0:c0 + ck]
        x1, x2 = krg[0:half, :], krg[half:, :]
        roped = jnp.concatenate([x1 * cs - x2 * sn_, x2 * cs + x1 * sn_], axis=0)
        sr = _dot(qr, roped.astype(BF16))[0:8, :]
        scores.append((sn + sr) * lax.rsqrt(ss * (1.0 / QK_HEAD) + EPS))
    s_all = jnp.concatenate(scores, axis=1)

    m_old = m_sc[...]
    m_new = jnp.maximum(m_old, jnp.max(s_all, axis=1, keepdims=True))
    alpha = jnp.exp(m_old - m_new)
    p = jnp.exp(s_all - m_new[:, 0:1])
    l_sc[...] = alpha * l_sc[...] + jnp.sum(p, axis=1, keepdims=True)
    acc_sc[...] = alpha[:, 0:1] * acc_sc[...] + _dot(p.astype(BF16), lb_sc[...])
    m_sc[...] = m_new

    @pl.when(t == n_steps - 1)
    def _():
        for cp in page_copies(t, 1 - slot):
            cp.wait()

    @pl.when(j == nj - 1)
    def _():
        s_self = jnp.sum((q16 * k_ref[...].astype(F32))[0:8, :], axis=1, keepdims=True)
        m_o = m_sc[...]
        m_n = jnp.maximum(m_o, s_self)
        al = jnp.exp(m_o - m_n)
        ps = jnp.exp(s_self - m_n)
        l = al * l_sc[...] + ps
        acc = al[:, 0:1] * acc_sc[...] + ps[:, 0:1] * latn_ref[...]
        o_lat = acc / l[:, 0:1]
        res = _dot(o_lat.astype(BF16), wuv_ref[...])
        row = lax.broadcasted_iota(jnp.int32, res.shape, 0)
        lane = lax.broadcasted_iota(jnp.int32, res.shape, 1) // V_HEAD
        o_ref[...] = jnp.sum(jnp.where(row == lane, res, 0.0), axis=0, keepdims=True).astype(o_ref.dtype)


def _mla_decode(page_table, q16, k16, lat_new3, cache_lat, cache_krt, layer, lw, tabs):
    pp = DEC_PAGES
    n = q16.shape[0]
    n_pages = page_table.shape[1]
    nj = n_pages // pp
    cos_t, sin_t = tabs
    fix2 = lambda t, pt: (0, 0)
    seq3 = lambda t, pt: (t // nj, 0, 0)
    tab = lambda t, pt: (0, t % nj)
    weights = [lw["w_uk_t"], lw["g_k_nope"], lw["g_k_rope_col"], lw["ss_select"], lw["w_uv"]]
    in_specs = [pl.BlockSpec((None, 16, HEAD_PAD), seq3), pl.BlockSpec((None, 16, HEAD_PAD), seq3),
                pl.BlockSpec((None, 1, KV_RANK), seq3),
                pl.BlockSpec((QK_ROPE // 2, pp * PAGE_SIZE), tab),
                pl.BlockSpec((QK_ROPE // 2, pp * PAGE_SIZE), tab)]
    in_specs += [pl.BlockSpec(w.shape, fix2) for w in weights]
    in_specs += [pl.BlockSpec(memory_space=pl.ANY), pl.BlockSpec(memory_space=pl.ANY)]
    grid_spec = pltpu.PrefetchScalarGridSpec(
        num_scalar_prefetch=1,
        grid=(n * nj,),
        in_specs=in_specs,
        out_specs=pl.BlockSpec((None, 1, MLA_OUT), seq3),
        scratch_shapes=[pltpu.VMEM((2, pp, PAGE_SIZE, KV_RANK), F32),
                        pltpu.VMEM((2, QK_ROPE, pp * PAGE_SIZE), F32),
                        pltpu.SemaphoreType.DMA((2, 2)),
                        pltpu.VMEM((8, 128), F32), pltpu.VMEM((8, 128), F32), pltpu.VMEM((8, KV_RANK), F32),
                        pltpu.VMEM((DEC_LHS_ROWS, KV_RANK), BF16),
                        pltpu.VMEM((pp * PAGE_SIZE, KV_RANK), BF16)])
    return pl.pallas_call(
        functools.partial(_mla_decode_kernel, pp=pp, nj=nj, layer=layer),
        grid_spec=grid_spec,
        out_shape=jax.ShapeDtypeStruct((n, 1, MLA_OUT), F32),
        compiler_params=_cparams(("arbitrary",), 52),
        name="mla_decode",
    )(page_table, q16, k16, lat_new3, cos_t, sin_t, *weights, cache_lat, cache_krt)


def _dec_mix_kernel(q_ref, k_ref, la_ref, v_ref, gr_ref, u_ref, s_ref, pb_ref, gout_ref, wp_ref, sc_ref,
                    s_o, gla_o, pool_o, pb_o, *, past):
    a = jnp.exp(la_ref[...])
    v4 = v_ref[...]
    vexp = jnp.concatenate([jnp.broadcast_to(v4[h:h + 1, :], (GLA_DK, GLA_DV)) for h in range(GLA_HEADS)],
                           axis=0)
    s_new = a * s_ref[...] + k_ref[...] * vexp
    s_o[...] = s_new
    prod = (q_ref[...] * (GLA_DK ** -0.5)) * s_new
    gr = gr_ref[...]
    gout = gout_ref[...]
    for h in range(GLA_HEADS):
        o = jnp.sum(prod[h * GLA_DK:(h + 1) * GLA_DK, :], axis=0, keepdims=True)
        on = _rms(o, gout)
        g = gr[h:h + 1, :]
        gla_o[h:h + 1, :] = (on * (g * _sigmoid(g))).astype(gla_o.dtype)

    st = pb_ref[...]
    u = u_ref[...]
    ridx = lax.broadcasted_iota(jnp.int32, st.shape, 0)
    parts = []
    for w in POOL_WINDOWS:
        tot = u + jnp.sum(jnp.where(ridx >= POOL_BUF + 1 - w, st, 0.0), axis=0, keepdims=True)
        parts.append(tot / float(min(w, past + 1)) - u)
    pooled = jnp.broadcast_to(_pool_select(parts), (8, POOL_WIDTH)).astype(BF16)
    y = _dot(pooled, wp_ref[...])[0:1, :] * sc_ref[...]
    pool_o[...] = y.astype(pool_o.dtype)
    pb_o[0:POOL_BUF - 1, :] = st[1:POOL_BUF, :]
    pb_o[POOL_BUF - 1:POOL_BUF, :] = u


def _dec_mix(gq, gk, la, gv, gr, u, state_gla4, state_pool, layer, lw, past):
    n = gq.shape[0]
    col = lambda x: x.reshape(n, GLA_QK, 1)
    b3 = lambda b: (b, 0, 0)
    fix = lambda b: (0, 0)
    return pl.pallas_call(
        functools.partial(_dec_mix_kernel, past=past),
        grid=(n,),
        in_specs=[pl.BlockSpec((None, GLA_QK, 1), b3)] * 3
        + [pl.BlockSpec((None, GLA_HEADS, GLA_DV), b3)] * 2
        + [pl.BlockSpec((None, 1, POOL_WIDTH), b3),
           pl.BlockSpec((None, None, GLA_QK, GLA_DV), lambda b: (b, layer, 0, 0)),
           pl.BlockSpec((None, None, POOL_BUF, POOL_WIDTH), lambda b: (b, layer, 0, 0)),
           pl.BlockSpec((1, GLA_DV), fix), pl.BlockSpec((POOL_WIDTH, POOL_WIDTH), fix),
           pl.BlockSpec((1, POOL_WIDTH), fix)],
        out_specs=[pl.BlockSpec((None, GLA_QK, GLA_DV), b3), pl.BlockSpec((None, GLA_HEADS, GLA_DV), b3),
                   pl.BlockSpec((None, 1, POOL_WIDTH), b3), pl.BlockSpec((None, POOL_BUF, POOL_WIDTH), b3)],
        out_shape=[jax.ShapeDtypeStruct((n, GLA_QK, GLA_DV), F32),
                   jax.ShapeDtypeStruct((n, GLA_HEADS, GLA_DV), F32),
                   jax.ShapeDtypeStruct((n, 1, POOL_WIDTH), F32),
                   jax.ShapeDtypeStruct((n, POOL_BUF, POOL_WIDTH), F32)],
        compiler_params=_cparams(("parallel",), 32),
        name="dec_mix",
    )(col(gq), col(gk), col(la), gv.reshape(n, GLA_HEADS, GLA_DV), gr.reshape(n, GLA_HEADS, GLA_DV),
      u.reshape(n, 1, POOL_WIDTH), state_gla4, state_pool, lw["g_gla_out64"], lw["w_pool"], lw["pool_scale"])


def _post_a_kernel(x_ref, mla_ref, gla_ref, pool_ref, wo_ref, gxa_ref, wxq_ref, gxq_ref, x1_ref, q_ref):
    x1 = _mix_out(x_ref[...], mla_ref[...], gla_ref[...], pool_ref[...], wo_ref)
    x1_ref[...] = x1
    q_ref[...] = _xa_query(x1, gxa_ref[...], wxq_ref[...], gxq_ref[...])


def _post_a(x2, mla, gla, pool, lw):
    m = x2.shape[0]
    return pl.pallas_call(
        _post_a_kernel,
        out_shape=[jax.ShapeDtypeStruct((m, D_MODEL), F32), jax.ShapeDtypeStruct((m, D_MODEL), F32)],
        compiler_params=pltpu.CompilerParams(vmem_limit_bytes=32 * MIB),
        name="post_a",
    )(x2, mla, gla, pool, lw["w_out"], lw["norm_xa"], lw["w_xq"], lw["g_xq"])


XA_HALVES = XA_HEAD // 128
XA_ROWS = XA_HALVES * XA_HEADS


def _xa_decode_kernel(q_ref, mk_ref, mv_ref, o_ref):
    k3 = mk_ref[...].reshape(N_MEM, XA_ROWS, 128)
    v3 = mv_ref[...].reshape(N_MEM, XA_ROWS, 128)
    s = jnp.sum(k3 * q_ref[...][None], axis=-1, keepdims=True)
    s = s[:, 0:XA_HEADS] + s[:, XA_HEADS:XA_ROWS]
    s = jnp.concatenate([s, s], axis=1)
    e = jnp.exp(s - jnp.max(s, axis=0, keepdims=True))
    p = e / jnp.sum(e, axis=0, keepdims=True)
    o_ref[...] = jnp.sum(p * v3, axis=0)


def _xa_decode(q3, cache_k, cache_v, layer):
    n = q3.shape[0]
    depth = cache_k.shape[1]
    view = lambda c: c.reshape(n, depth, N_MEM, XA_HEADS, XA_HALVES, 128).transpose(0, 1, 2, 4, 3, 5).reshape(
        n, depth, N_MEM * XA_ROWS, 128)
    q8 = q3.reshape(n, XA_HEADS, XA_HALVES, 128).transpose(0, 2, 1, 3).reshape(n, XA_ROWS, 128)
    b3 = lambda b: (b, 0, 0)
    mem = lambda b: (b, layer, 0, 0)
    o = pl.pallas_call(
        _xa_decode_kernel,
        grid=(n,),
        in_specs=[pl.BlockSpec((None, XA_ROWS, 128), b3),
                  pl.BlockSpec((None, None, N_MEM * XA_ROWS, 128), mem),
                  pl.BlockSpec((None, None, N_MEM * XA_ROWS, 128), mem)],
        out_specs=pl.BlockSpec((None, XA_ROWS, 128), b3),
        out_shape=jax.ShapeDtypeStruct((n, XA_ROWS, 128), F32),
        compiler_params=_cparams(("parallel",), 32),
        name="xa_decode",
    )(q8, view(cache_k), view(cache_v))
    return o.reshape(n, XA_HALVES, XA_HEADS, 128).transpose(0, 2, 1, 3).reshape(n, 1, D_MODEL)


def _post_b_kernel(x1_ref, o_ref, wxo_ref, x2_ref):
    x2_ref[...] = x1_ref[...] + _dot(o_ref[...].astype(BF16), wxo_ref[...])


def _post_b(x1, o, lw):
    return pl.pallas_call(
        _post_b_kernel,
        out_shape=jax.ShapeDtypeStruct(x1.shape, F32),
        compiler_params=pltpu.CompilerParams(vmem_limit_bytes=32 * MIB),
        name="post_b",
    )(x1, o, lw["w_xo"])


def _rope_tables_rows(pos):
    half = QK_ROPE // 2
    inv = ROPE_THETA ** (-jnp.arange(half, dtype=F32) / half)
    ang = pos.astype(F32)[:, None] * inv[None, :]
    cos, sin = jnp.cos(ang), jnp.sin(ang)
    n = pos.shape[0]
    c = jnp.concatenate([jnp.ones((n, QK_NOPE), F32), cos, cos, jnp.ones((n, HEAD_PAD - QK_HEAD), F32)], axis=1)
    s = jnp.concatenate([jnp.zeros((n, QK_NOPE), F32), sin, sin, jnp.zeros((n, HEAD_PAD - QK_HEAD), F32)],
                        axis=1)
    return c, s


def _rope_tables_cols(pos):
    half = QK_ROPE // 2
    inv = ROPE_THETA ** (-jnp.arange(half, dtype=F32) / half)
    ang = pos.astype(F32)[:, None] * inv[None, :]
    return jnp.cos(ang).T, jnp.sin(ang).T


def _pad_heads(w, width):
    pad = [(0, 0)] * (w.ndim - 1) + [(0, HEAD_PAD - width)]
    w = jnp.pad(w, pad)
    return w.reshape(w.shape[:-2] + (w.shape[-2] * HEAD_PAD,))


def _layer_weights(l, p):
    w_in = p["w_in"][l]
    sizes = (Q_RANK, KV_RANK, QK_ROPE, GLA_QK, GLA_QK, GLA_OUT, GLA_GATE_RANK, GLA_OUT, POOL_WIDTH)
    offs = [0]
    for s in sizes:
        offs.append(offs[-1] + s)
    piece = lambda i: w_in[:, offs[i]:offs[i + 1]]
    w_in_p = jnp.concatenate([piece(0), piece(1), piece(3), piece(4), piece(5), piece(7), piece(8), piece(2),
                              piece(6), jnp.zeros((D_MODEL, IN_PAD - sum(sizes)), F32)], axis=1)
    g_q = _pad_heads(jnp.broadcast_to(p["g_qk_q"][l], (MLA_HEADS, QK_HEAD)), QK_HEAD)
    g_k = _pad_heads(jnp.broadcast_to(p["g_qk_k"][l], (MLA_HEADS, QK_HEAD)), QK_HEAD)
    place = jnp.zeros((128, QK_W), F32)
    j = jnp.arange(QK_ROPE)
    for h in range(MLA_HEADS):
        place = place.at[j, h * HEAD_PAD + QK_NOPE + j].set(1.0)
    w_gate = jnp.zeros((128, GLA_QK), F32).at[QK_ROPE:QK_ROPE + GLA_GATE_RANK].set(p["w_gate_up"][l])
    half = QK_ROPE // 2

    def partner(w):
        w4 = w.reshape(w.shape[:-1] + (MLA_HEADS, HEAD_PAD))
        x1 = w4[..., QK_NOPE:QK_NOPE + half]
        x2 = w4[..., QK_NOPE + half:QK_HEAD]
        out = jnp.zeros_like(w4).at[..., QK_NOPE:QK_NOPE + half].set(x2).at[..., QK_NOPE + half:QK_HEAD].set(x1)
        return out.reshape(w.shape)

    sign = jnp.tile(jnp.concatenate([jnp.zeros((QK_NOPE,), F32), -jnp.ones((half,), F32), jnp.ones((half,), F32),
                                     jnp.zeros((HEAD_PAD - QK_HEAD,), F32)]), MLA_HEADS)
    w_uq_pad = _pad_heads(p["w_uq"][l], QK_HEAD)
    w_uv2 = p["w_uv"][l].reshape(KV_RANK, MLA_OUT)
    place_both = jnp.concatenate([place, partner(place) * sign], axis=1)
    wp = p["w_pool"][l]
    w_pool = jnp.zeros((POOL_WIDTH, POOL_WIDTH), F32)
    for g in range(len(POOL_WINDOWS)):
        w_pool = w_pool.at[g * POOL_GROUP:(g + 1) * POOL_GROUP, g * POOL_GROUP:(g + 1) * POOL_GROUP].set(wp[g])
    blk = jnp.arange(GLA_OUT) // GLA_DV
    row = lambda v: v.reshape(1, -1).astype(F32)
    bf = lambda v: v.astype(BF16)
    w_uk2 = p["w_uk"][l].reshape(KV_RANK, MLA_HEADS * QK_NOPE)
    g_k_nope = jnp.zeros((1, HEAD_PAD), F32).at[0, :QK_NOPE].set(p["g_qk_k"][l][:QK_NOPE])
    col = jnp.arange(DEC_SS_ROWS)[None, :] // 8
    hrow = jnp.arange(8)[:, None]
    ss_select = ((hrow < MLA_HEADS) & ((col == hrow) | (col == MLA_HEADS))).astype(F32)
    return {
        "norm_mix": row(p["norm_mix"][l]), "w_in": bf(w_in_p), "g_q_lat": row(p["g_q_lat"][l]),
        "w_uq2": bf(jnp.concatenate([w_uq_pad, partner(w_uq_pad) * sign], axis=1)),
        "g_kv_lat": row(p["g_kv_lat"][l]),
        "w_ukv": bf(jnp.concatenate([_pad_heads(p["w_uk"][l], QK_NOPE), w_uv2], axis=1)), "w_uv": bf(w_uv2),
        "g_qk_q": row(g_q), "g_qk_k": row(g_k), "place": bf(jnp.tile(place_both, (2, 1))), "w_gate": bf(w_gate),
        "g_q_rot": row(partner(g_q)), "g_k_rot": row(partner(g_k)),
        "b_gate": row(p["b_gate"][l]),
        "e64": bf((blk[:, None] == blk[None, :]).astype(F32)),
        "g_gla_out": row(jnp.tile(p["g_gla_out"][l], GLA_HEADS)), "g_gla_out64": row(p["g_gla_out"][l]),
        "w_pool": bf(w_pool), "pool_scale": row(p["pool_scale"][l]),
        "w_out": bf(p["w_out"][l]), "norm_xa": row(p["norm_xa"][l]), "w_xq": bf(p["w_xq"][l]),
        "g_xq": row(jnp.tile(p["g_xq"][l], XA_HEADS)), "w_xo": bf(p["w_xo"][l]),
        "norm_mlp": row(p["norm_mlp"][l]), "w_ff1": p["w_ff1_bf16"], "w_ff2": p["w_ff2_bf16"], "layer": l,
        "w_uk_t": bf(w_uk2.T), "g_k_nope": g_k_nope,
        "g_k_rope_col": jnp.broadcast_to(p["g_qk_k"][l][QK_NOPE:, None], (QK_ROPE, DEC_CHUNK)).astype(F32),
        "ss_select": bf(ss_select),
    }


def _prompt_layer(x2, layer, mk4, mv4, lw, ropes, b, t):
    lat, kr, q, k, v, gq, gk, gv, la, gr, u = _in_proj(x2, lw, ropes, ROW_TILE)
    r3 = lambda a: a.reshape(b, t, a.shape[-1])
    mla = _mla_prompt(r3(q), r3(k), r3(v), ROW_TILE).reshape(b * t, MLA_OUT)
    gla, st = _gla_prompt(r3(gq), r3(gk), r3(gv), r3(la), r3(gr), lw)
    pool = _pool_prompt(r3(u), lw, t).reshape(b * t, POOL_WIDTH)
    x2 = _post_prompt(x2, mla, gla.reshape(b * t, GLA_OUT), pool, mk4, mv4, layer, lw, t, ROW_TILE)
    x2 = _mlp(x2, lw, ROW_TILE)
    gla_state = jnp.stack([st[:, h * GLA_DV:(h + 1) * GLA_DV, h * GLA_DK:(h + 1) * GLA_DK]
                           for h in range(GLA_HEADS)], axis=1).transpose(0, 1, 3, 2)
    outs = (lat.reshape(b, t, KV_RANK), kr.reshape(b, t, QK_ROPE), gla_state,
            r3(u)[:, t - POOL_BUF:])
    return x2, outs


def _sample_layer(x2, layer, lw, ropes, tabs, page_table, cache_lat, cache_krt, state_gla4, state_pool,
                  cache_mk, cache_mv, past):
    n = x2.shape[0]
    lat, kr, q, k, v, gq, gk, gv, la, gr, u = _in_proj(x2, lw, ropes, n)
    pad16 = lambda a: jnp.pad(a.reshape(n, MLA_HEADS, HEAD_PAD), ((0, 0), (0, 16 - MLA_HEADS), (0, 0)))
    mla = _mla_decode(page_table, pad16(q), pad16(k), lat.reshape(n, 1, KV_RANK), cache_lat, cache_krt, layer,
                      lw, tabs).reshape(n, MLA_OUT)
    s_new, gla, pool, pb = _dec_mix(gq, gk, la, gv, gr, u, state_gla4, state_pool, layer, lw, past)
    x1, qx = _post_a(x2, mla, gla.reshape(n, GLA_OUT), pool.reshape(n, POOL_WIDTH), lw)
    ox = _xa_decode(qx.reshape(n, 1, D_MODEL), cache_mk, cache_mv, layer).reshape(n, D_MODEL)
    x2 = _mlp(_post_b(x1, ox, lw), lw, n)
    outs = (lat.reshape(n, 1, KV_RANK), kr.reshape(n, 1, QK_ROPE),
            s_new.reshape(n, GLA_HEADS, GLA_DK, GLA_DV), pb)
    return x2, outs


def kernel(x_prompt, x_sample, mem_prompt, cache_mla_latent, cache_mla_krope, state_gla, state_pool,
           cache_mem_k, cache_mem_v, page_table, norm_mix, w_in, g_q_lat, w_uq, g_kv_lat, w_uk, w_uv,
           g_qk_q, g_qk_k, w_gate_up, b_gate, g_gla_out, w_pool, pool_scale, w_out, norm_xa, norm_mem,
           w_xq, w_xk, w_xv, g_xq, g_xk, w_xo, norm_mlp, w_ff1, w_ff2):
    params = dict(norm_mix=norm_mix, w_in=w_in, g_q_lat=g_q_lat, w_uq=w_uq, g_kv_lat=g_kv_lat, w_uk=w_uk,
                  w_uv=w_uv, g_qk_q=g_qk_q, g_qk_k=g_qk_k, w_gate_up=w_gate_up, b_gate=b_gate,
                  g_gla_out=g_gla_out, w_pool=w_pool, pool_scale=pool_scale, w_out=w_out, norm_xa=norm_xa,
                  norm_mem=norm_mem, w_xq=w_xq, w_xk=w_xk, w_xv=w_xv, g_xq=g_xq, g_xk=g_xk, w_xo=w_xo,
                  norm_mlp=norm_mlp, w_ff1_bf16=w_ff1.astype(BF16), w_ff2_bf16=w_ff2.astype(BF16))
    b, t, _ = x_prompt.shape
    n, dec_seq, _ = x_sample.shape
    depth = norm_mix.shape[0]
    n_pages = page_table.shape[1]
    past = n_pages * PAGE_SIZE

    ropes_p = _rope_tables_rows(jnp.arange(t, dtype=jnp.int32))
    ropes_s = _rope_tables_rows(jnp.full((n,), past, jnp.int32))
    tabs = _rope_tables_cols(jnp.arange(past, dtype=jnp.int32))

    yp = x_prompt.reshape(b * t, D_MODEL)
    ys = x_sample.reshape(n * dec_seq, D_MODEL)
    mem2 = mem_prompt.reshape(b * N_MEM, D_MODEL)
    state_gla4 = state_gla.reshape(n, depth, GLA_QK, GLA_DV)
    cache_krt = cache_mla_krope.transpose(0, 1, 3, 2)

    mk, mv, mkb, mvb = _memory_kv(mem2, norm_mem.reshape(depth, 1, D_MODEL), w_xk.astype(BF16),
                                  w_xv.astype(BF16), jnp.tile(g_xk, (1, XA_HEADS)).reshape(depth, 1, D_MODEL))
    mk4 = mkb.reshape(depth, b, N_MEM, D_MODEL)
    mv4 = mvb.reshape(depth, b, N_MEM, D_MODEL)

    p_outs, s_outs = [], []
    for l in range(depth):
        lw = _layer_weights(l, params)
        yp, po = _prompt_layer(yp, l, mk4, mv4, lw, ropes_p, b, t)
        ys, so = _sample_layer(ys, l, lw, ropes_s, tabs, page_table, cache_mla_latent, cache_krt,
                               state_gla4, state_pool, cache_mem_k, cache_mem_v, past)
        p_outs.append(po)
        s_outs.append(so)

    stack = lambda outs, i: jnp.stack([o[i] for o in outs], axis=1)
    return (yp.reshape(b, t, D_MODEL), ys.reshape(n, dec_seq, D_MODEL),
            stack(p_outs, 0), stack(p_outs, 1), stack(p_outs, 2), stack(p_outs, 3),
            mk.reshape(b, depth, N_MEM, XA_HEADS, XA_HEAD), mv.reshape(b, depth, N_MEM, XA_HEADS, XA_HEAD),
            stack(s_outs, 0), stack(s_outs, 1), stack(s_outs, 2), stack(s_outs, 3))
```

```python
import functools

import jax
import jax.numpy as jnp
from jax import lax
from jax.experimental import pallas as pl
from jax.experimental.pallas import tpu as pltpu

F32 = jnp.float32
BF16 = jnp.bfloat16

D_MODEL = 1024
MLA_HEADS = 4
QK_NOPE = 64
QK_ROPE = 32
QK_HEAD = QK_NOPE + QK_ROPE
V_HEAD = 128
Q_RANK = 384
KV_RANK = 256
ROPE_THETA = 10000.0
GLA_HEADS = 4
GLA_DK = 32
GLA_DV = 64
GLA_GATE_RANK = 16
GLA_TAU = 16.0
GLA_CHUNK = 64
GLA_GROUP = 256
POOL_WINDOWS = (2, 4, 8, 16)
POOL_GROUP = 64
POOL_WIDTH = POOL_GROUP * len(POOL_WINDOWS)
POOL_BUF = max(POOL_WINDOWS) - 1
MLA_OUT = MLA_HEADS * V_HEAD
GLA_OUT = GLA_HEADS * GLA_DV
N_MEM = 256
XA_HEADS = 4
XA_HEAD = D_MODEL // XA_HEADS
D_FF = 4 * D_MODEL
PAGE_SIZE = 128
EPS = 1e-6

HEAD_PAD = 128
QK_W = MLA_HEADS * HEAD_PAD
GLA_QK = GLA_HEADS * GLA_DK
IN_PAD = 1792
MISC0 = 1664

ROW_TILE = 512
MIB = 1024 * 1024


def _cparams(sem, vmem_mib):
    return pltpu.CompilerParams(dimension_semantics=sem, vmem_limit_bytes=vmem_mib * MIB)


def _rms(x, g):
    y = x * lax.rsqrt(jnp.mean(x * x, axis=-1, keepdims=True) + EPS)
    return y * g


def _head_rms(x, g, nh, width, count):
    outs = []
    for h in range(nh):
        p = x[:, h * width:(h + 1) * width]
        ms = jnp.sum(p * p, axis=-1, keepdims=True) * (1.0 / count)
        outs.append(p * lax.rsqrt(ms + EPS) * g[:, h * width:(h + 1) * width])
    return jnp.concatenate(outs, axis=1)


def _dot(a, b):
    return jnp.dot(a, b, preferred_element_type=F32)


def _dot_nt(a, b):
    return lax.dot_general(a, b, (((1,), (1,)), ((), ())), preferred_element_type=F32)


def _split_bf16(x):
    hi = x.astype(BF16)
    lo = (x - hi.astype(F32)).astype(BF16)
    return hi, lo


def _dot_split(x, w):
    hi, lo = _split_bf16(x)
    return _dot(hi, w) + _dot(lo, w)


def _norm_rope(x, xrot, g, grot, c, s):
    outs = []
    for h in range(MLA_HEADS):
        sl = slice(h * HEAD_PAD, (h + 1) * HEAD_PAD)
        p = x[:, sl]
        r = lax.rsqrt(jnp.sum(p * p, axis=-1, keepdims=True) * (1.0 / QK_HEAD) + EPS)
        outs.append((p * r * g[:, sl]) * c + (xrot[:, sl] * r * grot[:, sl]) * s)
    return jnp.concatenate(outs, axis=1)


def _log_sigmoid(x):
    return -(jnp.maximum(-x, 0.0) + jnp.log1p(jnp.exp(-jnp.abs(x))))


def _sigmoid(x):
    return 1.0 / (1.0 + jnp.exp(-x))


def _memkv_kernel(mem_ref, g_ref, wk_ref, wv_ref, gk_ref, k_ref, v_ref, kb_ref, vb_ref):
    m = _rms(mem_ref[...], g_ref[...]).astype(BF16)
    k = _head_rms(_dot(m, wk_ref[...]), gk_ref[...], XA_HEADS, XA_HEAD, XA_HEAD)
    v = _dot(m, wv_ref[...])
    k_ref[...] = k.reshape(k_ref.shape)
    v_ref[...] = v.reshape(v_ref.shape)
    kb_ref[...] = k.astype(BF16)
    vb_ref[...] = v.astype(BF16)


def _memory_kv(mem2, g, wk, wv, gk):
    m = mem2.shape[0]
    depth = wk.shape[0]
    tm = ROW_TILE
    nb = tm // N_MEM
    vec = lambda l, i: (l, 0, 0)
    return pl.pallas_call(
        _memkv_kernel,
        grid=(depth, m // tm),
        in_specs=[pl.BlockSpec((tm, D_MODEL), lambda l, i: (i, 0)), pl.BlockSpec((None, 1, D_MODEL), vec),
                  pl.BlockSpec((None, D_MODEL, D_MODEL), vec), pl.BlockSpec((None, D_MODEL, D_MODEL), vec),
                  pl.BlockSpec((None, 1, D_MODEL), vec)],
        out_specs=[pl.BlockSpec((nb, None, N_MEM, D_MODEL), lambda l, i: (i, l, 0, 0))] * 2
        + [pl.BlockSpec((None, tm, D_MODEL), lambda l, i: (l, i, 0))] * 2,
        out_shape=[jax.ShapeDtypeStruct((m // N_MEM, depth, N_MEM, D_MODEL), F32)] * 2
        + [jax.ShapeDtypeStruct((depth, m, D_MODEL), BF16)] * 2,
        compiler_params=_cparams(("parallel", "parallel"), 40),
        name="memory_kv",
    )(mem2, g, wk, wv, gk)


def _inproj_kernel(x_ref, gmix_ref, win_ref, gql_ref, wuq_ref, gkv_ref, wukv_ref,
                   gq_ref, gk_ref, place_ref, wg_ref, bg_ref, gqr_ref, gkr_ref,
                   c_ref, s_ref, lat_ref, kr_ref, q_ref, k_ref, v_ref, gq_o, gk_o, gv_o, la_o, gr_o, u_o):
    h = _rms(x_ref[...], gmix_ref[...]).astype(BF16)
    z = _dot(h, win_ref[...])
    c_q = z[:, 0:384]
    c_kv = z[:, 384:640]
    misc = z[:, MISC0:MISC0 + 128]
    gq_o[...] = z[:, 640:768]
    gk_o[...] = z[:, 768:896]
    gv_o[...] = z[:, 896:1152]
    gr_o[...] = z[:, 1152:1408]
    u_o[...] = z[:, 1408:1664]
    kr_ref[...] = misc[:, 0:QK_ROPE]

    lat = _rms(c_kv, gkv_ref[...])
    lat_ref[...] = lat
    latb = lat.astype(BF16)
    c, s = c_ref[...], s_ref[...]

    q2 = _dot(_rms(c_q, gql_ref[...]).astype(BF16), wuq_ref[...])
    q = _norm_rope(q2[:, :QK_W], q2[:, QK_W:], gq_ref[...], gqr_ref[...], c, s)
    q_ref[...] = (q * (QK_HEAD ** -0.5)).astype(BF16)

    placed = _dot(jnp.concatenate(_split_bf16(misc), axis=1), place_ref[...])
    kv = _dot(latb, wukv_ref[...])
    k = kv[:, :QK_W] + placed[:, :QK_W]
    k_ref[...] = _norm_rope(k, placed[:, QK_W:], gk_ref[...], gkr_ref[...], c, s).astype(BF16)
    v_ref[...] = kv[:, QK_W:].astype(BF16)

    gate = _dot(misc.astype(BF16), wg_ref[...]) + bg_ref[...]
    la_o[...] = _log_sigmoid(gate) * (1.0 / GLA_TAU)


def _in_proj(x2, lw, ropes, tm):
    m = x2.shape[0]
    row = lambda i: (i, 0)
    fix = lambda i: (0, 0)
    c, s = ropes
    if c.shape[0] == m:
        rope_map = row
    else:
        nb = c.shape[0] // tm
        rope_map = lambda i: (i % nb, 0)
    weights = [lw["norm_mix"], lw["w_in"], lw["g_q_lat"], lw["w_uq2"], lw["g_kv_lat"], lw["w_ukv"],
               lw["g_qk_q"], lw["g_qk_k"], lw["place"], lw["w_gate"], lw["b_gate"],
               lw["g_q_rot"], lw["g_k_rot"]]
    in_specs = [pl.BlockSpec((tm, D_MODEL), row)]
    in_specs += [pl.BlockSpec(w.shape, fix) for w in weights]
    in_specs += [pl.BlockSpec((tm, HEAD_PAD), rope_map)] * 2
    widths = [(KV_RANK, F32), (QK_ROPE, F32), (QK_W, BF16), (QK_W, BF16), (MLA_OUT, BF16),
              (GLA_QK, F32), (GLA_QK, F32), (GLA_OUT, F32), (GLA_QK, F32), (GLA_OUT, F32),
              (POOL_WIDTH, F32)]
    return pl.pallas_call(
        _inproj_kernel,
        grid=(m // tm,),
        in_specs=in_specs,
        out_specs=[pl.BlockSpec((tm, w), row) for w, _ in widths],
        out_shape=[jax.ShapeDtypeStruct((m, w), dt) for w, dt in widths],
        compiler_params=_cparams(("parallel",), 48),
        name="in_proj",
    )(x2, *weights, c, s)


FLASH_HEADS = 4


def _flash_kernel(q_ref, k_ref, v_ref, o_ref, *, tq):
    qi = pl.program_id(2)
    nh = q_ref.shape[1] // HEAD_PAD
    qs = [q_ref[:, h * HEAD_PAD:(h + 1) * HEAD_PAD] for h in range(nh)]

    def block(j, carry, masked):
        start = pl.multiple_of(j * tq, tq)
        out = []
        for h in range(nh):
            m, l, acc = carry[h]
            k = k_ref[pl.ds(start, tq), h * HEAD_PAD:(h + 1) * HEAD_PAD]
            v = v_ref[pl.ds(start, tq), h * V_HEAD:(h + 1) * V_HEAD]
            s = _dot_nt(qs[h], k)
            if masked:
                rows = lax.broadcasted_iota(jnp.int32, (tq, tq), 0)
                cols = lax.broadcasted_iota(jnp.int32, (tq, tq), 1)
                s = jnp.where(cols <= rows, s, -jnp.inf)
            m_new = jnp.maximum(m, jnp.max(s, axis=-1, keepdims=True))
            alpha = jnp.exp(m - m_new)
            p = jnp.exp(s - m_new)
            l = alpha * l + jnp.sum(p, axis=-1, keepdims=True)
            acc = alpha * acc + _dot(p.astype(BF16), v)
            out.append((m_new, l, acc))
        return tuple(out)

    init = tuple((jnp.full((tq, 1), -jnp.inf, F32), jnp.zeros((tq, 1), F32), jnp.zeros((tq, V_HEAD), F32))
                 for _ in range(nh))
    carry = lax.fori_loop(0, qi, lambda j, cr: block(j, cr, False), init)
    res = block(qi, carry, True)
    o_ref[...] = jnp.concatenate([acc / l for _, l, acc in res], axis=1).astype(o_ref.dtype)


def _mla_prompt(q3, k3, v3, tq):
    b, t, _ = q3.shape
    hp = FLASH_HEADS
    qmap = lambda bi, h, qi: (bi, qi, h)
    kmap = lambda bi, h, qi: (bi, 0, h)
    return pl.pallas_call(
        functools.partial(_flash_kernel, tq=tq),
        grid=(b, MLA_HEADS // hp, t // tq),
        in_specs=[pl.BlockSpec((None, tq, hp * HEAD_PAD), qmap), pl.BlockSpec((None, t, hp * HEAD_PAD), kmap),
                  pl.BlockSpec((None, t, hp * V_HEAD), kmap)],
        out_specs=pl.BlockSpec((None, tq, hp * V_HEAD), qmap),
        out_shape=jax.ShapeDtypeStruct((b, t, MLA_OUT), BF16),
        compiler_params=_cparams(("parallel", "parallel", "arbitrary"), 48),
        name="mla_prompt",
    )(q3, k3, v3)


def _gla_kernel(q_ref, k_ref, v_ref, la_ref, gr_ref, tri_ref, ones_ref, e64_ref, gout_ref, o_ref, st_ref, *, t):
    c = GLA_CHUNK
    gt = GLA_GROUP
    tri = tri_ref[...]
    ones_bd = ones_ref[...]
    e64 = e64_ref[...]
    gout = gout_ref[...]
    lane_qk = lax.broadcasted_iota(jnp.int32, (gt, GLA_QK), 1) // GLA_DK
    lane_v = lax.broadcasted_iota(jnp.int32, (gt, GLA_OUT), 1) // GLA_DV
    row_chunk = lax.broadcasted_iota(jnp.int32, (gt, GLA_QK), 0) // c
    rr = lax.broadcasted_iota(jnp.int32, (gt, gt), 0)
    cc = lax.broadcasted_iota(jnp.int32, (gt, gt), 1)
    causal = (cc <= rr) & (rr // c == cc // c)
    bd = (lax.broadcasted_iota(jnp.int32, (GLA_OUT, GLA_QK), 0) // GLA_DV
          == lax.broadcasted_iota(jnp.int32, (GLA_OUT, GLA_QK), 1) // GLA_DK)

    def group(gi, st):
        r0 = pl.multiple_of(gi * gt, gt)
        q = q_ref[pl.ds(r0, gt), :] * (GLA_DK ** -0.5)
        k = k_ref[pl.ds(r0, gt), :]
        v = v_ref[pl.ds(r0, gt), :]
        a_pair = jnp.concatenate(_split_bf16(la_ref[pl.ds(r0, gt), :]), axis=1)
        b2 = _dot(tri, a_pair)
        t2 = _dot(ones_bd, a_pair)
        b = b2[:, :GLA_QK] + b2[:, GLA_QK:]
        tot = t2[:, :GLA_QK] + t2[:, GLA_QK:]
        qd = q * jnp.exp(b)
        qdb = qd.astype(BF16)
        kd = (k * jnp.exp(-b)).astype(BF16)
        k2 = k * jnp.exp(tot - b)
        decay = jnp.exp(tot)
        vb = v.astype(BF16)
        vt = v.T.astype(BF16)
        o = jnp.zeros((gt, GLA_OUT), F32)
        for h in range(GLA_HEADS):
            qh = jnp.where(lane_qk == h, qd, 0.0).astype(BF16)
            att = jnp.where(causal, _dot_nt(qh, kd), 0.0)
            o = o + jnp.where(lane_v == h, _dot(att.astype(BF16), vb), 0.0)
        k2_blocks = jnp.concatenate([jnp.where(row_chunk == ci, k2, 0.0) for ci in range(gt // c)], axis=1)
        incr = _dot(vt, k2_blocks.astype(BF16))
        inter = []
        for ci in range(gt // c):
            inter.append(_dot_nt(qdb[ci * c:(ci + 1) * c, :], st.astype(BF16)))
            st = st * decay[ci * c:ci * c + 1, :] + jnp.where(bd, incr[:, ci * GLA_QK:(ci + 1) * GLA_QK], 0.0)
        o = o + jnp.concatenate(inter, axis=0)
        ms = _dot_split(o * o, e64) * (1.0 / GLA_DV)
        on = o * lax.rsqrt(ms + EPS) * gout
        gr = gr_ref[pl.ds(r0, gt), :]
        o_ref[pl.ds(r0, gt), :] = (on * (gr * _sigmoid(gr))).astype(o_ref.dtype)
        return st

    st_ref[...] = lax.fori_loop(0, t // gt, group, jnp.zeros((GLA_OUT, GLA_QK), F32), unroll=4)


def _gla_prompt(gq3, gk3, gv3, la3, gr3, lw):
    b, t, _ = gq3.shape
    bmap = lambda i: (i, 0, 0)
    fix = lambda i: (0, 0)
    chunk_id = jnp.arange(GLA_GROUP) // GLA_CHUNK
    same = chunk_id[:, None] == chunk_id[None, :]
    tri = (same & (jnp.arange(GLA_GROUP)[None, :] <= jnp.arange(GLA_GROUP)[:, None])).astype(BF16)
    return pl.pallas_call(
        functools.partial(_gla_kernel, t=t),
        grid=(b,),
        in_specs=[pl.BlockSpec((None, t, GLA_QK), bmap), pl.BlockSpec((None, t, GLA_QK), bmap),
                  pl.BlockSpec((None, t, GLA_OUT), bmap), pl.BlockSpec((None, t, GLA_QK), bmap),
                  pl.BlockSpec((None, t, GLA_OUT), bmap), pl.BlockSpec((GLA_GROUP, GLA_GROUP), fix),
                  pl.BlockSpec((GLA_GROUP, GLA_GROUP), fix),
                  pl.BlockSpec((GLA_OUT, GLA_OUT), fix), pl.BlockSpec((1, GLA_OUT), fix)],
        out_specs=[pl.BlockSpec((None, t, GLA_OUT), bmap), pl.BlockSpec((None, GLA_OUT, GLA_QK), bmap)],
        out_shape=[jax.ShapeDtypeStruct((b, t, GLA_OUT), BF16),
                   jax.ShapeDtypeStruct((b, GLA_OUT, GLA_QK), F32)],
        compiler_params=_cparams(("parallel",), 40),
        name="gla_prompt",
    )(gq3, gk3, gv3, la3, gr3, tri, same.astype(BF16), lw["e64"], lw["g_gla_out"])


def _pool_select(parts):
    lane = lax.broadcasted_iota(jnp.int32, parts[0].shape, 1) // POOL_GROUP
    out = parts[-1]
    for g in range(len(parts) - 2, -1, -1):
        out = jnp.where(lane == g, parts[g], out)
    return out


POOL_HALO = 16


def _pool_rows(u, halo, pos, w_pool, scale):
    ext = jnp.concatenate([halo, u], axis=0)
    hb = halo.shape[0]
    sums = ext
    parts = []
    span = 1
    for w in POOL_WINDOWS:
        while span < w:
            sums = sums + pltpu.roll(sums, span, 0)
            span *= 2
        cnt = jnp.minimum(w, pos + 1).astype(F32)
        parts.append(sums[hb:, :] / cnt - u)
    pooled = _pool_select(parts)
    return _dot(pooled.astype(BF16), w_pool) * scale


def _mix_out(x, mla, gla, pool, wo_ref):
    mix = jnp.concatenate([mla.astype(BF16), gla.astype(BF16), pool.astype(BF16)], axis=1)
    return x + _dot(mix, wo_ref[...])


def _xa_query(x1, gxa, wxq, gxq):
    q = _dot(_rms(x1, gxa).astype(BF16), wxq)
    return _head_rms(q, gxq, XA_HEADS, XA_HEAD, XA_HEAD) * (XA_HEAD ** -0.5)


def _xa_attend(qb, kb, vb):
    outs = []
    for h in range(XA_HEADS):
        sl = slice(h * XA_HEAD, (h + 1) * XA_HEAD)
        s = _dot_nt(qb[:, sl], kb[:, sl])
        e = jnp.exp(s - jnp.max(s, axis=-1, keepdims=True))
        p = e / jnp.sum(e, axis=-1, keepdims=True)
        outs.append(_dot(p.astype(BF16), vb[:, sl]))
    return jnp.concatenate(outs, axis=1).astype(BF16)


def _post_kernel(x_ref, mla_ref, gla_ref, u_ref, halo_ref, wp_ref, sc_ref, wo_ref, gxa_ref, wxq_ref, gxq_ref,
                 mk_ref, mv_ref, wxo_ref, o_ref, *, per_seq):
    tile = pl.program_id(0) % per_seq
    tm = u_ref.shape[0]
    halo = jnp.where(tile == 0, 0.0, halo_ref[...])
    pos = tile * tm + lax.broadcasted_iota(jnp.int32, (tm, 1), 0)
    pool = _pool_rows(u_ref[...], halo, pos, wp_ref[...], sc_ref[...])
    x1 = _mix_out(x_ref[...], mla_ref[...], gla_ref[...], pool, wo_ref)
    qb = _xa_query(x1, gxa_ref[...], wxq_ref[...], gxq_ref[...]).astype(BF16)
    o = _xa_attend(qb, mk_ref[...], mv_ref[...])
    o_ref[...] = x1 + _dot(o, wxo_ref[...])


def _post_prompt(x2, mla, gla, u, mk4, mv4, layer, lw, t, tm):
    m = x2.shape[0]
    per_seq = t // tm
    row = lambda i: (i, 0)
    fix = lambda i: (0, 0)
    mem = lambda i: (layer, i // per_seq, 0, 0)
    halo = lambda i: (jnp.maximum(i * (tm // POOL_HALO) - 1, 0), 0)
    sq = (D_MODEL, D_MODEL)
    return pl.pallas_call(
        functools.partial(_post_kernel, per_seq=per_seq),
        grid=(m // tm,),
        in_specs=[pl.BlockSpec((tm, D_MODEL), row), pl.BlockSpec((tm, MLA_OUT), row),
                  pl.BlockSpec((tm, GLA_OUT), row), pl.BlockSpec((tm, POOL_WIDTH), row),
                  pl.BlockSpec((POOL_HALO, POOL_WIDTH), halo), pl.BlockSpec((POOL_WIDTH, POOL_WIDTH), fix),
                  pl.BlockSpec((1, POOL_WIDTH), fix),
                  pl.BlockSpec(sq, fix), pl.BlockSpec((1, D_MODEL), fix), pl.BlockSpec(sq, fix),
                  pl.BlockSpec((1, D_MODEL), fix), pl.BlockSpec((None, None, N_MEM, D_MODEL), mem),
                  pl.BlockSpec((None, None, N_MEM, D_MODEL), mem), pl.BlockSpec(sq, fix)],
        out_specs=pl.BlockSpec((tm, D_MODEL), row),
        out_shape=jax.ShapeDtypeStruct((m, D_MODEL), F32),
        compiler_params=_cparams(("parallel",), 48),
        name="post_prompt",
    )(x2, mla, gla, u, u, lw["w_pool"], lw["pool_scale"], lw["w_out"], lw["norm_xa"], lw["w_xq"], lw["g_xq"],
      mk4, mv4, lw["w_xo"])


def _mlp_kernel(x_ref, g_ref, w1_ref, w2_ref, o_ref):
    x = x_ref[...]
    h = _rms(x, g_ref[...]).astype(BF16)
    a = jnp.maximum(_dot(h, w1_ref[...]), 0.0)
    o_ref[...] = x + _dot((a * a).astype(BF16), w2_ref[...])


def _mlp(x2, lw, tm):
    m = x2.shape[0]
    once = pl.Buffered(1)
    layer = lw["layer"]
    return pl.pallas_call(
        _mlp_kernel,
        grid=(m // tm,),
        in_specs=[pl.BlockSpec((tm, D_MODEL), lambda i: (i, 0)),
                  pl.BlockSpec((1, D_MODEL), lambda i: (0, 0)),
                  pl.BlockSpec((None, D_MODEL, D_FF), lambda i: (layer, 0, 0), pipeline_mode=once),
                  pl.BlockSpec((None, D_FF, D_MODEL), lambda i: (layer, 0, 0), pipeline_mode=once)],
        out_specs=pl.BlockSpec((tm, D_MODEL), lambda i: (i, 0)),
        out_shape=jax.ShapeDtypeStruct((m, D_MODEL), F32),
        compiler_params=_cparams(("parallel",), 56),
        name="mlp",
    )(x2, lw["norm_mlp"], lw["w_ff1"], lw["w_ff2"])


DEC_PAGES = 64
DEC_CHUNK = 8192
DEC_LHS_ROWS = MLA_HEADS * QK_NOPE + 16
DEC_SS_ROWS = 48


def _mla_decode_kernel(pt_ref, q_ref, k_ref, latn_ref, cos_ref, sin_ref, wukt_ref, gkn_ref, gcol_ref,
                       ssel_ref, wuv_ref, lat_hbm, krt_hbm, o_ref,
                       lat_buf, kr_buf, sem, m_sc, l_sc, acc_sc, lhs_sc, lb_sc, *, pp, nj, layer):
    t = pl.program_id(0)
    n_steps = pl.num_programs(0)
    j = t % nj
    slot = t % 2

    def page_copies(step, buf_slot):
        b = step // nj
        base = (step % nj) * pp
        copies = []
        for p in range(pp):
            page = pt_ref[b, base + p]
            copies.append(pltpu.make_async_copy(lat_hbm.at[page, layer], lat_buf.at[buf_slot, p],
                                                sem.at[buf_slot, 0]))
            copies.append(pltpu.make_async_copy(krt_hbm.at[page, layer],
                                                kr_buf.at[buf_slot, :, pl.ds(p * PAGE_SIZE, PAGE_SIZE)],
                                                sem.at[buf_slot, 1]))
        return copies

    def start_all(copies):
        for i, cp in enumerate(copies):
            cp.start(priority=(i // 2 + i) % 2)

    start_all(page_copies(jnp.minimum(t + 1, n_steps - 1), 1 - slot))

    @pl.when(t == 0)
    def _():
        start_all(page_copies(t, slot))
        lhs_sc[0:MLA_HEADS * QK_NOPE, :] = wukt_ref[...]

    q16 = q_ref[...].astype(F32)

    @pl.when(j == 0)
    def _():
        m_sc[...] = jnp.full(m_sc.shape, -jnp.inf, F32)
        l_sc[...] = jnp.zeros(l_sc.shape, F32)
        acc_sc[...] = jnp.zeros(acc_sc.shape, F32)
        qn = (q16 * gkn_ref[...])[:, 0:QK_NOPE]
        tiled = jnp.concatenate([qn] * MLA_HEADS, axis=1)
        row = lax.broadcasted_iota(jnp.int32, tiled.shape, 0)
        lane = lax.broadcasted_iota(jnp.int32, tiled.shape, 1) // QK_NOPE
        qbd = jnp.where(row == lane, tiled, 0.0).astype(BF16)
        lhs_sc[MLA_HEADS * QK_NOPE:, :] = _dot(qbd, wukt_ref[...]).astype(BF16)

    for cp in page_copies(t, slot):
        cp.wait()

    qr = q16[:, QK_NOPE:QK_HEAD].astype(BF16)
    lhs = lhs_sc[...]
    gcol = gcol_ref[...]
    ssel = ssel_ref[...]
    half = QK_ROPE // 2
    nf = MLA_HEADS * QK_NOPE
    for p in range(pp):
        lb_sc[p * PAGE_SIZE:(p + 1) * PAGE_SIZE, :] = lat_buf[slot, p].astype(BF16)

    ck = DEC_CHUNK
    scores = []
    for c0 in range(0, pp * PAGE_SIZE, ck):
        res = _dot_nt(lhs, lb_sc[c0:c0 + ck, :])
        sq = res[0:nf, :] * res[0:nf, :]
        sn = res[nf:nf + 8, :]
        krt = kr_buf[slot, :, c0:c0 + ck]
        parts = [sq[h * QK_NOPE:(h + 1) * QK_NOPE, :].reshape(8, 8, ck).sum(axis=0) for h in range(MLA_HEADS)]
        parts.append((krt * krt).reshape(4, 8, ck).sum(axis=0))
        parts.append(jnp.zeros((8, ck), F32))
        p_hi, p_lo = _split_bf16(jnp.concatenate(parts, axis=0))
        ss = _dot(ssel, p_hi) + _dot(ssel, p_lo)
        krg = krt * gcol
        cs = cos_ref[:, c0:c0 + ck]
        sn_ = sin_ref[:, c0:c0 + ck]
        x1, x2 = krg[0:half, :], krg[half:, :]
        roped = jnp.concatenate([x1 * cs - x2 * sn_, x2 * cs + x1 * sn_], axis=0)
        sr = _dot(qr, roped.astype(BF16))[0:8, :]
        scores.append((sn + sr) * lax.rsqrt(ss * (1.0 / QK_HEAD) + EPS))
    s_all = jnp.concatenate(scores, axis=1)

    m_old = m_sc[...]
    m_new = jnp.maximum(m_old, jnp.max(s_all, axis=1, keepdims=True))
    alpha = jnp.exp(m_old - m_new)
    p = jnp.exp(s_all - m_new[:, 0:1])
    l_sc[...] = alpha * l_sc[...] + jnp.sum(p, axis=1, keepdims=True)
    acc_sc[...] = alpha[:, 0:1] * acc_sc[...] + _dot(p.astype(BF16), lb_sc[...])
    m_sc[...] = m_new

    @pl.when(t == n_steps - 1)
    def _():
        for cp in page_copies(t, 1 - slot):
            cp.wait()

    @pl.when(j == nj - 1)
    def _():
        s_self = jnp.sum((q16 * k_ref[...].astype(F32))[0:8, :], axis=1, keepdims=True)
        m_o = m_sc[...]
        m_n = jnp.maximum(m_o, s_self)
        al = jnp.exp(m_o - m_n)
        ps = jnp.exp(s_self - m_n)
        l = al * l_sc[...] + ps
        acc = al[:, 0:1] * acc_sc[...] + ps[:, 0:1] * latn_ref[...]
        o_lat = acc / l[:, 0:1]
        res = _dot(o_lat.astype(BF16), wuv_ref[...])
        row = lax.broadcasted_iota(jnp.int32, res.shape, 0)
        lane = lax.broadcasted_iota(jnp.int32, res.shape, 1) // V_HEAD
        o_ref[...] = jnp.sum(jnp.where(row == lane, res, 0.0), axis=0, keepdims=True).astype(o_ref.dtype)


def _mla_decode(page_table, q16, k16, lat_new3, cache_lat, cache_krt, layer, lw, tabs):
    pp = DEC_PAGES
    n = q16.shape[0]
    n_pages = page_table.shape[1]
    nj = n_pages // pp
    cos_t, sin_t = tabs
    fix2 = lambda t, pt: (0, 0)
    seq3 = lambda t, pt: (t // nj, 0, 0)
    tab = lambda t, pt: (0, t % nj)
    weights = [lw["w_uk_t"], lw["g_k_nope"], lw["g_k_rope_col"], lw["ss_select"], lw["w_uv"]]
    in_specs = [pl.BlockSpec((None, 16, HEAD_PAD), seq3), pl.BlockSpec((None, 16, HEAD_PAD), seq3),
                pl.BlockSpec((None, 1, KV_RANK), seq3),
                pl.BlockSpec((QK_ROPE // 2, pp * PAGE_SIZE), tab),
                pl.BlockSpec((QK_ROPE // 2, pp * PAGE_SIZE), tab)]
    in_specs += [pl.BlockSpec(w.shape, fix2) for w in weights]
    in_specs += [pl.BlockSpec(memory_space=pl.ANY), pl.BlockSpec(memory_space=pl.ANY)]
    grid_spec = pltpu.PrefetchScalarGridSpec(
        num_scalar_prefetch=1,
        grid=(n * nj,),
        in_specs=in_specs,
        out_specs=pl.BlockSpec((None, 1, MLA_OUT), seq3),
        scratch_shapes=[pltpu.VMEM((2, pp, PAGE_SIZE, KV_RANK), F32),
                        pltpu.VMEM((2, QK_ROPE, pp * PAGE_SIZE), F32),
                        pltpu.SemaphoreType.DMA((2, 2)),
                        pltpu.VMEM((8, 128), F32), pltpu.VMEM((8, 128), F32), pltpu.VMEM((8, KV_RANK), F32),
                        pltpu.VMEM((DEC_LHS_ROWS, KV_RANK), BF16),
                        pltpu.VMEM((pp * PAGE_SIZE, KV_RANK), BF16)])
    return pl.pallas_call(
        functools.partial(_mla_decode_kernel, pp=pp, nj=nj, layer=layer),
        grid_spec=grid_spec,
        out_shape=jax.ShapeDtypeStruct((n, 1, MLA_OUT), F32),
        compiler_params=_cparams(("arbitrary",), 52),
        name="mla_decode",
    )(page_table, q16, k16, lat_new3, cos_t, sin_t, *weights, cache_lat, cache_krt)


def _dec_mix_kernel(q_ref, k_ref, la_ref, v_ref, gr_ref, u_ref, s_ref, pb_ref, gout_ref, wp_ref, sc_ref,
                    s_o, gla_o, pool_o, pb_o, *, past):
    a = jnp.exp(la_ref[...])
    v4 = v_ref[...]
    vexp = jnp.concatenate([jnp.broadcast_to(v4[h:h + 1, :], (GLA_DK, GLA_DV)) for h in range(GLA_HEADS)],
                           axis=0)
    s_new = a * s_ref[...] + k_ref[...] * vexp
    s_o[...] = s_new
    prod = (q_ref[...] * (GLA_DK ** -0.5)) * s_new
    gr = gr_ref[...]
    gout = gout_ref[...]
    for h in range(GLA_HEADS):
        o = jnp.sum(prod[h * GLA_DK:(h + 1) * GLA_DK, :], axis=0, keepdims=True)
        on = _rms(o, gout)
        g = gr[h:h + 1, :]
        gla_o[h:h + 1, :] = (on * (g * _sigmoid(g))).astype(gla_o.dtype)

    st = pb_ref[...]
    u = u_ref[...]
    ridx = lax.broadcasted_iota(jnp.int32, st.shape, 0)
    parts = []
    for w in POOL_WINDOWS:
        tot = u + jnp.sum(jnp.where(ridx >= POOL_BUF + 1 - w, st, 0.0), axis=0, keepdims=True)
        parts.append(tot / float(min(w, past + 1)) - u)
    pooled = jnp.broadcast_to(_pool_select(parts), (8, POOL_WIDTH)).astype(BF16)
    y = _dot(pooled, wp_ref[...])[0:1, :] * sc_ref[...]
    pool_o[...] = y.astype(pool_o.dtype)
    pb_o[0:POOL_BUF - 1, :] = st[1:POOL_BUF, :]
    pb_o[POOL_BUF - 1:POOL_BUF, :] = u


def _dec_mix(gq, gk, la, gv, gr, u, state_gla4, state_pool, layer, lw, past):
    n = gq.shape[0]
    col = lambda x: x.reshape(n, GLA_QK, 1)
    b3 = lambda b: (b, 0, 0)
    fix = lambda b: (0, 0)
    return pl.pallas_call(
        functools.partial(_dec_mix_kernel, past=past),
        grid=(n,),
        in_specs=[pl.BlockSpec((None, GLA_QK, 1), b3)] * 3
        + [pl.BlockSpec((None, GLA_HEADS, GLA_DV), b3)] * 2
        + [pl.BlockSpec((None, 1, POOL_WIDTH), b3),
           pl.BlockSpec((None, None, GLA_QK, GLA_DV), lambda b: (b, layer, 0, 0)),
           pl.BlockSpec((None, None, POOL_BUF, POOL_WIDTH), lambda b: (b, layer, 0, 0)),
           pl.BlockSpec((1, GLA_DV), fix), pl.BlockSpec((POOL_WIDTH, POOL_WIDTH), fix),
           pl.BlockSpec((1, POOL_WIDTH), fix)],
        out_specs=[pl.BlockSpec((None, GLA_QK, GLA_DV), b3), pl.BlockSpec((None, GLA_HEADS, GLA_DV), b3),
                   pl.BlockSpec((None, 1, POOL_WIDTH), b3), pl.BlockSpec((None, POOL_BUF, POOL_WIDTH), b3)],
        out_shape=[jax.ShapeDtypeStruct((n, GLA_QK, GLA_DV), F32),
                   jax.ShapeDtypeStruct((n, GLA_HEADS, GLA_DV), F32),
                   jax.ShapeDtypeStruct((n, 1, POOL_WIDTH), F32),
                   jax.ShapeDtypeStruct((n, POOL_BUF, POOL_WIDTH), F32)],
        compiler_params=_cparams(("parallel",), 32),
        name="dec_mix",
    )(col(gq), col(gk), col(la), gv.reshape(n, GLA_HEADS, GLA_DV), gr.reshape(n, GLA_HEADS, GLA_DV),
      u.reshape(n, 1, POOL_WIDTH), state_gla4, state_pool, lw["g_gla_out64"], lw["w_pool"], lw["pool_scale"])


def _post_a_kernel(x_ref, mla_ref, gla_ref, pool_ref, wo_ref, gxa_ref, wxq_ref, gxq_ref, x1_ref, q_ref):
    x1 = _mix_out(x_ref[...], mla_ref[...], gla_ref[...], pool_ref[...], wo_ref)
    x1_ref[...] = x1
    q_ref[...] = _xa_query(x1, gxa_ref[...], wxq_ref[...], gxq_ref[...])


def _post_a(x2, mla, gla, pool, lw):
    m = x2.shape[0]
    return pl.pallas_call(
        _post_a_kernel,
        out_shape=[jax.ShapeDtypeStruct((m, D_MODEL), F32), jax.ShapeDtypeStruct((m, D_MODEL), F32)],
        compiler_params=pltpu.CompilerParams(vmem_limit_bytes=32 * MIB),
        name="post_a",
    )(x2, mla, gla, pool, lw["w_out"], lw["norm_xa"], lw["w_xq"], lw["g_xq"])


XA_HALVES = XA_HEAD // 128
XA_ROWS = XA_HALVES * XA_HEADS


def _xa_decode_kernel(q_ref, mk_ref, mv_ref, o_ref):
    k3 = mk_ref[...].reshape(N_MEM, XA_ROWS, 128)
    v3 = mv_ref[...].reshape(N_MEM, XA_ROWS, 128)
    s = jnp.sum(k3 * q_ref[...][None], axis=-1, keepdims=True)
    s = s[:, 0:XA_HEADS] + s[:, XA_HEADS:XA_ROWS]
    s = jnp.concatenate([s, s], axis=1)
    e = jnp.exp(s - jnp.max(s, axis=0, keepdims=True))
    p = e / jnp.sum(e, axis=0, keepdims=True)
    o_ref[...] = jnp.sum(p * v3, axis=0)


def _xa_decode(q3, cache_k, cache_v, layer):
    n = q3.shape[0]
    depth = cache_k.shape[1]
    view = lambda c: c.reshape(n, depth, N_MEM, XA_HEADS, XA_HALVES, 128).transpose(0, 1, 2, 4, 3, 5).reshape(
        n, depth, N_MEM * XA_ROWS, 128)
    q8 = q3.reshape(n, XA_HEADS, XA_HALVES, 128).transpose(0, 2, 1, 3).reshape(n, XA_ROWS, 128)
    b3 = lambda b: (b, 0, 0)
    mem = lambda b: (b, layer, 0, 0)
    o = pl.pallas_call(
        _xa_decode_kernel,
        grid=(n,),
        in_specs=[pl.BlockSpec((None, XA_ROWS, 128), b3),
                  pl.BlockSpec((None, None, N_MEM * XA_ROWS, 128), mem),
                  pl.BlockSpec((None, None, N_MEM * XA_ROWS, 128), mem)],
        out_specs=pl.BlockSpec((None, XA_ROWS, 128), b3),
        out_shape=jax.ShapeDtypeStruct((n, XA_ROWS, 128), F32),
        compiler_params=_cparams(("parallel",), 32),
        name="xa_decode",
    )(q8, view(cache_k), view(cache_v))
    return o.reshape(n, XA_HALVES, XA_HEADS, 128).transpose(0, 2, 1, 3).reshape(n, 1, D_MODEL)


def _post_b_kernel(x1_ref, o_ref, wxo_ref, x2_ref):
    x2_ref[...] = x1_ref[...] + _dot(o_ref[...].astype(BF16), wxo_ref[...])


def _post_b(x1, o, lw):
    return pl.pallas_call(
        _post_b_kernel,
        out_shape=jax.ShapeDtypeStruct(x1.shape, F32),
        compiler_params=pltpu.CompilerParams(vmem_limit_bytes=32 * MIB),
        name="post_b",
    )(x1, o, lw["w_xo"])


def _rope_tables_rows(pos):
    half = QK_ROPE // 2
    inv = ROPE_THETA ** (-jnp.arange(half, dtype=F32) / half)
    ang = pos.astype(F32)[:, None] * inv[None, :]
    cos, sin = jnp.cos(ang), jnp.sin(ang)
    n = pos.shape[0]
    c = jnp.concatenate([jnp.ones((n, QK_NOPE), F32), cos, cos, jnp.ones((n, HEAD_PAD - QK_HEAD), F32)], axis=1)
    s = jnp.concatenate([jnp.zeros((n, QK_NOPE), F32), sin, sin, jnp.zeros((n, HEAD_PAD - QK_HEAD), F32)],
                        axis=1)
    return c, s


def _rope_tables_cols(pos):
    half = QK_ROPE // 2
    inv = ROPE_THETA ** (-jnp.arange(half, dtype=F32) / half)
    ang = pos.astype(F32)[:, None] * inv[None, :]
    return jnp.cos(ang).T, jnp.sin(ang).T


def _pad_heads(w, width):
    pad = [(0, 0)] * (w.ndim - 1) + [(0, HEAD_PAD - width)]
    w = jnp.pad(w, pad)
    return w.reshape(w.shape[:-2] + (w.shape[-2] * HEAD_PAD,))


def _layer_weights(l, p):
    w_in = p["w_in"][l]
    sizes = (Q_RANK, KV_RANK, QK_ROPE, GLA_QK, GLA_QK, GLA_OUT, GLA_GATE_RANK, GLA_OUT, POOL_WIDTH)
    offs = [0]
    for s in sizes:
        offs.append(offs[-1] + s)
    piece = lambda i: w_in[:, offs[i]:offs[i + 1]]
    w_in_p = jnp.concatenate([piece(0), piece(1), piece(3), piece(4), piece(5), piece(7), piece(8), piece(2),
                              piece(6), jnp.zeros((D_MODEL, IN_PAD - sum(sizes)), F32)], axis=1)
    g_q = _pad_heads(jnp.broadcast_to(p["g_qk_q"][l], (MLA_HEADS, QK_HEAD)), QK_HEAD)
    g_k = _pad_heads(jnp.broadcast_to(p["g_qk_k"][l], (MLA_HEADS, QK_HEAD)), QK_HEAD)
    place = jnp.zeros((128, QK_W), F32)
    j = jnp.arange(QK_ROPE)
    for h in range(MLA_HEADS):
        place = place.at[j, h * HEAD_PAD + QK_NOPE + j].set(1.0)
    w_gate = jnp.zeros((128, GLA_QK), F32).at[QK_ROPE:QK_ROPE + GLA_GATE_RANK].set(p["w_gate_up"][l])
    half = QK_ROPE // 2

    def partner(w):
        w4 = w.reshape(w.shape[:-1] + (MLA_HEADS, HEAD_PAD))
        x1 = w4[..., QK_NOPE:QK_NOPE + half]
        x2 = w4[..., QK_NOPE + half:QK_HEAD]
        out = jnp.zeros_like(w4).at[..., QK_NOPE:QK_NOPE + half].set(x2).at[..., QK_NOPE + half:QK_HEAD].set(x1)
        return out.reshape(w.shape)

    sign = jnp.tile(jnp.concatenate([jnp.zeros((QK_NOPE,), F32), -jnp.ones((half,), F32), jnp.ones((half,), F32),
                                     jnp.zeros((HEAD_PAD - QK_HEAD,), F32)]), MLA_HEADS)
    w_uq_pad = _pad_heads(p["w_uq"][l], QK_HEAD)
    w_uv2 = p["w_uv"][l].reshape(KV_RANK, MLA_OUT)
    place_both = jnp.concatenate([place, partner(place) * sign], axis=1)
    wp = p["w_pool"][l]
    w_pool = jnp.zeros((POOL_WIDTH, POOL_WIDTH), F32)
    for g in range(len(POOL_WINDOWS)):
        w_pool = w_pool.at[g * POOL_GROUP:(g + 1) * POOL_GROUP, g * POOL_GROUP:(g + 1) * POOL_GROUP].set(wp[g])
    blk = jnp.arange(GLA_OUT) // GLA_DV
    row = lambda v: v.reshape(1, -1).astype(F32)
    bf = lambda v: v.astype(BF16)
    w_uk2 = p["w_uk"][l].reshape(KV_RANK, MLA_HEADS * QK_NOPE)
    g_k_nope = jnp.zeros((1, HEAD_PAD), F32).at[0, :QK_NOPE].set(p["g_qk_k"][l][:QK_NOPE])
    col = jnp.arange(DEC_SS_ROWS)[None, :] // 8
    hrow = jnp.arange(8)[:, None]
    ss_select = ((hrow < MLA_HEADS) & ((col == hrow) | (col == MLA_HEADS))).astype(F32)
    return {
        "norm_mix": row(p["norm_mix"][l]), "w_in": bf(w_in_p), "g_q_lat": row(p["g_q_lat"][l]),
        "w_uq2": bf(jnp.concatenate([w_uq_pad, partner(w_uq_pad) * sign], axis=1)),
        "g_kv_lat": row(p["g_kv_lat"][l]),
        "w_ukv": bf(jnp.concatenate([_pad_heads(p["w_uk"][l], QK_NOPE), w_uv2], axis=1)), "w_uv": bf(w_uv2),
        "g_qk_q": row(g_q), "g_qk_k": row(g_k), "place": bf(jnp.tile(place_both, (2, 1))), "w_gate": bf(w_gate),
        "g_q_rot": row(partner(g_q)), "g_k_rot": row(partner(g_k)),
        "b_gate": row(p["b_gate"][l]),
        "e64": bf((blk[:, None] == blk[None, :]).astype(F32)),
        "g_gla_out": row(jnp.tile(p["g_gla_out"][l], GLA_HEADS)), "g_gla_out64": row(p["g_gla_out"][l]),
        "w_pool": bf(w_pool), "pool_scale": row(p["pool_scale"][l]),
        "w_out": bf(p["w_out"][l]), "norm_xa": row(p["norm_xa"][l]), "w_xq": bf(p["w_xq"][l]),
        "g_xq": row(jnp.tile(p["g_xq"][l], XA_HEADS)), "w_xo": bf(p["w_xo"][l]),
        "norm_mlp": row(p["norm_mlp"][l]), "w_ff1": p["w_ff1_bf16"], "w_ff2": p["w_ff2_bf16"], "layer": l,
        "w_uk_t": bf(w_uk2.T), "g_k_nope": g_k_nope,
        "g_k_rope_col": jnp.broadcast_to(p["g_qk_k"][l][QK_NOPE:, None], (QK_ROPE, DEC_CHUNK)).astype(F32),
        "ss_select": bf(ss_select),
    }


def _prompt_layer(x2, layer, mk4, mv4, lw, ropes, b, t):
    lat, kr, q, k, v, gq, gk, gv, la, gr, u = _in_proj(x2, lw, ropes, ROW_TILE)
    r3 = lambda a: a.reshape(b, t, a.shape[-1])
    mla = _mla_prompt(r3(q), r3(k), r3(v), ROW_TILE).reshape(b * t, MLA_OUT)
    gla, st = _gla_prompt(r3(gq), r3(gk), r3(gv), r3(la), r3(gr), lw)
    x2 = _post_prompt(x2, mla, gla.reshape(b * t, GLA_OUT), u, mk4, mv4, layer, lw, t, ROW_TILE)
    x2 = _mlp(x2, lw, ROW_TILE)
    gla_state = jnp.stack([st[:, h * GLA_DV:(h + 1) * GLA_DV, h * GLA_DK:(h + 1) * GLA_DK]
                           for h in range(GLA_HEADS)], axis=1).transpose(0, 1, 3, 2)
    outs = (lat.reshape(b, t, KV_RANK), kr.reshape(b, t, QK_ROPE), gla_state,
            r3(u)[:, t - POOL_BUF:])
    return x2, outs


def _sample_layer(x2, layer, lw, ropes, tabs, page_table, cache_lat, cache_krt, state_gla4, state_pool,
                  cache_mk, cache_mv, past):
    n = x2.shape[0]
    lat, kr, q, k, v, gq, gk, gv, la, gr, u = _in_proj(x2, lw, ropes, n)
    pad16 = lambda a: jnp.pad(a.reshape(n, MLA_HEADS, HEAD_PAD), ((0, 0), (0, 16 - MLA_HEADS), (0, 0)))
    mla = _mla_decode(page_table, pad16(q), pad16(k), lat.reshape(n, 1, KV_RANK), cache_lat, cache_krt, layer,
                      lw, tabs).reshape(n, MLA_OUT)
    s_new, gla, pool, pb = _dec_mix(gq, gk, la, gv, gr, u, state_gla4, state_pool, layer, lw, past)
    x1, qx = _post_a(x2, mla, gla.reshape(n, GLA_OUT), pool.reshape(n, POOL_WIDTH), lw)
    ox = _xa_decode(qx.reshape(n, 1, D_MODEL), cache_mk, cache_mv, layer).reshape(n, D_MODEL)
    x2 = _mlp(_post_b(x1, ox, lw), lw, n)
    outs = (lat.reshape(n, 1, KV_RANK), kr.reshape(n, 1, QK_ROPE),
            s_new.reshape(n, GLA_HEADS, GLA_DK, GLA_DV), pb)
    return x2, outs


def kernel(x_prompt, x_sample, mem_prompt, cache_mla_latent, cache_mla_krope, state_gla, state_pool,
           cache_mem_k, cache_mem_v, page_table, norm_mix, w_in, g_q_lat, w_uq, g_kv_lat, w_uk, w_uv,
           g_qk_q, g_qk_k, w_gate_up, b_gate, g_gla_out, w_pool, pool_scale, w_out, norm_xa, norm_mem,
           w_xq, w_xk, w_xv, g_xq, g_xk, w_xo, norm_mlp, w_ff1, w_ff2):
    params = dict(norm_mix=norm_mix, w_in=w_in, g_q_lat=g_q_lat, w_uq=w_uq, g_kv_lat=g_kv_lat, w_uk=w_uk,
                  w_uv=w_uv, g_qk_q=g_qk_q, g_qk_k=g_qk_k, w_gate_up=w_gate_up, b_gate=b_gate,
                  g_gla_out=g_gla_out, w_pool=w_pool, pool_scale=pool_scale, w_out=w_out, norm_xa=norm_xa,
                  norm_mem=norm_mem, w_xq=w_xq, w_xk=w_xk, w_xv=w_xv, g_xq=g_xq, g_xk=g_xk, w_xo=w_xo,
                  norm_mlp=norm_mlp, w_ff1_bf16=w_ff1.astype(BF16), w_ff2_bf16=w_ff2.astype(BF16))
    b, t, _ = x_prompt.shape
    n, dec_seq, _ = x_sample.shape
    depth = norm_mix.shape[0]
    n_pages = page_table.shape[1]
    past = n_pages * PAGE_SIZE

    ropes_p = _rope_tables_rows(jnp.arange(t, dtype=jnp.int32))
    ropes_s = _rope_tables_rows(jnp.full((n,), past, jnp.int32))
    tabs = _rope_tables_cols(jnp.arange(past, dtype=jnp.int32))

    yp = x_prompt.reshape(b * t, D_MODEL)
    ys = x_sample.reshape(n * dec_seq, D_MODEL)
    mem2 = mem_prompt.reshape(b * N_MEM, D_MODEL)
    state_gla4 = state_gla.reshape(n, depth, GLA_QK, GLA_DV)
    cache_krt = cache_mla_krope.transpose(0, 1, 3, 2)

    mk, mv, mkb, mvb = _memory_kv(mem2, norm_mem.reshape(depth, 1, D_MODEL), w_xk.astype(BF16),
                                  w_xv.astype(BF16), jnp.tile(g_xk, (1, XA_HEADS)).reshape(depth, 1, D_MODEL))
    mk4 = mkb.reshape(depth, b, N_MEM, D_MODEL)
    mv4 = mvb.reshape(depth, b, N_MEM, D_MODEL)

    p_outs, s_outs = [], []
    for l in range(depth):
        lw = _layer_weights(l, params)
        yp, po = _prompt_layer(yp, l, mk4, mv4, lw, ropes_p, b, t)
        ys, so = _sample_layer(ys, l, lw, ropes_s, tabs, page_table, cache_mla_latent, cache_krt,
                               state_gla4, state_pool, cache_mem_k, cache_mem_v, past)
        p_outs.append(po)
        s_outs.append(so)

    stack = lambda outs, i: jnp.stack([o[i] for o in outs], axis=1)
    return (yp.reshape(b, t, D_MODEL), ys.reshape(n, dec_seq, D_MODEL),
            stack(p_outs, 0), stack(p_outs, 1), stack(p_outs, 2), stack(p_outs, 3),
            mk.reshape(b, depth, N_MEM, XA_HEADS, XA_HEAD), mv.reshape(b, depth, N_MEM, XA_HEADS, XA_HEAD),
            stack(s_outs, 0), stack(s_outs, 1), stack(s_outs, 2), stack(s_outs, 3))
```
